```python
import jax, jax.numpy as jnp
from jax import lax
import numpy as np

D_MODEL = 2048
BATCH = 2
SEQ = 16384
DEPTH = 2

CTX_LEN = 256
GRID_W = 64

RWKV_HEAD_DIM = 64
RWKV_W = 3 * D_MODEL // 8
RWKV_HEADS = RWKV_W // RWKV_HEAD_DIM
DECAY_LORA = 64
AAA_LORA = 64
GATE_LORA = 128
RWKV_COLS = 3 * RWKV_W + DECAY_LORA + AAA_LORA + GATE_LORA
CONV_W = D_MODEL // 4
CONV_COLS = 3 * CONV_W
RET_HEAD_DIM = 128
RET_W = D_MODEL - RWKV_W - CONV_W
RET_HEADS = RET_W // RET_HEAD_DIM
RET_COLS = 4 * RET_W
RET_CHUNK = 128
IN_COLS = RWKV_COLS + CONV_COLS + RET_COLS
D_FF = ((8 * D_MODEL // 3 + 255) // 256) * 256
ROPE_BASE = 10000.0
ROPE_PAIRS = RET_HEAD_DIM // 4
NORM_EPS = 1e-6
RWKV_GN_EPS = 64e-5

kernel_name = "hybrid_rwkv7_shortconv_retention_dit"


def _rmsnorm(x, g):
    x32 = x.astype(jnp.float32)
    return x32 * lax.rsqrt(jnp.mean(x32 * x32, axis=-1, keepdims=True) + NORM_EPS) * g


def _head_norm(y, eps):
    y = y.astype(jnp.float32)
    mu = jnp.mean(y, axis=-1, keepdims=True)
    var = jnp.mean(jnp.square(y - mu), axis=-1, keepdims=True)
    return (y - mu) * lax.rsqrt(var + eps)


def _dwconv3(z, w):
    zp = jnp.pad(z, ((0, 0), (1, 1), (0, 0)))
    return zp[:, :-2] * w[:, 0] + zp[:, 1:-1] * w[:, 1] + zp[:, 2:] * w[:, 2]


def _token_lerp(z, mu):
    zp = jnp.pad(z, ((0, 0), (1, 1), (0, 0)))
    nb = 0.5 * (zp[:, :-2] + zp[:, 2:])
    return z + (nb - z) * mu


def _axial_rope_tables(row, col):
    freqs = 1.0 / (ROPE_BASE ** (jnp.arange(ROPE_PAIRS, dtype=jnp.float32) / ROPE_PAIRS))
    ang = jnp.concatenate([row[:, None] * freqs, col[:, None] * freqs], axis=-1)
    return jnp.cos(ang), jnp.sin(ang)


def _apply_rope(u, cos, sin):
    half = RET_HEAD_DIM // 2
    u1, u2 = u[..., :half], u[..., half:]
    return jnp.concatenate([u1 * cos - u2 * sin, u1 * sin + u2 * cos], axis=-1)


def _rwkv_features(z, mu, w0, w_up, a0, a_up, g_up, k_k, k_a):
    z = _token_lerp(z, mu)
    o = 3 * RWKV_W
    r, k, v, wd, ad, gd = jnp.split(
        z, [RWKV_W, 2 * RWKV_W, o, o + DECAY_LORA, o + DECAY_LORA + AAA_LORA], axis=-1)
    b, t = z.shape[:2]
    kk = (k * k_k).reshape(b, t, RWKV_HEADS, RWKV_HEAD_DIM)
    kk = kk / jnp.maximum(jnp.sqrt(jnp.sum(kk * kk, axis=-1, keepdims=True)), 1e-12)
    kk = kk.reshape(b, t, RWKV_W)
    tw = jnp.tanh(wd)
    decays, keys, rates = [], [], []
    for d in range(2):
        w_log = -jax.nn.softplus(-(w0[d] + tw @ w_up[d])) - 0.5
        decays.append(jnp.exp(-jnp.exp(w_log)))
        a = jax.nn.sigmoid(a0[d] + ad @ a_up[d])
        rates.append(a)
        keys.append(k * (1.0 + (a - 1.0) * k_a))
    g = jax.nn.sigmoid(gd) @ g_up
    return r, v, kk, decays, keys, rates, g


def _wkv7_scan(r, w, k, v, kk, a, s0, reverse):
    b, t = r.shape[:2]

    def tm(u):
        return u.astype(jnp.float32).reshape(b, t, RWKV_HEADS, RWKV_HEAD_DIM).transpose(1, 0, 2, 3)

    xs = (tm(r), tm(w), tm(k), tm(v), tm(-kk), tm(kk * a))

    def step(s, inp):
        r_t, w_t, k_t, v_t, na_t, b_t = inp
        sa = jnp.einsum('bhij,bhj->bhi', s, na_t)
        s = s * w_t[:, :, None, :] + sa[..., None] * b_t[:, :, None, :] + v_t[..., None] * k_t[:, :, None, :]
        return s, jnp.einsum('bhij,bhj->bhi', s, r_t)

    s, ys = lax.scan(step, s0, xs, reverse=reverse)
    return ys.transpose(1, 0, 2, 3), s


def _rwkv_bi(feats, s0_f, s0_b):
    r, v, kk, decays, keys, rates, _ = feats
    y_f, s_f = _wkv7_scan(r, decays[0], keys[0], v, kk, rates[0], s0_f, False)
    y_b, s_b = _wkv7_scan(r, decays[1], keys[1], v, kk, rates[1], s0_b, True)
    return y_f + y_b, s_f, s_b


def _rwkv_out(y, feats, r_k, lnx_g, lnx_b):
    r, v, _, _, keys, _, g = feats
    b, t = r.shape[:2]
    shp = (b, t, RWKV_HEADS, RWKV_HEAD_DIM)
    k_mid = 0.5 * (keys[0] + keys[1])
    bonus = jnp.sum(r.reshape(shp) * k_mid.reshape(shp) * r_k, axis=-1, keepdims=True) * v.reshape(shp)
    out = _head_norm(y, RWKV_GN_EPS).reshape(b, t, RWKV_W) * lnx_g + lnx_b + bonus.reshape(b, t, RWKV_W)
    return out * g


def _conv_mix(z, conv_w):
    gate_b, gate_c, h = jnp.split(z, 3, axis=-1)
    return gate_b * _dwconv3(gate_c * h, conv_w)


def _ret_prepare(z, rope):
    q, k, v, g = jnp.split(z, 4, axis=-1)
    b, n = z.shape[:2]

    def heads(u):
        return u.reshape(b, n, RET_HEADS, RET_HEAD_DIM).transpose(0, 2, 1, 3)

    q, k, v = heads(q) * (RET_HEAD_DIM ** -0.5), heads(k), heads(v)
    if rope is not None:
        q, k = _apply_rope(q, *rope), _apply_rope(k, *rope)
    return q, k, v, g


def _retention_scan(q, k, v, log_gamma, r0, strict):
    b, h, n, _ = q.shape
    nc = n // RET_CHUNK

    def chunks(u):
        return u.astype(jnp.float32).reshape(b, h, nc, RET_CHUNK, u.shape[-1]).transpose(2, 0, 1, 3, 4)

    idx = jnp.arange(RET_CHUNK, dtype=jnp.float32)
    diff = idx[:, None] - idx[None, :]
    keep = diff > 0 if strict else diff >= 0
    d_in = jnp.where(keep, jnp.exp(log_gamma[:, None, None] * jnp.maximum(diff, 0.0)), 0.0)
    xi = jnp.exp(log_gamma[:, None] * (idx + 1.0))[..., None]
    zeta = jnp.exp(log_gamma[:, None] * (RET_CHUNK - 1.0 - idx))[..., None]
    g_chunk = jnp.exp(log_gamma * RET_CHUNK)[:, None, None]

    def step(r, inp):
        qi, ki, vi = inp
        s = jnp.einsum('bhnd,bhmd->bhnm', qi, ki) * d_in
        y = jnp.einsum('bhnm,bhmv->bhnv', s, vi) + jnp.einsum('bhnd,bhdv->bhnv', qi, r) * xi
        r = r * g_chunk + jnp.einsum('bhmd,bhmv->bhdv', ki * zeta, vi)
        return r, y

    r, ys = lax.scan(step, r0, (chunks(q), chunks(k), chunks(v)))
    return ys.transpose(1, 2, 0, 3, 4).reshape(b, h, n, -1), r


def _retention_bi(q, k, v, r0_f, r0_b, lg_f, lg_b):
    flip = lambda u: jnp.flip(u, axis=2)
    y_f, r_f = _retention_scan(q, k, v, lg_f, r0_f, False)
    y_b, r_b = _retention_scan(flip(q), flip(k), flip(v), lg_b, r0_b, True)
    return y_f + flip(y_b), r_f, r_b


def _ret_out(y, g, ret_norm_g):
    b, h, n, d = y.shape
    yn = _head_norm(y, NORM_EPS).transpose(0, 2, 1, 3).reshape(b, n, h * d) * ret_norm_g
    return jax.nn.silu(g) * yn


def _conv_ffn(h, up, cw, down):
    u = _dwconv3(h @ up, cw)
    gt, vl = jnp.split(u, 2, axis=-1)
    return (jax.nn.silu(gt) * vl) @ down


def setup_inputs(seed: int = 0) -> dict:
    key = jax.random.key(seed)
    ks = iter(jax.random.split(key, 32))
    f32 = jnp.float32
    L, D = DEPTH, D_MODEL

    def nrm(shape, s):
        return s * jax.random.normal(next(ks), shape, f32)

    return {
        "x": nrm((BATCH, SEQ, D), 1.0),
        "c": nrm((BATCH, D), 1.0),
        "ctx": nrm((BATCH, CTX_LEN, D), 1.0),
        "c_ctx": nrm((D,), 1.0),
        "mod_w": nrm((L, D, 6 * D), 0.3 * D ** -0.5),
        "mod_b": nrm((L, 6 * D), 0.05),
        "norm1_g": 1.0 + nrm((L, D), 0.05),
        "norm2_g": 1.0 + nrm((L, D), 0.05),
        "w_in": nrm((L, D, IN_COLS), D ** -0.5),
        "tshift_mu": jax.random.uniform(next(ks), (L, RWKV_COLS), f32),
        "w0": jax.random.uniform(next(ks), (L, 2, RWKV_W), f32, minval=-5.0, maxval=1.0),
        "w_up": nrm((L, 2, DECAY_LORA, RWKV_W), 0.5 * DECAY_LORA ** -0.5),
        "a0": nrm((L, 2, RWKV_W), 0.5),
        "a_up": nrm((L, 2, AAA_LORA, RWKV_W), 0.5 * AAA_LORA ** -0.5),
        "g_up": nrm((L, GATE_LORA, RWKV_W), GATE_LORA ** -0.5),
        "k_k": 0.85 + nrm((L, RWKV_W), 0.05),
        "k_a": 1.0 + nrm((L, RWKV_W), 0.05),
        "r_k": nrm((L, RWKV_HEADS, RWKV_HEAD_DIM), 0.1),
        "lnx_g": 1.0 + nrm((L, RWKV_W), 0.05),
        "lnx_b": nrm((L, RWKV_W), 0.02),
        "conv_w": nrm((L, CONV_W, 3), 3 ** -0.5),
        "ret_norm_g": 1.0 + nrm((L, RET_W), 0.05),
        "w_out": nrm((L, D, D), D ** -0.5),
        "ffn_up": nrm((L, D, 2 * D_FF), D ** -0.5),
        "ffn_conv": nrm((L, 2 * D_FF, 3), 3 ** -0.5),
        "ffn_down": nrm((L, D_FF, D), D_FF ** -0.5),
        "final_norm_g": 1.0 + nrm((D,), 0.05),
    }


def reference(x, c, ctx, c_ctx, mod_w, mod_b, norm1_g, norm2_g, w_in, tshift_mu, w0, w_up,
              a0, a_up, g_up, k_k, k_a, r_k, lnx_g, lnx_b, conv_w, ret_norm_g, w_out,
              ffn_up, ffn_conv, ffn_down, final_norm_g):
    f32 = jnp.float32
    bsz, n, _ = x.shape
    rows = n // GRID_W
    row = jnp.repeat(jnp.arange(rows, dtype=f32), GRID_W)
    col = jnp.tile(jnp.arange(GRID_W, dtype=f32), rows)
    rope = _axial_rope_tables(row, col)
    hidx = jnp.arange(RET_HEADS, dtype=f32)
    lg_f = jnp.log1p(-jnp.exp2(-5.0 - hidx))
    lg_b = jnp.log1p(-jnp.exp2(-5.5 - hidx))
    s_zero = jnp.zeros((bsz, RWKV_HEADS, RWKV_HEAD_DIM, RWKV_HEAD_DIM), f32)
    r_zero = jnp.zeros((bsz, RET_HEADS, RET_HEAD_DIM, RET_HEAD_DIM), f32)

    xl = x.astype(f32)
    xc = ctx.astype(f32)
    silu_c = jax.nn.silu(c.astype(f32))
    silu_cc = jax.nn.silu(c_ctx.astype(f32))
    split_at = [RWKV_COLS, RWKV_COLS + CONV_COLS]

    for l in range(DEPTH):
        last = l == DEPTH - 1
        ml = [m[:, None, :] for m in jnp.split(silu_c @ mod_w[l] + mod_b[l], 6, axis=-1)]
        mc = jnp.split(silu_cc @ mod_w[l] + mod_b[l], 6, axis=-1)
        rwkv_p = (tshift_mu[l], w0[l], w_up[l], a0[l], a_up[l], g_up[l], k_k[l], k_a[l])

        zc = (_rmsnorm(xc, norm1_g[l]) * (1.0 + mc[1]) + mc[0]) @ w_in[l]
        zl = (_rmsnorm(xl, norm1_g[l]) * (1.0 + ml[1]) + ml[0]) @ w_in[l]
        zc_rw, zc_cv, zc_rt = jnp.split(zc, split_at, axis=-1)
        zl_rw, zl_cv, zl_rt = jnp.split(zl, split_at, axis=-1)

        feats_c = _rwkv_features(zc_rw, *rwkv_p)
        yc_rw, s_f, s_b = _rwkv_bi(feats_c, s_zero, s_zero)
        feats_l = _rwkv_features(zl_rw, *rwkv_p)
        yl_rw, _, _ = _rwkv_bi(feats_l, s_f, s_b)

        qc, kc, vc, gc = _ret_prepare(zc_rt, None)
        yc_rt, r_f, r_b = _retention_bi(qc, kc, vc, r_zero, r_zero, lg_f, lg_b)
        ql, kl, vl, gl = _ret_prepare(zl_rt, rope)
        yl_rt, _, _ = _retention_bi(ql, kl, vl, r_f, r_b, lg_f, lg_b)

        mix_l = jnp.concatenate([
            _rwkv_out(yl_rw, feats_l, r_k[l], lnx_g[l], lnx_b[l]),
            _conv_mix(zl_cv, conv_w[l]),
            _ret_out(yl_rt, gl, ret_norm_g[l]),
        ], axis=-1)
        xl = xl + ml[2] * (mix_l @ w_out[l])
        xl = xl + ml[5] * _conv_ffn(_rmsnorm(xl, norm2_g[l]) * (1.0 + ml[4]) + ml[3],
                                    ffn_up[l], ffn_conv[l], ffn_down[l])

        if not last:
            mix_c = jnp.concatenate([
                _rwkv_out(yc_rw, feats_c, r_k[l], lnx_g[l], lnx_b[l]),
                _conv_mix(zc_cv, conv_w[l]),
                _ret_out(yc_rt, gc, ret_norm_g[l]),
            ], axis=-1)
            xc = xc + mc[2] * (mix_c @ w_out[l])
            xc = xc + mc[5] * _conv_ffn(_rmsnorm(xc, norm2_g[l]) * (1.0 + mc[4]) + mc[3],
                                        ffn_up[l], ffn_conv[l], ffn_down[l])

    return _rmsnorm(xl, final_norm_g).astype(x.dtype)
```

```python
import functools

import jax
import jax.numpy as jnp
from jax import lax
from jax.experimental import pallas as pl
from jax.experimental.pallas import tpu as pltpu

F32 = jnp.float32
BF16 = jnp.bfloat16

GRID_W = 64
RWKV_HEAD_DIM = 64
DECAY_LORA = 64
AAA_LORA = 64
GATE_LORA = 128
RET_HEAD_DIM = 128
RET_CHUNK = 128
WKV_CHUNK = 64
ROPE_BASE = 10000.0
NORM_EPS = 1e-6
RWKV_GN_EPS = 64e-5

VMEM_LIMIT_BYTES = 48 * 1024 * 1024

_NT = (((1,), (1,)), ((), ()))
_TN = (((0,), (0,)), ((), ()))
_NN = (((1,), (0,)), ((), ()))


def _mm_kernel(a_ref, b_ref, o_ref):
    o_ref[...] = jnp.dot(a_ref[...].astype(BF16), b_ref[...],
                         preferred_element_type=F32).astype(o_ref.dtype)


def _pick_tile(n, candidates):
    for c in candidates:
        if n % c == 0:
            return c
    return n


def _mm(a, b, out_dtype=F32):
    m, k = a.shape
    _, n = b.shape
    b = b.astype(BF16)
    mp = -(-m // 8) * 8
    if mp != m:
        a = jnp.pad(a, ((0, mp - m), (0, 0)))
    tm = _pick_tile(mp, (512, 256, 128, 64, 32, 16, 8))
    tn = _pick_tile(n, (512, 384, 256, 128))
    out = pl.pallas_call(
        _mm_kernel,
        grid=(mp // tm, n // tn),
        in_specs=[pl.BlockSpec((tm, k), lambda i, j: (i, 0)),
                  pl.BlockSpec((k, tn), lambda i, j: (0, j))],
        out_specs=pl.BlockSpec((tm, tn), lambda i, j: (i, j)),
        out_shape=jax.ShapeDtypeStruct((mp, n), out_dtype),
        compiler_params=pltpu.CompilerParams(
            dimension_semantics=("parallel", "arbitrary"),
            vmem_limit_bytes=VMEM_LIMIT_BYTES),
        name="mm",
    )(a, b)
    return out[:m] if mp != m else out


def _mm3(a, b, out_dtype=F32):
    bsz, t, k = a.shape
    return _mm(a.reshape(bsz * t, k), b, out_dtype).reshape(bsz, t, b.shape[1])


def _split_bf16(x):
    hi = x.astype(BF16)
    lo = (x - hi.astype(F32)).astype(BF16)
    return hi, lo


def _dot1(a, b, dn=_NN):
    return lax.dot_general(a.astype(BF16), b.astype(BF16), dn, preferred_element_type=F32)


def _dot3(a, b, dn=_NN):
    ah, al = _split_bf16(a)
    bh, bl = _split_bf16(b)
    out = lax.dot_general(ah, bh, dn, preferred_element_type=F32)
    out += lax.dot_general(ah, bl, dn, preferred_element_type=F32)
    out += lax.dot_general(al, bh, dn, preferred_element_type=F32)
    return out


def _wkv_kernel(r_ref, v_ref, a_ref, lw_ref, k_ref, b_ref, s0_ref, y_ref, sT_ref, s_scr,
                *, reverse, nheads, chunk):
    c = chunk
    i = pl.program_id(1)

    @pl.when(i == 0)
    def _():
        s_scr[...] = s0_ref[0]

    row = lax.broadcasted_iota(jnp.int32, (c, c), 0)
    col = lax.broadcasted_iota(jnp.int32, (c, c), 1)
    if reverse:
        strict, incl = row < col, row <= col
    else:
        strict, incl = row > col, row >= col
    cum = incl.astype(BF16)
    last = 0 if reverse else c - 1

    for h in range(nheads):
        r = r_ref[0, h]
        v = v_ref[0, h]
        a = a_ref[0, h]
        lw = lw_ref[0, h]
        k = k_ref[0, h]
        b = b_ref[0, h]
        s0 = s_scr[h]

        lw_hi, lw_lo = _split_bf16(lw)
        cl = (jnp.dot(cum, lw_hi, preferred_element_type=F32)
              + jnp.dot(cum, lw_lo, preferred_element_type=F32))
        p = jnp.exp(cl)
        pinv = jnp.exp(-cl)
        at = a * jnp.exp(cl - lw)
        bt = b * pinv
        kt = k * pinv
        rt = r * p

        a_ab = jnp.where(strict, _dot3(at, bt, _NT), 0.0)
        a_ak = jnp.where(strict, _dot3(at, kt, _NT), 0.0)
        m_rb = jnp.where(incl, _dot1(rt, bt, _NT), 0.0)
        m_rk = jnp.where(incl, _dot1(rt, kt, _NT), 0.0)

        xa = at
        xu = _dot3(a_ak, v)
        ak = a_ab
        steps = max(1, (c - 1).bit_length())
        for s in range(steps):
            xa = xa + _dot3(ak, xa)
            xu = xu + _dot3(ak, xu)
            if s + 1 < steps:
                ak = _dot3(ak, ak)

        rbar = rt + _dot1(m_rb, xa)
        y0 = _dot1(m_rb, xu) + _dot1(m_rk, v)
        g = _dot3(xa, bt, _TN)
        hh = _dot3(xu, bt, _TN) + _dot3(v, kt, _TN)

        y_ref[0, h] = _dot1(rbar, s0, _NT) + y0
        s_scr[h] = (s0 + _dot3(s0, g) + hh) * p[last:last + 1, :]

    @pl.when(i == pl.num_programs(1) - 1)
    def _():
        sT_ref[0] = s_scr[...]


def _wkv_scan(r, v, a, lw, k, b, s0, reverse):
    bsz, nh, t, d = r.shape
    c = WKV_CHUNK
    nc = t // c
    if reverse:
        tok = lambda bi, i: (bi, 0, nc - 1 - i, 0)
    else:
        tok = lambda bi, i: (bi, 0, i, 0)
    tok_spec = pl.BlockSpec((1, nh, c, d), tok)
    st_spec = pl.BlockSpec((1, nh, d, d), lambda bi, i: (bi, 0, 0, 0))
    return pl.pallas_call(
        functools.partial(_wkv_kernel, reverse=reverse, nheads=nh, chunk=c),
        grid=(bsz, nc),
        in_specs=[tok_spec] * 6 + [st_spec],
        out_specs=[tok_spec, st_spec],
        out_shape=[jax.ShapeDtypeStruct((bsz, nh, t, d), F32),
                   jax.ShapeDtypeStruct((bsz, nh, d, d), F32)],
        scratch_shapes=[pltpu.VMEM((nh, d, d), F32)],
        compiler_params=pltpu.CompilerParams(
            dimension_semantics=("parallel", "arbitrary"),
            vmem_limit_bytes=VMEM_LIMIT_BYTES),
        name="wkv_bwd" if reverse else "wkv_fwd",
    )(r, v, a, lw, k, b, s0)


def _ret_kernel(q_ref, k_ref, v_ref, din_ref, xi_ref, zeta_ref, g_ref, r0_ref, y_ref, rT_ref, r_scr,
                *, nheads):
    i = pl.program_id(1)

    @pl.when(i == 0)
    def _():
        r_scr[...] = r0_ref[0]

    for h in range(nheads):
        q = q_ref[0, h]
        k = k_ref[0, h]
        v = v_ref[0, h]
        st = r_scr[h]
        s = _dot1(q, k, _NT) * din_ref[h]
        y_ref[0, h] = _dot1(s, v) + _dot1(q, st) * xi_ref[h]
        r_scr[h] = st * g_ref[h] + _dot1(k * zeta_ref[h], v, _TN)

    @pl.when(i == pl.num_programs(1) - 1)
    def _():
        rT_ref[0] = r_scr[...]


def _ret_tables(log_gamma, reverse):
    c, d = RET_CHUNK, RET_HEAD_DIM
    idx = jnp.arange(c, dtype=F32)
    lg = log_gamma[:, None, None]
    if reverse:
        diff = idx[None, :] - idx[:, None]
        keep = diff > 0
        xi = jnp.exp(log_gamma[:, None] * (c - idx))
        zeta = jnp.exp(log_gamma[:, None] * idx)
    else:
        diff = idx[:, None] - idx[None, :]
        keep = diff >= 0
        xi = jnp.exp(log_gamma[:, None] * (idx + 1.0))
        zeta = jnp.exp(log_gamma[:, None] * (c - 1.0 - idx))
    din = jnp.where(keep, jnp.exp(lg * jnp.maximum(diff, 0.0)), 0.0)
    nh = log_gamma.shape[0]
    xi = jnp.broadcast_to(xi[:, :, None], (nh, c, d))
    zeta = jnp.broadcast_to(zeta[:, :, None], (nh, c, d))
    g = jnp.broadcast_to(jnp.exp(log_gamma * c)[:, None, None], (nh, d, d))
    return din, xi, zeta, g


def _ret_scan(q, k, v, r0, log_gamma, reverse):
    bsz, nh, t, d = q.shape
    c = RET_CHUNK
    nc = t // c
    din, xi, zeta, g = _ret_tables(log_gamma, reverse)
    if reverse:
        tok = lambda bi, i: (bi, 0, nc - 1 - i, 0)
    else:
        tok = lambda bi, i: (bi, 0, i, 0)
    tok_spec = pl.BlockSpec((1, nh, c, d), tok)
    st_spec = pl.BlockSpec((1, nh, d, d), lambda bi, i: (bi, 0, 0, 0))
    tab_spec = lambda n: pl.BlockSpec((nh, n, d), lambda bi, i: (0, 0, 0))
    return pl.pallas_call(
        functools.partial(_ret_kernel, nheads=nh),
        grid=(bsz, nc),
        in_specs=[tok_spec] * 3 + [tab_spec(c), tab_spec(c), tab_spec(c), tab_spec(d), st_spec],
        out_specs=[tok_spec, st_spec],
        out_shape=[jax.ShapeDtypeStruct((bsz, nh, t, d), F32),
                   jax.ShapeDtypeStruct((bsz, nh, d, d), F32)],
        scratch_shapes=[pltpu.VMEM((nh, d, d), F32)],
        compiler_params=pltpu.CompilerParams(
            dimension_semantics=("parallel", "arbitrary"),
            vmem_limit_bytes=VMEM_LIMIT_BYTES),
        name="ret_bwd" if reverse else "ret_fwd",
    )(q, k, v, din, xi, zeta, g, r0)


def _rmsnorm(x, g):
    return x * lax.rsqrt(jnp.mean(x * x, axis=-1, keepdims=True) + NORM_EPS) * g


def _head_norm(y, eps):
    mu = jnp.mean(y, axis=-1, keepdims=True)
    var = jnp.mean(jnp.square(y - mu), axis=-1, keepdims=True)
    return (y - mu) * lax.rsqrt(var + eps)


def _dwconv3(z, w):
    zp = jnp.pad(z, ((0, 0), (1, 1), (0, 0)))
    return zp[:, :-2] * w[:, 0] + zp[:, 1:-1] * w[:, 1] + zp[:, 2:] * w[:, 2]


def _token_lerp(z, mu):
    zp = jnp.pad(z, ((0, 0), (1, 1), (0, 0)))
    nb = 0.5 * (zp[:, :-2] + zp[:, 2:])
    return z + (nb - z) * mu


def _to_heads(u, d):
    bsz, t, w = u.shape
    return u.reshape(bsz, t, w // d, d).transpose(0, 2, 1, 3)


def _from_heads(u):
    bsz, nh, t, d = u.shape
    return u.transpose(0, 2, 1, 3).reshape(bsz, t, nh * d)


def _rwkv_mix(z, p, s0_f, s0_b, want_out):
    mu, w0, w_up, a0, a_up, g_up, k_k, k_a, r_k, lnx_g, lnx_b = p
    bsz, t, _ = z.shape
    rw = k_k.shape[0]
    nh = rw // RWKV_HEAD_DIM
    z = _token_lerp(z, mu)
    o = 3 * rw
    r, k, v = z[..., :rw], z[..., rw:2 * rw], z[..., 2 * rw:o]
    wd = z[..., o:o + DECAY_LORA]
    ad = z[..., o + DECAY_LORA:o + DECAY_LORA + AAA_LORA]
    gd = z[..., o + DECAY_LORA + AAA_LORA:]
    kk = (k * k_k).reshape(bsz, t, nh, RWKV_HEAD_DIM)
    kk = kk / jnp.maximum(jnp.sqrt(jnp.sum(kk * kk, axis=-1, keepdims=True)), 1e-12)
    kk = kk.reshape(bsz, t, rw)
    tw = jnp.tanh(wd)
    hd = lambda u: _to_heads(u, RWKV_HEAD_DIM)
    r_h, v_h, na_h = hd(r), hd(v), hd(-kk)
    ys, states, keys = [], [], []
    for d, (s0, rev) in enumerate(((s0_f, False), (s0_b, True))):
        w_log = -jax.nn.softplus(-(w0[d] + _mm3(tw, w_up[d]))) - 0.5
        lw = -jnp.exp(w_log)
        rate = jax.nn.sigmoid(a0[d] + _mm3(ad, a_up[d]))
        key = k * (1.0 + (rate - 1.0) * k_a)
        keys.append(key)
        y, s = _wkv_scan(r_h, v_h, na_h, hd(lw), hd(key), hd(kk * rate), s0, rev)
        ys.append(y)
        states.append(s)
    if not want_out:
        return None, states[0], states[1]
    g = _mm3(jax.nn.sigmoid(gd), g_up)
    y = (ys[0] + ys[1]).transpose(0, 2, 1, 3)
    shp = (bsz, t, nh, RWKV_HEAD_DIM)
    k_mid = 0.5 * (keys[0] + keys[1])
    bonus = jnp.sum(r.reshape(shp) * k_mid.reshape(shp) * r_k, axis=-1, keepdims=True) * v.reshape(shp)
    out = _head_norm(y, RWKV_GN_EPS).reshape(bsz, t, rw) * lnx_g + lnx_b + bonus.reshape(bsz, t, rw)
    return out * g, states[0], states[1]


def _conv_mix(z, conv_w):
    gate_b, gate_c, h = jnp.split(z, 3, axis=-1)
    return gate_b * _dwconv3(gate_c * h, conv_w)


def _apply_rope(u, cos, sin):
    half = RET_HEAD_DIM // 2
    u1, u2 = u[..., :half], u[..., half:]
    return jnp.concatenate([u1 * cos - u2 * sin, u1 * sin + u2 * cos], axis=-1)


def _ret_mix(z, rope, r0_f, r0_b, lg_f, lg_b, ret_norm_g, want_out):
    q, k, v, g = jnp.split(z, 4, axis=-1)
    hd = lambda u: _to_heads(u, RET_HEAD_DIM)
    q, k, v = hd(q) * (RET_HEAD_DIM ** -0.5), hd(k), hd(v)
    if rope is not None:
        q, k = _apply_rope(q, *rope), _apply_rope(k, *rope)
    y_f, r_f = _ret_scan(q, k, v, r0_f, lg_f, False)
    y_b, r_b = _ret_scan(q, k, v, r0_b, lg_b, True)
    if not want_out:
        return None, r_f, r_b
    yn = _from_heads(_head_norm(y_f + y_b, NORM_EPS)) * ret_norm_g
    return jax.nn.silu(g) * yn, r_f, r_b


def _conv_ffn(h, up, cw, down):
    u = _dwconv3(_mm3(h, up), cw)
    gt, vl = jnp.split(u, 2, axis=-1)
    return _mm3(jax.nn.silu(gt) * vl, down)


def kernel(x, c, ctx, c_ctx, mod_w, mod_b, norm1_g, norm2_g, w_in, tshift_mu, w0, w_up, a0, a_up, g_up,
           k_k, k_a, r_k, lnx_g, lnx_b, conv_w, ret_norm_g, w_out, ffn_up, ffn_conv, ffn_down,
           final_norm_g):
    bsz, n, dm = x.shape
    depth = mod_w.shape[0]
    rwkv_w = k_k.shape[1]
    rwkv_cols = tshift_mu.shape[1]
    conv_cols = 3 * conv_w.shape[1]
    ret_heads = ret_norm_g.shape[1] // RET_HEAD_DIM
    nh = rwkv_w // RWKV_HEAD_DIM

    rows = n // GRID_W
    row = jnp.repeat(jnp.arange(rows, dtype=F32), GRID_W)
    col = jnp.tile(jnp.arange(GRID_W, dtype=F32), rows)
    pairs = RET_HEAD_DIM // 4
    freqs = 1.0 / (ROPE_BASE ** (jnp.arange(pairs, dtype=F32) / pairs))
    ang = jnp.concatenate([row[:, None] * freqs, col[:, None] * freqs], axis=-1)
    rope = (jnp.cos(ang), jnp.sin(ang))
    hidx = jnp.arange(ret_heads, dtype=F32)
    lg_f = jnp.log1p(-jnp.exp2(-5.0 - hidx))
    lg_b = jnp.log1p(-jnp.exp2(-5.5 - hidx))
    s_zero = jnp.zeros((bsz, nh, RWKV_HEAD_DIM, RWKV_HEAD_DIM), F32)
    r_zero = jnp.zeros((bsz, ret_heads, RET_HEAD_DIM, RET_HEAD_DIM), F32)

    xl = x.astype(F32)
    xc = ctx.astype(F32)
    silu_all = jnp.concatenate([jax.nn.silu(c.astype(F32)), jax.nn.silu(c_ctx.astype(F32))[None]], axis=0)

    for l in range(depth):
        last = l == depth - 1
        mod = _mm(silu_all, mod_w[l]) + mod_b[l]
        ml = [m[:, None, :] for m in jnp.split(mod[:bsz], 6, axis=-1)]
        mc = jnp.split(mod[bsz], 6, axis=-1)
        rwkv_p = (tshift_mu[l], w0[l], w_up[l], a0[l], a_up[l], g_up[l], k_k[l], k_a[l],
                  r_k[l], lnx_g[l], lnx_b[l])
        w_in_l = w_in[l].astype(BF16)
        w_out_l = w_out[l].astype(BF16)
        up_l = ffn_up[l].astype(BF16)
        down_l = ffn_down[l].astype(BF16)

        zc = _mm3(_rmsnorm(xc, norm1_g[l]) * (1.0 + mc[1]) + mc[0], w_in_l)
        zl = _mm3(_rmsnorm(xl, norm1_g[l]) * (1.0 + ml[1]) + ml[0], w_in_l)
        s1, s2 = rwkv_cols, rwkv_cols + conv_cols

        out_c_rw, s_f, s_b = _rwkv_mix(zc[..., :s1], rwkv_p, s_zero, s_zero, not last)
        out_l_rw, _, _ = _rwkv_mix(zl[..., :s1], rwkv_p, s_f, s_b, True)
        out_c_rt, r_f, r_b = _ret_mix(zc[..., s2:], None, r_zero, r_zero, lg_f, lg_b, ret_norm_g[l], not last)
        out_l_rt, _, _ = _ret_mix(zl[..., s2:], rope, r_f, r_b, lg_f, lg_b, ret_norm_g[l], True)

        mix_l = jnp.concatenate([out_l_rw, _conv_mix(zl[..., s1:s2], conv_w[l]), out_l_rt], axis=-1)
        xl = xl + ml[2] * _mm3(mix_l, w_out_l)
        xl = xl + ml[5] * _conv_ffn(_rmsnorm(xl, norm2_g[l]) * (1.0 + ml[4]) + ml[3],
                                    up_l, ffn_conv[l], down_l)
        if not last:
            mix_c = jnp.concatenate([out_c_rw, _conv_mix(zc[..., s1:s2], conv_w[l]), out_c_rt], axis=-1)
            xc = xc + mc[2] * _mm3(mix_c, w_out_l)
            xc = xc + mc[5] * _conv_ffn(_rmsnorm(xc, norm2_g[l]) * (1.0 + mc[4]) + mc[3],
                                        up_l, ffn_conv[l], down_l)

    return _rmsnorm(xl, final_norm_g).astype(x.dtype)
```

```python
import functools

import jax
import jax.numpy as jnp
from jax import lax
from jax.experimental import pallas as pl
from jax.experimental.pallas import tpu as pltpu

F32 = jnp.float32
BF16 = jnp.bfloat16

GRID_W = 64
RWKV_HEAD_DIM = 64
DECAY_LORA = 64
AAA_LORA = 64
GATE_LORA = 128
RET_HEAD_DIM = 128
RET_CHUNK = 128
WKV_CHUNK = 64
ROPE_BASE = 10000.0
NORM_EPS = 1e-6
RWKV_GN_EPS = 64e-5

VMEM_LIMIT_BYTES = 48 * 1024 * 1024

_NT = (((1,), (1,)), ((), ()))
_TN = (((0,), (0,)), ((), ()))
_NN = (((1,), (0,)), ((), ()))


def _mm_kernel(a_ref, b_ref, o_ref):
    o_ref[...] = jnp.dot(a_ref[...].astype(BF16), b_ref[...],
                         preferred_element_type=F32).astype(o_ref.dtype)


def _pick_tile(n, candidates):
    for c in candidates:
        if n % c == 0:
            return c
    return n


def _mm(a, b, out_dtype=F32):
    m, k = a.shape
    _, n = b.shape
    b = b.astype(BF16)
    mp = -(-m // 8) * 8
    if mp != m:
        a = jnp.pad(a, ((0, mp - m), (0, 0)))
    tm = _pick_tile(mp, (512, 256, 128, 64, 32, 16, 8))
    tn = _pick_tile(n, (512, 384, 256, 128))
    out = pl.pallas_call(
        _mm_kernel,
        grid=(mp // tm, n // tn),
        in_specs=[pl.BlockSpec((tm, k), lambda i, j: (i, 0)),
                  pl.BlockSpec((k, tn), lambda i, j: (0, j))],
        out_specs=pl.BlockSpec((tm, tn), lambda i, j: (i, j)),
        out_shape=jax.ShapeDtypeStruct((mp, n), out_dtype),
        compiler_params=pltpu.CompilerParams(
            dimension_semantics=("parallel", "arbitrary"),
            vmem_limit_bytes=VMEM_LIMIT_BYTES),
        name="mm",
    )(a, b)
    return out[:m] if mp != m else out


def _mm3(a, b, out_dtype=F32):
    bsz, t, k = a.shape
    return _mm(a.reshape(bsz * t, k), b, out_dtype).reshape(bsz, t, b.shape[1])


def _split_bf16(x):
    hi = x.astype(BF16)
    lo = (x - hi.astype(F32)).astype(BF16)
    return hi, lo


def _dot1(a, b, dn=_NN):
    return lax.dot_general(a.astype(BF16), b.astype(BF16), dn, preferred_element_type=F32)


def _wkv_kernel(r_ref, v_ref, a_ref, lw_ref, k_ref, b_ref, s0_ref, y_ref, sT_ref, s_scr,
                *, reverse, npairs, chunk):
    c = chunk
    w2 = 2 * RWKV_HEAD_DIM
    i = pl.program_id(1)

    @pl.when(i == 0)
    def _():
        s_scr[...] = s0_ref[0]

    row = lax.broadcasted_iota(jnp.int32, (c, c), 0)
    col = lax.broadcasted_iota(jnp.int32, (c, c), 1)
    if reverse:
        strict, incl = row < col, row <= col
    else:
        strict, incl = row > col, row >= col
    cum = incl.astype(BF16)
    last = 0 if reverse else c - 1
    first_head = lax.broadcasted_iota(jnp.int32, (c, w2), 1) < RWKV_HEAD_DIM
    first_head2 = jnp.concatenate([first_head, first_head], axis=1)
    brow = lax.broadcasted_iota(jnp.int32, (w2, w2), 0) < RWKV_HEAD_DIM
    bcol = lax.broadcasted_iota(jnp.int32, (w2, w2), 1) < RWKV_HEAD_DIM
    same_head = brow == bcol

    lw = lw_ref[0]
    lw_hi, lw_lo = _split_bf16(lw)
    cl = (jnp.dot(cum, lw_hi, preferred_element_type=F32)
          + jnp.dot(cum, lw_lo, preferred_element_type=F32))
    p = jnp.exp(cl)
    pinv = jnp.exp(-cl)
    at_all = a_ref[0] * jnp.exp(cl - lw)
    bt_all = b_ref[0] * pinv
    kt_all = k_ref[0] * pinv
    rt_all = r_ref[0] * p

    def per_head(mats, rhs, mask):
        return jnp.where(mask, _dot1(mats[0], rhs), _dot1(mats[1], rhs))

    pairs = range(npairs)
    sls = [slice(q * w2, (q + 1) * w2) for q in pairs]
    at = [at_all[:, sl] for sl in sls]
    bt = [bt_all[:, sl] for sl in sls]
    kt = [kt_all[:, sl] for sl in sls]
    rt = [rt_all[:, sl] for sl in sls]
    v = [v_ref[0, :, sl] for sl in sls]

    a_ab, a_ak, m_rb, m_rk = [], [], [], []
    for q in pairs:
        ab_q, ak_q, rb_q, rk_q = [], [], [], []
        for hm in (first_head, jnp.logical_not(first_head)):
            lhs = jnp.concatenate([jnp.where(hm, at[q], 0.0), jnp.where(hm, rt[q], 0.0)], axis=0)
            mb = _dot1(lhs, bt[q], _NT)
            mk = _dot1(lhs, kt[q], _NT)
            ab_q.append(jnp.where(strict, mb[:c], 0.0))
            rb_q.append(jnp.where(incl, mb[c:], 0.0))
            ak_q.append(jnp.where(strict, mk[:c], 0.0))
            rk_q.append(jnp.where(incl, mk[c:], 0.0))
        a_ab.append(ab_q)
        a_ak.append(ak_q)
        m_rb.append(rb_q)
        m_rk.append(rk_q)

    x = [jnp.concatenate([at[q], per_head(a_ak[q], v[q], first_head)], axis=1) for q in pairs]
    ak = a_ab
    steps = max(1, (c - 1).bit_length())
    for s in range(steps):
        x = [x[q] + per_head(ak[q], x[q], first_head2) for q in pairs]
        if s + 1 < steps:
            ak = [[_dot1(ak[q][0], ak[q][0]), _dot1(ak[q][1], ak[q][1])] for q in pairs]

    rbx = [per_head(m_rb[q], x[q], first_head2) for q in pairs]
    rbar = [rt[q] + rbx[q][:, :w2] for q in pairs]
    y0 = [rbx[q][:, w2:] + per_head(m_rk[q], v[q], first_head) for q in pairs]
    g = [jnp.where(same_head, _dot1(x[q][:, :w2], bt[q], _TN), 0.0) for q in pairs]
    hh = [jnp.where(same_head,
                    _dot1(jnp.concatenate([x[q][:, w2:], v[q]], axis=0),
                          jnp.concatenate([bt[q], kt[q]], axis=0), _TN),
                    0.0) for q in pairs]

    for q in pairs:
        s0 = s_scr[q]
        y_ref[0, :, sls[q]] = _dot1(rbar[q], s0, _NT) + y0[q]
        s_scr[q] = (s0 + _dot1(s0, g[q]) + hh[q]) * p[last:last + 1, sls[q]]


    @pl.when(i == pl.num_programs(1) - 1)
    def _():
        sT_ref[0] = s_scr[...]


def _wkv_scan(r, v, a, lw, k, b, s0, reverse):
    bsz, t, w = r.shape
    w2 = 2 * RWKV_HEAD_DIM
    npairs = w // w2
    c = WKV_CHUNK
    nc = t // c
    if reverse:
        tok = lambda bi, i: (bi, nc - 1 - i, 0)
    else:
        tok = lambda bi, i: (bi, i, 0)
    tok_spec = pl.BlockSpec((1, c, w), tok)
    st_spec = pl.BlockSpec((1, npairs, w2, w2), lambda bi, i: (bi, 0, 0, 0))
    return pl.pallas_call(
        functools.partial(_wkv_kernel, reverse=reverse, npairs=npairs, chunk=c),
        grid=(bsz, nc),
        in_specs=[tok_spec] * 6 + [st_spec],
        out_specs=[tok_spec, st_spec],
        out_shape=[jax.ShapeDtypeStruct((bsz, t, w), F32),
                   jax.ShapeDtypeStruct((bsz, npairs, w2, w2), F32)],
        scratch_shapes=[pltpu.VMEM((npairs, w2, w2), F32)],
        compiler_params=pltpu.CompilerParams(
            dimension_semantics=("parallel", "arbitrary"),
            vmem_limit_bytes=VMEM_LIMIT_BYTES),
        name="wkv_bwd" if reverse else "wkv_fwd",
    )(r, v, a, lw, k, b, s0)


def _ret_kernel(q_ref, k_ref, v_ref, din_ref, xi_ref, zeta_ref, g_ref, r0_ref, y_ref, rT_ref, r_scr,
                *, nheads):
    i = pl.program_id(1)

    @pl.when(i == 0)
    def _():
        r_scr[...] = r0_ref[0]

    for h in range(nheads):
        q = q_ref[0, h]
        k = k_ref[0, h]
        v = v_ref[0, h]
        st = r_scr[h]
        s = _dot1(q, k, _NT) * din_ref[h]
        y_ref[0, h] = _dot1(s, v) + _dot1(q, st) * xi_ref[h]
        r_scr[h] = st * g_ref[h] + _dot1(k * zeta_ref[h], v, _TN)

    @pl.when(i == pl.num_programs(1) - 1)
    def _():
        rT_ref[0] = r_scr[...]


def _ret_tables(log_gamma, reverse):
    c, d = RET_CHUNK, RET_HEAD_DIM
    idx = jnp.arange(c, dtype=F32)
    lg = log_gamma[:, None, None]
    if reverse:
        diff = idx[None, :] - idx[:, None]
        keep = diff > 0
        xi = jnp.exp(log_gamma[:, None] * (c - idx))
        zeta = jnp.exp(log_gamma[:, None] * idx)
    else:
        diff = idx[:, None] - idx[None, :]
        keep = diff >= 0
        xi = jnp.exp(log_gamma[:, None] * (idx + 1.0))
        zeta = jnp.exp(log_gamma[:, None] * (c - 1.0 - idx))
    din = jnp.where(keep, jnp.exp(lg * jnp.maximum(diff, 0.0)), 0.0)
    nh = log_gamma.shape[0]
    xi = jnp.broadcast_to(xi[:, :, None], (nh, c, d))
    zeta = jnp.broadcast_to(zeta[:, :, None], (nh, c, d))
    g = jnp.broadcast_to(jnp.exp(log_gamma * c)[:, None, None], (nh, d, d))
    return din, xi, zeta, g


def _ret_scan(q, k, v, r0, log_gamma, reverse):
    bsz, nh, t, d = q.shape
    c = RET_CHUNK
    nc = t // c
    din, xi, zeta, g = _ret_tables(log_gamma, reverse)
    if reverse:
        tok = lambda bi, i: (bi, 0, nc - 1 - i, 0)
    else:
        tok = lambda bi, i: (bi, 0, i, 0)
    tok_spec = pl.BlockSpec((1, nh, c, d), tok)
    st_spec = pl.BlockSpec((1, nh, d, d), lambda bi, i: (bi, 0, 0, 0))
    tab_spec = lambda n: pl.BlockSpec((nh, n, d), lambda bi, i: (0, 0, 0))
    return pl.pallas_call(
        functools.partial(_ret_kernel, nheads=nh),
        grid=(bsz, nc),
        in_specs=[tok_spec] * 3 + [tab_spec(c), tab_spec(c), tab_spec(c), tab_spec(d), st_spec],
        out_specs=[tok_spec, st_spec],
        out_shape=[jax.ShapeDtypeStruct((bsz, nh, t, d), F32),
                   jax.ShapeDtypeStruct((bsz, nh, d, d), F32)],
        scratch_shapes=[pltpu.VMEM((nh, d, d), F32)],
        compiler_params=pltpu.CompilerParams(
            dimension_semantics=("parallel", "arbitrary"),
            vmem_limit_bytes=VMEM_LIMIT_BYTES),
        name="ret_bwd" if reverse else "ret_fwd",
    )(q, k, v, din, xi, zeta, g, r0)


def _rmsnorm(x, g):
    return x * lax.rsqrt(jnp.mean(x * x, axis=-1, keepdims=True) + NORM_EPS) * g


def _head_norm(y, eps):
    mu = jnp.mean(y, axis=-1, keepdims=True)
    var = jnp.mean(jnp.square(y - mu), axis=-1, keepdims=True)
    return (y - mu) * lax.rsqrt(var + eps)


def _dwconv3(z, w):
    zp = jnp.pad(z, ((0, 0), (1, 1), (0, 0)))
    return zp[:, :-2] * w[:, 0] + zp[:, 1:-1] * w[:, 1] + zp[:, 2:] * w[:, 2]


def _token_lerp(z, mu):
    zp = jnp.pad(z, ((0, 0), (1, 1), (0, 0)))
    nb = 0.5 * (zp[:, :-2] + zp[:, 2:])
    return z + (nb - z) * mu


def _to_heads(u, d):
    bsz, t, w = u.shape
    return u.reshape(bsz, t, w // d, d).transpose(0, 2, 1, 3)


def _from_heads(u):
    bsz, nh, t, d = u.shape
    return u.transpose(0, 2, 1, 3).reshape(bsz, t, nh * d)


def _rwkv_mix(z, p, s0_f, s0_b, want_out):
    mu, w0, w_up, a0, a_up, g_up, k_k, k_a, r_k, lnx_g, lnx_b = p
    bsz, t, _ = z.shape
    rw = k_k.shape[0]
    nh = rw // RWKV_HEAD_DIM
    shp = (bsz, t, nh, RWKV_HEAD_DIM)
    z = _token_lerp(z, mu)
    o = 3 * rw
    r, k, v = z[..., :rw], z[..., rw:2 * rw], z[..., 2 * rw:o]
    wd = z[..., o:o + DECAY_LORA]
    ad = z[..., o + DECAY_LORA:o + DECAY_LORA + AAA_LORA]
    gd = z[..., o + DECAY_LORA + AAA_LORA:]
    kk = (k * k_k).reshape(shp)
    kk = kk / jnp.maximum(jnp.sqrt(jnp.sum(kk * kk, axis=-1, keepdims=True)), 1e-12)
    kk = kk.reshape(bsz, t, rw)
    tw = jnp.tanh(wd)
    ys, states, keys = [], [], []
    for d, (s0, rev) in enumerate(((s0_f, False), (s0_b, True))):
        w_log = -jax.nn.softplus(-(w0[d] + _mm3(tw, w_up[d]))) - 0.5
        lw = -jnp.exp(w_log)
        rate = jax.nn.sigmoid(a0[d] + _mm3(ad, a_up[d]))
        key = k * (1.0 + (rate - 1.0) * k_a)
        keys.append(key)
        y, s = _wkv_scan(r, v, -kk, lw, key, kk * rate, s0, rev)
        ys.append(y)
        states.append(s)
    if not want_out:
        return None, states[0], states[1]
    g = _mm3(jax.nn.sigmoid(gd), g_up)
    y = (ys[0] + ys[1]).reshape(shp)
    k_mid = 0.5 * (keys[0] + keys[1])
    bonus = jnp.sum(r.reshape(shp) * k_mid.reshape(shp) * r_k, axis=-1, keepdims=True) * v.reshape(shp)
    out = _head_norm(y, RWKV_GN_EPS).reshape(bsz, t, rw) * lnx_g + lnx_b + bonus.reshape(bsz, t, rw)
    return out * g, states[0], states[1]


def _conv_mix(z, conv_w):
    gate_b, gate_c, h = jnp.split(z, 3, axis=-1)
    return gate_b * _dwconv3(gate_c * h, conv_w)


def _apply_rope(u, cos, sin):
    half = RET_HEAD_DIM // 2
    u1, u2 = u[..., :half], u[..., half:]
    return jnp.concatenate([u1 * cos - u2 * sin, u1 * sin + u2 * cos], axis=-1)


def _ret_mix(z, rope, r0_f, r0_b, lg_f, lg_b, ret_norm_g, want_out):
    q, k, v, g = jnp.split(z, 4, axis=-1)
    hd = lambda u: _to_heads(u, RET_HEAD_DIM)
    q, k, v = hd(q) * (RET_HEAD_DIM ** -0.5), hd(k), hd(v)
    if rope is not None:
        q, k = _apply_rope(q, *rope), _apply_rope(k, *rope)
    y_f, r_f = _ret_scan(q, k, v, r0_f, lg_f, False)
    y_b, r_b = _ret_scan(q, k, v, r0_b, lg_b, True)
    if not want_out:
        return None, r_f, r_b
    yn = _from_heads(_head_norm(y_f + y_b, NORM_EPS)) * ret_norm_g
    return jax.nn.silu(g) * yn, r_f, r_b


def _conv_ffn(h, up, cw, down):
    u = _dwconv3(_mm3(h, up), cw)
    gt, vl = jnp.split(u, 2, axis=-1)
    return _mm3(jax.nn.silu(gt) * vl, down)


def kernel(x, c, ctx, c_ctx, mod_w, mod_b, norm1_g, norm2_g, w_in, tshift_mu, w0, w_up, a0, a_up, g_up,
           k_k, k_a, r_k, lnx_g, lnx_b, conv_w, ret_norm_g, w_out, ffn_up, ffn_conv, ffn_down,
           final_norm_g):
    bsz, n, dm = x.shape
    depth = mod_w.shape[0]
    rwkv_w = k_k.shape[1]
    rwkv_cols = tshift_mu.shape[1]
    conv_cols = 3 * conv_w.shape[1]
    ret_heads = ret_norm_g.shape[1] // RET_HEAD_DIM
    npairs = rwkv_w // (2 * RWKV_HEAD_DIM)

    rows = n // GRID_W
    row = jnp.repeat(jnp.arange(rows, dtype=F32), GRID_W)
    col = jnp.tile(jnp.arange(GRID_W, dtype=F32), rows)
    pairs = RET_HEAD_DIM // 4
    freqs = 1.0 / (ROPE_BASE ** (jnp.arange(pairs, dtype=F32) / pairs))
    ang = jnp.concatenate([row[:, None] * freqs, col[:, None] * freqs], axis=-1)
    rope = (jnp.cos(ang), jnp.sin(ang))
    hidx = jnp.arange(ret_heads, dtype=F32)
    lg_f = jnp.log1p(-jnp.exp2(-5.0 - hidx))
    lg_b = jnp.log1p(-jnp.exp2(-5.5 - hidx))
    s_zero = jnp.zeros((bsz, npairs, 2 * RWKV_HEAD_DIM, 2 * RWKV_HEAD_DIM), F32)
    r_zero = jnp.zeros((bsz, ret_heads, RET_HEAD_DIM, RET_HEAD_DIM), F32)

    xl = x.astype(F32)
    xc = ctx.astype(F32)
    silu_all = jnp.concatenate([jax.nn.silu(c.astype(F32)), jax.nn.silu(c_ctx.astype(F32))[None]], axis=0)

    for l in range(depth):
        last = l == depth - 1
        mod = _mm(silu_all, mod_w[l]) + mod_b[l]
        ml = [m[:, None, :] for m in jnp.split(mod[:bsz], 6, axis=-1)]
        mc = jnp.split(mod[bsz], 6, axis=-1)
        rwkv_p = (tshift_mu[l], w0[l], w_up[l], a0[l], a_up[l], g_up[l], k_k[l], k_a[l],
                  r_k[l], lnx_g[l], lnx_b[l])
        w_in_l = w_in[l].astype(BF16)
        w_out_l = w_out[l].astype(BF16)
        up_l = ffn_up[l].astype(BF16)
        down_l = ffn_down[l].astype(BF16)

        zc = _mm3(_rmsnorm(xc, norm1_g[l]) * (1.0 + mc[1]) + mc[0], w_in_l)
        zl = _mm3(_rmsnorm(xl, norm1_g[l]) * (1.0 + ml[1]) + ml[0], w_in_l)
        s1, s2 = rwkv_cols, rwkv_cols + conv_cols

        out_c_rw, s_f, s_b = _rwkv_mix(zc[..., :s1], rwkv_p, s_zero, s_zero, not last)
        out_l_rw, _, _ = _rwkv_mix(zl[..., :s1], rwkv_p, s_f, s_b, True)
        out_c_rt, r_f, r_b = _ret_mix(zc[..., s2:], None, r_zero, r_zero, lg_f, lg_b, ret_norm_g[l], not last)
        out_l_rt, _, _ = _ret_mix(zl[..., s2:], rope, r_f, r_b, lg_f, lg_b, ret_norm_g[l], True)

        mix_l = jnp.concatenate([out_l_rw, _conv_mix(zl[..., s1:s2], conv_w[l]), out_l_rt], axis=-1)
        xl = xl + ml[2] * _mm3(mix_l, w_out_l)
        xl = xl + ml[5] * _conv_ffn(_rmsnorm(xl, norm2_g[l]) * (1.0 + ml[4]) + ml[3],
                                    up_l, ffn_conv[l], down_l)
        if not last:
            mix_c = jnp.concatenate([out_c_rw, _conv_mix(zc[..., s1:s2], conv_w[l]), out_c_rt], axis=-1)
            xc = xc + mc[2] * _mm3(mix_c, w_out_l)
            xc = xc + mc[5] * _conv_ffn(_rmsnorm(xc, norm2_g[l]) * (1.0 + mc[4]) + mc[3],
                                        up_l, ffn_conv[l], down_l)

    return _rmsnorm(xl, final_norm_g).astype(x.dtype)
```

```python
import functools

import jax
import jax.numpy as jnp
from jax import lax
from jax.experimental import pallas as pl
from jax.experimental.pallas import tpu as pltpu

F32 = jnp.float32
BF16 = jnp.bfloat16

GRID_W = 64
RWKV_HEAD_DIM = 64
DECAY_LORA = 64
AAA_LORA = 64
GATE_LORA = 128
RET_HEAD_DIM = 128
RET_CHUNK = 128
WKV_CHUNK = 64
ROPE_BASE = 10000.0
NORM_EPS = 1e-6
RWKV_GN_EPS = 64e-5

VMEM_LIMIT_BYTES = 48 * 1024 * 1024

_NT = (((1,), (1,)), ((), ()))
_TN = (((0,), (0,)), ((), ()))
_NN = (((1,), (0,)), ((), ()))


def _mm_kernel(a_ref, b_ref, o_ref):
    o_ref[...] = jnp.dot(a_ref[...].astype(BF16), b_ref[...],
                         preferred_element_type=F32).astype(o_ref.dtype)


def _pick_tile(n, candidates):
    for c in candidates:
        if n % c == 0:
            return c
    return n


def _mm(a, b, out_dtype=F32):
    m, k = a.shape
    _, n = b.shape
    b = b.astype(BF16)
    mp = -(-m // 8) * 8
    if mp != m:
        a = jnp.pad(a, ((0, mp - m), (0, 0)))
    tm = _pick_tile(mp, (512, 256, 128, 64, 32, 16, 8))
    tn = _pick_tile(n, (512, 384, 256, 128))
    out = pl.pallas_call(
        _mm_kernel,
        grid=(mp // tm, n // tn),
        in_specs=[pl.BlockSpec((tm, k), lambda i, j: (i, 0)),
                  pl.BlockSpec((k, tn), lambda i, j: (0, j))],
        out_specs=pl.BlockSpec((tm, tn), lambda i, j: (i, j)),
        out_shape=jax.ShapeDtypeStruct((mp, n), out_dtype),
        compiler_params=pltpu.CompilerParams(
            dimension_semantics=("parallel", "arbitrary"),
            vmem_limit_bytes=VMEM_LIMIT_BYTES),
        name="mm",
    )(a, b)
    return out[:m] if mp != m else out


def _mm3(a, b, out_dtype=F32):
    bsz, t, k = a.shape
    return _mm(a.reshape(bsz * t, k), b, out_dtype).reshape(bsz, t, b.shape[1])


def _mm_norm_kernel(x_ref, gm_ref, sh_ref, w_ref, o_ref, h_scr):
    @pl.when(pl.program_id(2) == 0)
    def _():
        xb = x_ref[0]
        ms = jnp.mean(xb * xb, axis=-1, keepdims=True)
        h_scr[...] = (xb * lax.rsqrt(ms + NORM_EPS) * gm_ref[0] + sh_ref[0]).astype(BF16)

    o_ref[0] = jnp.dot(h_scr[...], w_ref[...], preferred_element_type=F32)


def _mm_norm(x, gmod, shift, w):
    bsz, t, dm = x.shape
    n = w.shape[1]
    tm = _pick_tile(t, (512, 256, 128))
    tn = _pick_tile(n, (512, 384, 256, 128))
    vec_spec = pl.BlockSpec((1, 1, dm), lambda b, i, j: (b, 0, 0))
    return pl.pallas_call(
        _mm_norm_kernel,
        grid=(bsz, t // tm, n // tn),
        in_specs=[pl.BlockSpec((1, tm, dm), lambda b, i, j: (b, i, 0)), vec_spec, vec_spec,
                  pl.BlockSpec((dm, tn), lambda b, i, j: (0, j))],
        out_specs=pl.BlockSpec((1, tm, tn), lambda b, i, j: (b, i, j)),
        out_shape=jax.ShapeDtypeStruct((bsz, t, n), F32),
        scratch_shapes=[pltpu.VMEM((tm, dm), BF16)],
        compiler_params=pltpu.CompilerParams(
            dimension_semantics=("parallel", "parallel", "arbitrary"),
            vmem_limit_bytes=VMEM_LIMIT_BYTES),
        name="mm_norm",
    )(x, gmod, shift, w)


def _mix_out_kernel(rw_ref, cv_ref, cvp_ref, cvn_ref, rt_ref, x_ref, gate_ref, cw_ref, w_ref, o_ref, m_scr,
                    *, tm):
    i = pl.program_id(1)

    @pl.when(pl.program_id(2) == 0)
    def _():
        cw = cv_ref.shape[-1] // 3
        rw = rw_ref.shape[-1]
        zc = cv_ref[0]
        gch = zc[:, cw:2 * cw] * zc[:, 2 * cw:]
        before = jnp.where(i == 0, 0.0, cvp_ref[0, 7:8, cw:2 * cw] * cvp_ref[0, 7:8, 2 * cw:])
        after = jnp.where(i == pl.num_programs(1) - 1, 0.0,
                          cvn_ref[0, 0:1, cw:2 * cw] * cvn_ref[0, 0:1, 2 * cw:])
        trow = lax.broadcasted_iota(jnp.int32, (tm, 1), 0)
        prev = jnp.where(trow == 0, before, pltpu.roll(gch, 1, axis=0))
        nxt = jnp.where(trow == tm - 1, after, pltpu.roll(gch, tm - 1, axis=0))
        conv = zc[:, :cw] * (prev * cw_ref[0:1] + gch * cw_ref[1:2] + nxt * cw_ref[2:3])
        m_scr[:, :rw] = rw_ref[0].astype(BF16)
        m_scr[:, rw:rw + cw] = conv.astype(BF16)
        m_scr[:, rw + cw:] = rt_ref[0].astype(BF16)

    o_ref[0] = x_ref[0] + gate_ref[0] * jnp.dot(m_scr[...], w_ref[...], preferred_element_type=F32)


def _mix_out(out_rw, z_cv, out_rt, x, gate, w_out, conv_w):
    bsz, t, dm = x.shape
    rw, cvw, rtw = out_rw.shape[-1], z_cv.shape[-1], out_rt.shape[-1]
    tm = _pick_tile(t, (512, 256, 128))
    tn = _pick_tile(dm, (512, 256, 128))
    per = tm // 8
    nhb = t // 8
    return pl.pallas_call(
        functools.partial(_mix_out_kernel, tm=tm),
        grid=(bsz, t // tm, dm // tn),
        in_specs=[
            pl.BlockSpec((1, tm, rw), lambda b, i, j: (b, i, 0)),
            pl.BlockSpec((1, tm, cvw), lambda b, i, j: (b, i, 0)),
            pl.BlockSpec((1, 8, cvw), lambda b, i, j: (b, jnp.maximum(i * per - 1, 0), 0)),
            pl.BlockSpec((1, 8, cvw), lambda b, i, j: (b, jnp.minimum((i + 1) * per, nhb - 1), 0)),
            pl.BlockSpec((1, tm, rtw), lambda b, i, j: (b, i, 0)),
            pl.BlockSpec((1, tm, tn), lambda b, i, j: (b, i, j)),
            pl.BlockSpec((1, 1, tn), lambda b, i, j: (b, 0, j)),
            pl.BlockSpec((3, cvw // 3), lambda b, i, j: (0, 0)),
            pl.BlockSpec((rw + cvw // 3 + rtw, tn), lambda b, i, j: (0, j)),
        ],
        out_specs=pl.BlockSpec((1, tm, tn), lambda b, i, j: (b, i, j)),
        out_shape=jax.ShapeDtypeStruct((bsz, t, dm), F32),
        scratch_shapes=[pltpu.VMEM((tm, rw + cvw // 3 + rtw), BF16)],
        compiler_params=pltpu.CompilerParams(
            dimension_semantics=("parallel", "parallel", "arbitrary"),
            vmem_limit_bytes=VMEM_LIMIT_BYTES),
        name="mix_out",
    )(out_rw, z_cv, z_cv, z_cv, out_rt, x, gate, conv_w.T, w_out)


FFN_HALO = 16


def _ffn_kernel(*refs, tm, final):
    if final:
        (x_ref, xp_ref, xn_ref, gm_ref, sh_ref, gate_ref, upg_ref, upv_ref, cwg_ref, cwv_ref, dn_ref,
         fg_ref, o_ref, h_scr, acc_scr) = refs
    else:
        (x_ref, xp_ref, xn_ref, gm_ref, sh_ref, gate_ref, upg_ref, upv_ref, cwg_ref, cwv_ref, dn_ref,
         o_ref, h_scr, acc_scr) = refs
    i = pl.program_id(1)
    j = pl.program_id(2)
    halo = FFN_HALO
    rows = tm + 2 * halo

    @pl.when(j == 0)
    def _():
        def modnorm(xb):
            ms = jnp.mean(xb * xb, axis=-1, keepdims=True)
            return xb * lax.rsqrt(ms + NORM_EPS) * gm_ref[0] + sh_ref[0]

        hp = jnp.where(i == 0, 0.0, modnorm(xp_ref[0]))
        hn = jnp.where(i == pl.num_programs(1) - 1, 0.0, modnorm(xn_ref[0]))
        h_scr[0:halo] = hp.astype(BF16)
        h_scr[halo:halo + tm] = modnorm(x_ref[0]).astype(BF16)
        h_scr[halo + tm:rows] = hn.astype(BF16)
        acc_scr[...] = jnp.zeros_like(acc_scr)

    h = h_scr[...]

    def conv_half(up_ref, cw_ref):
        pm = jnp.dot(h, up_ref[...], preferred_element_type=F32)
        cw = cw_ref[...]
        u = (pltpu.roll(pm, 1, axis=0) * cw[0:1] + pm * cw[1:2]
             + pltpu.roll(pm, rows - 1, axis=0) * cw[2:3])
        return u[halo:halo + tm]

    gt = conv_half(upg_ref, cwg_ref)
    vl = conv_half(upv_ref, cwv_ref)
    act = gt * jax.nn.sigmoid(gt) * vl
    acc_scr[...] += jnp.dot(act.astype(BF16), dn_ref[...], preferred_element_type=F32)

    @pl.when(j == pl.num_programs(2) - 1)
    def _():
        out = x_ref[0] + gate_ref[0] * acc_scr[...]
        if final:
            ms = jnp.mean(out * out, axis=-1, keepdims=True)
            out = out * lax.rsqrt(ms + NORM_EPS) * fg_ref[...]
        o_ref[0] = out


def _ffn(x, gmod, shift, gate, up, cw, down, final_g=None):
    bsz, t, dm = x.shape
    f = down.shape[0]
    tm = _pick_tile(t, (512, 256, 128))
    tf = _pick_tile(f, (512, 256, 128))
    halo = FFN_HALO
    nf = f // tf
    per = tm // halo
    nhb = t // halo
    cwt = cw.T
    final = final_g is not None
    vec_spec = pl.BlockSpec((1, 1, dm), lambda b, i, j: (b, 0, 0))
    in_specs = [
        pl.BlockSpec((1, tm, dm), lambda b, i, j: (b, i, 0)),
        pl.BlockSpec((1, halo, dm), lambda b, i, j: (b, jnp.maximum(i * per - 1, 0), 0)),
        pl.BlockSpec((1, halo, dm), lambda b, i, j: (b, jnp.minimum((i + 1) * per, nhb - 1), 0)),
        vec_spec, vec_spec, vec_spec,
        pl.BlockSpec((dm, tf), lambda b, i, j: (0, j)),
        pl.BlockSpec((dm, tf), lambda b, i, j: (0, nf + j)),
        pl.BlockSpec((3, tf), lambda b, i, j: (0, j)),
        pl.BlockSpec((3, tf), lambda b, i, j: (0, nf + j)),
        pl.BlockSpec((tf, dm), lambda b, i, j: (j, 0)),
    ]
    args = [x, x, x, gmod, shift, gate, up, up, cwt, cwt, down]
    if final:
        in_specs.append(pl.BlockSpec((1, dm), lambda b, i, j: (0, 0)))
        args.append(final_g.reshape(1, dm))
    return pl.pallas_call(
        functools.partial(_ffn_kernel, tm=tm, final=final),
        grid=(bsz, t // tm, nf),
        in_specs=in_specs,
        out_specs=pl.BlockSpec((1, tm, dm), lambda b, i, j: (b, i, 0)),
        out_shape=jax.ShapeDtypeStruct((bsz, t, dm), F32),
        scratch_shapes=[pltpu.VMEM((tm + 2 * halo, dm), BF16), pltpu.VMEM((tm, dm), F32)],
        compiler_params=pltpu.CompilerParams(
            dimension_semantics=("parallel", "parallel", "arbitrary"),
            vmem_limit_bytes=VMEM_LIMIT_BYTES),
        name="ffn",
    )(*args)


def _split_bf16(x):
    hi = x.astype(BF16)
    lo = (x - hi.astype(F32)).astype(BF16)
    return hi, lo


def _dot1(a, b, dn=_NN):
    return lax.dot_general(a.astype(BF16), b.astype(BF16), dn, preferred_element_type=F32)


def _wkv_kernel(*refs, reverse, npairs, chunk):
    if reverse:
        (z_ref, zp_ref, zn_ref, mu_ref, kk_ref, ka_ref, w0_ref, a0_ref, lora_ref, s0_ref,
         yf_ref, a0f_ref, gup_ref, rk_ref, lng_ref, lnb_ref, y_ref, sT_ref, s_scr) = refs
    else:
        (z_ref, zp_ref, zn_ref, mu_ref, kk_ref, ka_ref, w0_ref, a0_ref, lora_ref, s0_ref,
         y_ref, sT_ref, s_scr) = refs
    c = chunk
    w2 = 2 * RWKV_HEAD_DIM
    rw = npairs * w2
    i = pl.program_id(1)
    nc = pl.num_programs(1)
    ci = nc - 1 - i if reverse else i

    @pl.when(i == 0)
    def _():
        s_scr[...] = s0_ref[0]

    row = lax.broadcasted_iota(jnp.int32, (c, c), 0)
    col = lax.broadcasted_iota(jnp.int32, (c, c), 1)
    if reverse:
        strict, incl = row < col, row <= col
    else:
        strict, incl = row > col, row >= col
    cum = incl.astype(BF16)
    last = 0 if reverse else c - 1
    first_head = lax.broadcasted_iota(jnp.int32, (c, w2), 1) < RWKV_HEAD_DIM
    first_head2 = jnp.concatenate([first_head, first_head], axis=1)
    brow = lax.broadcasted_iota(jnp.int32, (w2, w2), 0) < RWKV_HEAD_DIM
    bcol = lax.broadcasted_iota(jnp.int32, (w2, w2), 1) < RWKV_HEAD_DIM
    same_head = brow == bcol
    head_ones = same_head.astype(BF16)
    pairs = range(npairs)
    sls = [slice(q * w2, (q + 1) * w2) for q in pairs]

    def head_sum(u):
        hi, lo = _split_bf16(u)
        return jnp.concatenate(
            [jnp.dot(hi[:, sl], head_ones, preferred_element_type=F32)
             + jnp.dot(lo[:, sl], head_ones, preferred_element_type=F32) for sl in sls], axis=1)

    z = z_ref[0]
    trow = lax.broadcasted_iota(jnp.int32, (c, 1), 0)
    z_before = jnp.where(ci == 0, 0.0, zp_ref[0, 7:8, :])
    z_after = jnp.where(ci == nc - 1, 0.0, zn_ref[0, 0:1, :])
    zprev = jnp.where(trow == 0, z_before, pltpu.roll(z, 1, axis=0))
    znext = jnp.where(trow == c - 1, z_after, pltpu.roll(z, c - 1, axis=0))
    z = z + (0.5 * (zprev + znext) - z) * mu_ref[...]
    r = z[:, :rw]
    k = z[:, rw:2 * rw]
    v_all = z[:, 2 * rw:3 * rw]
    wa = z[:, 3 * rw:3 * rw + w2]
    gd = z[:, 3 * rw + w2:]

    kkr = k * kk_ref[...]
    kk = kkr / jnp.maximum(jnp.sqrt(head_sum(kkr * kkr)), 1e-12)
    decay_lane = lax.broadcasted_iota(jnp.int32, (c, w2), 1) < DECAY_LORA
    lora = jnp.dot(jnp.where(decay_lane, jnp.tanh(wa), wa).astype(BF16), lora_ref[...],
                   preferred_element_type=F32)
    xw = w0_ref[...] + lora[:, :rw]
    w_log = -(jnp.maximum(-xw, 0.0) + jnp.log1p(jnp.exp(-jnp.abs(xw)))) - 0.5
    lw = -jnp.exp(w_log)
    rate = jax.nn.sigmoid(a0_ref[...] + lora[:, rw:2 * rw])
    key = k * (1.0 + (rate - 1.0) * ka_ref[...])

    lw_hi, lw_lo = _split_bf16(lw)
    cl = (jnp.dot(cum, lw_hi, preferred_element_type=F32)
          + jnp.dot(cum, lw_lo, preferred_element_type=F32))
    p = jnp.exp(cl)
    pinv = jnp.exp(-cl)
    at_all = -kk * jnp.exp(cl - lw)
    bt_all = kk * rate * pinv
    kt_all = key * pinv
    rt_all = r * p

    def per_head(mats, rhs, mask):
        return jnp.where(mask, _dot1(mats[0], rhs), _dot1(mats[1], rhs))

    at = [at_all[:, sl] for sl in sls]
    bt = [bt_all[:, sl] for sl in sls]
    kt = [kt_all[:, sl] for sl in sls]
    rt = [rt_all[:, sl] for sl in sls]
    v = [v_all[:, sl] for sl in sls]

    a_ab, a_ak, m_rb, m_rk = [], [], [], []
    for q in pairs:
        ab_q, ak_q, rb_q, rk_q = [], [], [], []
        for hm in (first_head, jnp.logical_not(first_head)):
            lhs = jnp.concatenate([jnp.where(hm, at[q], 0.0), jnp.where(hm, rt[q], 0.0)], axis=0)
            mb = _dot1(lhs, bt[q], _NT)
            mk = _dot1(lhs, kt[q], _NT)
            ab_q.append(jnp.where(strict, mb[:c], 0.0))
            rb_q.append(jnp.where(incl, mb[c:], 0.0))
            ak_q.append(jnp.where(strict, mk[:c], 0.0))
            rk_q.append(jnp.where(incl, mk[c:], 0.0))
        a_ab.append(ab_q)
        a_ak.append(ak_q)
        m_rb.append(rb_q)
        m_rk.append(rk_q)

    x = [jnp.concatenate([at[q], per_head(a_ak[q], v[q], first_head)], axis=1) for q in pairs]
    ak = a_ab
    steps = max(1, (c - 1).bit_length())
    for s in range(steps):
        x = [x[q] + per_head(ak[q], x[q], first_head2) for q in pairs]
        if s + 1 < steps:
            ak = [[_dot1(ak[q][0], ak[q][0]), _dot1(ak[q][1], ak[q][1])] for q in pairs]

    rbx = [per_head(m_rb[q], x[q], first_head2) for q in pairs]
    rbar = [rt[q] + rbx[q][:, :w2] for q in pairs]
    y0 = [rbx[q][:, w2:] + per_head(m_rk[q], v[q], first_head) for q in pairs]
    g = [jnp.where(same_head, _dot1(x[q][:, :w2], bt[q], _TN), 0.0) for q in pairs]
    hh = [jnp.where(same_head,
                    _dot1(jnp.concatenate([x[q][:, w2:], v[q]], axis=0),
                          jnp.concatenate([bt[q], kt[q]], axis=0), _TN),
                    0.0) for q in pairs]

    ys = []
    for q in pairs:
        s0 = s_scr[q]
        ys.append(_dot1(rbar[q], s0, _NT) + y0[q])
        s_scr[q] = (s0 + _dot1(s0, g[q]) + hh[q]) * p[last:last + 1, sls[q]]
    y = jnp.concatenate(ys, axis=1)

    if reverse:
        y = yf_ref[0] + y
        inv_n = 1.0 / RWKV_HEAD_DIM
        dev = y - head_sum(y) * inv_n
        yn = dev * lax.rsqrt(head_sum(dev * dev) * inv_n + RWKV_GN_EPS)
        rate_f = jax.nn.sigmoid(a0f_ref[...] + lora[:, 2 * rw:])
        k_mid = 0.5 * (k * (1.0 + (rate_f - 1.0) * ka_ref[...]) + key)
        bonus = head_sum(r * k_mid * rk_ref[...]) * v_all
        gate = jnp.dot(jax.nn.sigmoid(gd).astype(BF16), gup_ref[...], preferred_element_type=F32)
        y = (yn * lng_ref[...] + lnb_ref[...] + bonus) * gate
    y_ref[0] = y

    @pl.when(i == pl.num_programs(1) - 1)
    def _():
        sT_ref[0] = s_scr[...]


def _wkv_scan(z, p, d, s0, y_fwd=None):
    mu, w0, w_up, a0, a_up, g_up, k_k, k_a, r_k, lnx_g, lnx_b = p
    bsz, t, zw = z.shape
    w = k_k.shape[0]
    w2 = 2 * RWKV_HEAD_DIM
    npairs = w // w2
    c = WKV_CHUNK
    nc = t // c
    reverse = d == 1
    per = c // 8
    nhb = t // 8
    pos = (lambda i: nc - 1 - i) if reverse else (lambda i: i)
    tok = lambda bi, i: (bi, pos(i), 0)
    row1 = lambda u: u.reshape(1, -1)
    zero = jnp.zeros((DECAY_LORA, w), F32)
    blocks = [jnp.concatenate([w_up[d], zero], axis=0), jnp.concatenate([zero, a_up[d]], axis=0)]
    if reverse:
        blocks.append(jnp.concatenate([zero, a_up[0]], axis=0))
    lora_w = jnp.concatenate(blocks, axis=1).astype(BF16)
    full = lambda arr: pl.BlockSpec(arr.shape, lambda bi, i: (0, 0))
    st_spec = pl.BlockSpec((1, npairs, w2, w2), lambda bi, i: (bi, 0, 0, 0))
    y_spec = pl.BlockSpec((1, c, w), tok)
    args = [z, z, z, row1(mu), row1(k_k), row1(k_a), row1(w0[d]), row1(a0[d]), lora_w, s0]
    in_specs = [
        pl.BlockSpec((1, c, zw), tok),
        pl.BlockSpec((1, 8, zw), lambda bi, i: (bi, jnp.maximum(pos(i) * per - 1, 0), 0)),
        pl.BlockSpec((1, 8, zw), lambda bi, i: (bi, jnp.minimum((pos(i) + 1) * per, nhb - 1), 0)),
    ] + [full(a) for a in args[3:9]] + [st_spec]
    if reverse:
        extra = [row1(a0[0]), g_up.astype(BF16), row1(r_k), row1(lnx_g), row1(lnx_b)]
        args += [y_fwd] + extra
        in_specs += [y_spec] + [full(a) for a in extra]
    return pl.pallas_call(
        functools.partial(_wkv_kernel, reverse=reverse, npairs=npairs, chunk=c),
        grid=(bsz, nc),
        in_specs=in_specs,
        out_specs=[y_spec, st_spec],
        out_shape=[jax.ShapeDtypeStruct((bsz, t, w), F32),
                   jax.ShapeDtypeStruct((bsz, npairs, w2, w2), F32)],
        scratch_shapes=[pltpu.VMEM((npairs, w2, w2), F32)],
        compiler_params=pltpu.CompilerParams(
            dimension_semantics=("parallel", "arbitrary"),
            vmem_limit_bytes=VMEM_LIMIT_BYTES),
        name="wkv_bwd" if reverse else "wkv_fwd",
    )(*args)


def _ret_kernel(*refs, nheads, reverse, use_rope):
    refs = list(refs)
    z_ref = refs.pop(0)
    cos_ref, sin_ref = (refs.pop(0), refs.pop(0)) if use_rope else (None, None)
    din_ref, xi_ref, zeta_ref, g_ref, r0_ref = refs[:5]
    refs = refs[5:]
    yf_ref, ng_ref = (refs.pop(0), refs.pop(0)) if reverse else (None, None)
    y_ref, rT_ref, r_scr = refs
    d = RET_HEAD_DIM
    w = nheads * d
    i = pl.program_id(1)

    @pl.when(i == 0)
    def _():
        r_scr[...] = r0_ref[0]

    def rope(u):
        return u * cos_ref[...] + pltpu.roll(u, d // 2, axis=1) * sin_ref[...]

    for h in range(nheads):
        q = z_ref[0, :, h * d:(h + 1) * d] * (d ** -0.5)
        k = z_ref[0, :, w + h * d:w + (h + 1) * d]
        v = z_ref[0, :, 2 * w + h * d:2 * w + (h + 1) * d]
        if use_rope:
            q, k = rope(q), rope(k)
        st = r_scr[h]
        s = _dot1(q, k, _NT) * din_ref[h]
        y = _dot1(s, v) + _dot1(q, st) * xi_ref[h]
        r_scr[h] = st * g_ref[h] + _dot1(k * zeta_ref[h], v, _TN)
        if reverse:
            y = yf_ref[0, :, h * d:(h + 1) * d] + y
            dev = y - jnp.mean(y, axis=-1, keepdims=True)
            yn = dev * lax.rsqrt(jnp.mean(dev * dev, axis=-1, keepdims=True) + NORM_EPS)
            gt = z_ref[0, :, 3 * w + h * d:3 * w + (h + 1) * d]
            y = gt * jax.nn.sigmoid(gt) * (yn * ng_ref[:, h * d:(h + 1) * d])
        y_ref[0, :, h * d:(h + 1) * d] = y

    @pl.when(i == pl.num_programs(1) - 1)
    def _():
        rT_ref[0] = r_scr[...]


def _ret_tables(log_gamma, reverse):
    c, d = RET_CHUNK, RET_HEAD_DIM
    idx = jnp.arange(c, dtype=F32)
    lg = log_gamma[:, None, None]
    if reverse:
        diff = idx[None, :] - idx[:, None]
        keep = diff > 0
        xi = jnp.exp(log_gamma[:, None] * (c - idx))
        zeta = jnp.exp(log_gamma[:, None] * idx)
    else:
        diff = idx[:, None] - idx[None, :]
        keep = diff >= 0
        xi = jnp.exp(log_gamma[:, None] * (idx + 1.0))
        zeta = jnp.exp(log_gamma[:, None] * (c - 1.0 - idx))
    din = jnp.where(keep, jnp.exp(lg * jnp.maximum(diff, 0.0)), 0.0)
    nh = log_gamma.shape[0]
    xi = jnp.broadcast_to(xi[:, :, None], (nh, c, d))
    zeta = jnp.broadcast_to(zeta[:, :, None], (nh, c, d))
    g = jnp.broadcast_to(jnp.exp(log_gamma * c)[:, None, None], (nh, d, d))
    return din, xi, zeta, g


def _ret_scan(z, rope, r0, log_gamma, reverse, y_fwd=None, norm_g=None):
    bsz, t, zw = z.shape
    d = RET_HEAD_DIM
    nh = zw // (4 * d)
    w = nh * d
    c = RET_CHUNK
    nc = t // c
    din, xi, zeta, g = _ret_tables(log_gamma, reverse)
    pos = (lambda i: nc - 1 - i) if reverse else (lambda i: i)
    tok = lambda bi, i: (bi, pos(i), 0)
    y_spec = pl.BlockSpec((1, c, w), tok)
    st_spec = pl.BlockSpec((1, nh, d, d), lambda bi, i: (bi, 0, 0, 0))
    tab_spec = lambda n: pl.BlockSpec((nh, n, d), lambda bi, i: (0, 0, 0))
    args = [z]
    in_specs = [pl.BlockSpec((1, c, zw), tok)]
    if rope is not None:
        args += list(rope)
        in_specs += [pl.BlockSpec((c, d), lambda bi, i: (pos(i), 0))] * 2
    args += [din, xi, zeta, g, r0]
    in_specs += [tab_spec(c), tab_spec(c), tab_spec(c), tab_spec(d), st_spec]
    if reverse:
        args += [y_fwd, norm_g.reshape(1, w)]
        in_specs += [y_spec, pl.BlockSpec((1, w), lambda bi, i: (0, 0))]
    return pl.pallas_call(
        functools.partial(_ret_kernel, nheads=nh, reverse=reverse, use_rope=rope is not None),
        grid=(bsz, nc),
        in_specs=in_specs,
        out_specs=[y_spec, st_spec],
        out_shape=[jax.ShapeDtypeStruct((bsz, t, w), F32),
                   jax.ShapeDtypeStruct((bsz, nh, d, d), F32)],
        scratch_shapes=[pltpu.VMEM((nh, d, d), F32)],
        compiler_params=pltpu.CompilerParams(
            dimension_semantics=("parallel", "arbitrary"),
            vmem_limit_bytes=VMEM_LIMIT_BYTES),
        name="ret_bwd" if reverse else "ret_fwd",
    )(*args)


def _rwkv_mix(z, p, s0_f, s0_b, want_out):
    y_f, s_f = _wkv_scan(z, p, 0, s0_f)
    out, s_b = _wkv_scan(z, p, 1, s0_b, y_f)
    return (out if want_out else None), s_f, s_b


def _ret_mix(z, rope, r0_f, r0_b, lg_f, lg_b, ret_norm_g, want_out):
    y_f, r_f = _ret_scan(z, rope, r0_f, lg_f, False)
    out, r_b = _ret_scan(z, rope, r0_b, lg_b, True, y_f, ret_norm_g)
    return (out if want_out else None), r_f, r_b


def kernel(x, c, ctx, c_ctx, mod_w, mod_b, norm1_g, norm2_g, w_in, tshift_mu, w0, w_up, a0, a_up, g_up,
           k_k, k_a, r_k, lnx_g, lnx_b, conv_w, ret_norm_g, w_out, ffn_up, ffn_conv, ffn_down,
           final_norm_g):
    bsz, n, dm = x.shape
    depth = mod_w.shape[0]
    rwkv_w = k_k.shape[1]
    rwkv_cols = tshift_mu.shape[1]
    conv_cols = 3 * conv_w.shape[1]
    ret_heads = ret_norm_g.shape[1] // RET_HEAD_DIM
    npairs = rwkv_w // (2 * RWKV_HEAD_DIM)

    rows = n // GRID_W
    row = jnp.repeat(jnp.arange(rows, dtype=F32), GRID_W)
    col = jnp.tile(jnp.arange(GRID_W, dtype=F32), rows)
    pairs = RET_HEAD_DIM // 4
    freqs = 1.0 / (ROPE_BASE ** (jnp.arange(pairs, dtype=F32) / pairs))
    ang = jnp.concatenate([row[:, None] * freqs, col[:, None] * freqs], axis=-1)
    cos, sin = jnp.cos(ang), jnp.sin(ang)
    rope = (jnp.concatenate([cos, cos], axis=-1), jnp.concatenate([-sin, sin], axis=-1))
    hidx = jnp.arange(ret_heads, dtype=F32)
    lg_f = jnp.log1p(-jnp.exp2(-5.0 - hidx))
    lg_b = jnp.log1p(-jnp.exp2(-5.5 - hidx))
    s_zero = jnp.zeros((bsz, npairs, 2 * RWKV_HEAD_DIM, 2 * RWKV_HEAD_DIM), F32)
    r_zero = jnp.zeros((bsz, ret_heads, RET_HEAD_DIM, RET_HEAD_DIM), F32)

    xl = x.astype(F32)
    xc = ctx.astype(F32)
    silu_all = jnp.concatenate([jax.nn.silu(c.astype(F32)), jax.nn.silu(c_ctx.astype(F32))[None]], axis=0)

    for l in range(depth):
        last = l == depth - 1
        mod = _mm(silu_all, mod_w[l]) + mod_b[l]
        ml = [m[:, None, :] for m in jnp.split(mod[:bsz], 6, axis=-1)]
        mc = jnp.split(mod[bsz], 6, axis=-1)
        rwkv_p = (tshift_mu[l], w0[l], w_up[l], a0[l], a_up[l], g_up[l], k_k[l], k_a[l],
                  r_k[l], lnx_g[l], lnx_b[l])
        w_in_l = w_in[l].astype(BF16)
        w_out_l = w_out[l].astype(BF16)
        up_l = ffn_up[l].astype(BF16)
        down_l = ffn_down[l].astype(BF16)

        bc = lambda m: jnp.broadcast_to(m, (bsz, 1, dm))
        s1, s2 = rwkv_cols, rwkv_cols + conv_cols
        w_groups = (w_in_l[:, :s1], w_in_l[:, s1:s2], w_in_l[:, s2:])
        gm_c, sh_c = bc(norm1_g[l] * (1.0 + mc[1])), bc(mc[0])
        gm_l, sh_l = norm1_g[l] * (1.0 + ml[1]), ml[0]
        zc_rw, zc_cv, zc_rt = [_mm_norm(xc, gm_c, sh_c, wg) for wg in w_groups]
        zl_rw, zl_cv, zl_rt = [_mm_norm(xl, gm_l, sh_l, wg) for wg in w_groups]

        out_c_rw, s_f, s_b = _rwkv_mix(zc_rw, rwkv_p, s_zero, s_zero, not last)
        out_l_rw, _, _ = _rwkv_mix(zl_rw, rwkv_p, s_f, s_b, True)
        out_c_rt, r_f, r_b = _ret_mix(zc_rt, None, r_zero, r_zero, lg_f, lg_b, ret_norm_g[l], not last)
        out_l_rt, _, _ = _ret_mix(zl_rt, rope, r_f, r_b, lg_f, lg_b, ret_norm_g[l], True)

        xl = _mix_out(out_l_rw, zl_cv, out_l_rt, xl, ml[2], w_out_l, conv_w[l])
        xl = _ffn(xl, norm2_g[l] * (1.0 + ml[4]), ml[3], ml[5], up_l, ffn_conv[l], down_l,
                  final_norm_g if last else None)
        if not last:
            xc = _mix_out(out_c_rw, zc_cv, out_c_rt, xc, bc(mc[2]), w_out_l, conv_w[l])
            xc = _ffn(xc, bc(norm2_g[l] * (1.0 + mc[4])), bc(mc[3]), bc(mc[5]), up_l, ffn_conv[l], down_l)

    return xl.astype(x.dtype)
```

```python
import functools

import jax
import jax.numpy as jnp
from jax import lax
from jax.experimental import pallas as pl
from jax.experimental.pallas import tpu as pltpu

F32 = jnp.float32
BF16 = jnp.bfloat16

GRID_W = 64
RWKV_HEAD_DIM = 64
DECAY_LORA = 64
AAA_LORA = 64
GATE_LORA = 128
RET_HEAD_DIM = 128
RET_CHUNK = 128
WKV_CHUNK = 64
WKV_CHUNKS_PER_STEP = 4
ROPE_BASE = 10000.0
NORM_EPS = 1e-6
RWKV_GN_EPS = 64e-5

VMEM_LIMIT_BYTES = 48 * 1024 * 1024

_NT = (((1,), (1,)), ((), ()))
_TN = (((0,), (0,)), ((), ()))
_NN = (((1,), (0,)), ((), ()))


def _mm_kernel(a_ref, b_ref, o_ref):
    o_ref[...] = jnp.dot(a_ref[...].astype(BF16), b_ref[...],
                         preferred_element_type=F32).astype(o_ref.dtype)


def _pick_tile(n, candidates):
    for c in candidates:
        if n % c == 0:
            return c
    return n


def _mm(a, b, out_dtype=F32):
    m, k = a.shape
    _, n = b.shape
    b = b.astype(BF16)
    mp = -(-m // 8) * 8
    if mp != m:
        a = jnp.pad(a, ((0, mp - m), (0, 0)))
    tm = _pick_tile(mp, (512, 256, 128, 64, 32, 16, 8))
    tn = _pick_tile(n, (512, 384, 256, 128))
    out = pl.pallas_call(
        _mm_kernel,
        grid=(mp // tm, n // tn),
        in_specs=[pl.BlockSpec((tm, k), lambda i, j: (i, 0)),
                  pl.BlockSpec((k, tn), lambda i, j: (0, j))],
        out_specs=pl.BlockSpec((tm, tn), lambda i, j: (i, j)),
        out_shape=jax.ShapeDtypeStruct((mp, n), out_dtype),
        compiler_params=pltpu.CompilerParams(
            dimension_semantics=("parallel", "arbitrary"),
            vmem_limit_bytes=VMEM_LIMIT_BYTES),
        name="mm",
    )(a, b)
    return out[:m] if mp != m else out


def _mm3(a, b, out_dtype=F32):
    bsz, t, k = a.shape
    return _mm(a.reshape(bsz * t, k), b, out_dtype).reshape(bsz, t, b.shape[1])


def _resident(shape):
    return pl.BlockSpec(shape, lambda *_: (0,) * len(shape), pipeline_mode=pl.Buffered(1))


def _mm_norm_kernel(x_ref, gm_ref, sh_ref, w_ref, o_ref):
    xb = x_ref[0]
    ms = jnp.mean(xb * xb, axis=-1, keepdims=True)
    h = (xb * lax.rsqrt(ms + NORM_EPS) * gm_ref[0] + sh_ref[0]).astype(BF16)
    o_ref[0] = jnp.dot(h, w_ref[...], preferred_element_type=F32)


def _mm_norm(x, gmod, shift, w):
    bsz, t, dm = x.shape
    n = w.shape[1]
    tm = _pick_tile(t, (512, 256, 128))
    vec_spec = pl.BlockSpec((1, 1, dm), lambda b, i: (b, 0, 0))
    return pl.pallas_call(
        _mm_norm_kernel,
        grid=(bsz, t // tm),
        in_specs=[pl.BlockSpec((1, tm, dm), lambda b, i: (b, i, 0)), vec_spec, vec_spec,
                  _resident((dm, n))],
        out_specs=pl.BlockSpec((1, tm, n), lambda b, i: (b, i, 0)),
        out_shape=jax.ShapeDtypeStruct((bsz, t, n), F32),
        compiler_params=pltpu.CompilerParams(
            dimension_semantics=("parallel", "parallel"),
            vmem_limit_bytes=VMEM_LIMIT_BYTES),
        name="mm_norm",
    )(x, gmod, shift, w)


def _mix_out_kernel(rw_ref, cv_ref, cvp_ref, cvn_ref, rt_ref, x_ref, gate_ref, cw_ref, w_ref, o_ref, *, tm):
    i = pl.program_id(1)
    cw = cv_ref.shape[-1] // 3
    zc = cv_ref[0]
    gch = zc[:, cw:2 * cw] * zc[:, 2 * cw:]
    before = jnp.where(i == 0, 0.0, cvp_ref[0, 7:8, cw:2 * cw] * cvp_ref[0, 7:8, 2 * cw:])
    after = jnp.where(i == pl.num_programs(1) - 1, 0.0,
                      cvn_ref[0, 0:1, cw:2 * cw] * cvn_ref[0, 0:1, 2 * cw:])
    trow = lax.broadcasted_iota(jnp.int32, (tm, 1), 0)
    prev = jnp.where(trow == 0, before, pltpu.roll(gch, 1, axis=0))
    nxt = jnp.where(trow == tm - 1, after, pltpu.roll(gch, tm - 1, axis=0))
    conv = zc[:, :cw] * (prev * cw_ref[0:1] + gch * cw_ref[1:2] + nxt * cw_ref[2:3])
    mix = jnp.concatenate([rw_ref[0].astype(BF16), conv.astype(BF16), rt_ref[0].astype(BF16)], axis=1)
    o_ref[0] = x_ref[0] + gate_ref[0] * jnp.dot(mix, w_ref[...], preferred_element_type=F32)


def _mix_out(out_rw, z_cv, out_rt, x, gate, w_out, conv_w):
    bsz, t, dm = x.shape
    rw, cvw, rtw = out_rw.shape[-1], z_cv.shape[-1], out_rt.shape[-1]
    tm = _pick_tile(t, (512, 256, 128))
    per = tm // 8
    nhb = t // 8
    return pl.pallas_call(
        functools.partial(_mix_out_kernel, tm=tm),
        grid=(bsz, t // tm),
        in_specs=[
            pl.BlockSpec((1, tm, rw), lambda b, i: (b, i, 0)),
            pl.BlockSpec((1, tm, cvw), lambda b, i: (b, i, 0)),
            pl.BlockSpec((1, 8, cvw), lambda b, i: (b, jnp.maximum(i * per - 1, 0), 0)),
            pl.BlockSpec((1, 8, cvw), lambda b, i: (b, jnp.minimum((i + 1) * per, nhb - 1), 0)),
            pl.BlockSpec((1, tm, rtw), lambda b, i: (b, i, 0)),
            pl.BlockSpec((1, tm, dm), lambda b, i: (b, i, 0)),
            pl.BlockSpec((1, 1, dm), lambda b, i: (b, 0, 0)),
            _resident((3, cvw // 3)),
            _resident((rw + cvw // 3 + rtw, dm)),
        ],
        out_specs=pl.BlockSpec((1, tm, dm), lambda b, i: (b, i, 0)),
        out_shape=jax.ShapeDtypeStruct((bsz, t, dm), F32),
        compiler_params=pltpu.CompilerParams(
            dimension_semantics=("parallel", "parallel"),
            vmem_limit_bytes=VMEM_LIMIT_BYTES),
        name="mix_out",
    )(out_rw, z_cv, z_cv, z_cv, out_rt, x, gate, conv_w.T, w_out)


FFN_HALO = 16


def _ffn_kernel(*refs, tm, final):
    if final:
        (x_ref, xp_ref, xn_ref, gm_ref, sh_ref, gate_ref, upg_ref, upv_ref, cwg_ref, cwv_ref, dn_ref,
         fg_ref, o_ref, h_scr, acc_scr) = refs
    else:
        (x_ref, xp_ref, xn_ref, gm_ref, sh_ref, gate_ref, upg_ref, upv_ref, cwg_ref, cwv_ref, dn_ref,
         o_ref, h_scr, acc_scr) = refs
    i = pl.program_id(1)
    j = pl.program_id(2)
    halo = FFN_HALO
    rows = tm + 2 * halo

    @pl.when(j == 0)
    def _():
        def modnorm(xb):
            ms = jnp.mean(xb * xb, axis=-1, keepdims=True)
            return xb * lax.rsqrt(ms + NORM_EPS) * gm_ref[0] + sh_ref[0]

        hp = jnp.where(i == 0, 0.0, modnorm(xp_ref[0]))
        hn = jnp.where(i == pl.num_programs(1) - 1, 0.0, modnorm(xn_ref[0]))
        h_scr[0:halo] = hp.astype(BF16)
        h_scr[halo:halo + tm] = modnorm(x_ref[0]).astype(BF16)
        h_scr[halo + tm:rows] = hn.astype(BF16)
        acc_scr[...] = jnp.zeros_like(acc_scr)

    h = h_scr[...]

    def conv_half(up_ref, cw_ref):
        pm = jnp.dot(h, up_ref[...], preferred_element_type=F32)
        cw = cw_ref[...]
        u = (pltpu.roll(pm, 1, axis=0) * cw[0:1] + pm * cw[1:2]
             + pltpu.roll(pm, rows - 1, axis=0) * cw[2:3])
        return u[halo:halo + tm]

    gt = conv_half(upg_ref, cwg_ref)
    vl = conv_half(upv_ref, cwv_ref)
    act = gt * jax.nn.sigmoid(gt) * vl
    acc_scr[...] += jnp.dot(act.astype(BF16), dn_ref[...], preferred_element_type=F32)

    @pl.when(j == pl.num_programs(2) - 1)
    def _():
        out = x_ref[0] + gate_ref[0] * acc_scr[...]
        if final:
            ms = jnp.mean(out * out, axis=-1, keepdims=True)
            out = out * lax.rsqrt(ms + NORM_EPS) * fg_ref[...]
        o_ref[0] = out


def _ffn(x, gmod, shift, gate, up, cw, down, final_g=None):
    bsz, t, dm = x.shape
    f = down.shape[0]
    tm = _pick_tile(t, (512, 256, 128))
    tf = _pick_tile(f, (512, 256, 128))
    halo = FFN_HALO
    nf = f // tf
    per = tm // halo
    nhb = t // halo
    cwt = cw.T
    final = final_g is not None
    vec_spec = pl.BlockSpec((1, 1, dm), lambda b, i, j: (b, 0, 0))
    in_specs = [
        pl.BlockSpec((1, tm, dm), lambda b, i, j: (b, i, 0)),
        pl.BlockSpec((1, halo, dm), lambda b, i, j: (b, jnp.maximum(i * per - 1, 0), 0)),
        pl.BlockSpec((1, halo, dm), lambda b, i, j: (b, jnp.minimum((i + 1) * per, nhb - 1), 0)),
        vec_spec, vec_spec, vec_spec,
        pl.BlockSpec((dm, tf), lambda b, i, j: (0, j)),
        pl.BlockSpec((dm, tf), lambda b, i, j: (0, nf + j)),
        pl.BlockSpec((3, tf), lambda b, i, j: (0, j)),
        pl.BlockSpec((3, tf), lambda b, i, j: (0, nf + j)),
        pl.BlockSpec((tf, dm), lambda b, i, j: (j, 0)),
    ]
    args = [x, x, x, gmod, shift, gate, up, up, cwt, cwt, down]
    if final:
        in_specs.append(pl.BlockSpec((1, dm), lambda b, i, j: (0, 0)))
        args.append(final_g.reshape(1, dm))
    return pl.pallas_call(
        functools.partial(_ffn_kernel, tm=tm, final=final),
        grid=(bsz, t // tm, nf),
        in_specs=in_specs,
        out_specs=pl.BlockSpec((1, tm, dm), lambda b, i, j: (b, i, 0)),
        out_shape=jax.ShapeDtypeStruct((bsz, t, dm), F32),
        scratch_shapes=[pltpu.VMEM((tm + 2 * halo, dm), BF16), pltpu.VMEM((tm, dm), F32)],
        compiler_params=pltpu.CompilerParams(
            dimension_semantics=("parallel", "parallel", "arbitrary"),
            vmem_limit_bytes=VMEM_LIMIT_BYTES),
        name="ffn",
    )(*args)


def _split_bf16(x):
    hi = x.astype(BF16)
    lo = (x - hi.astype(F32)).astype(BF16)
    return hi, lo


def _dot1(a, b, dn=_NN):
    return lax.dot_general(a.astype(BF16), b.astype(BF16), dn, preferred_element_type=F32)


def _wkv_kernel(*refs, reverse, npairs, chunk, nsub):
    if reverse:
        (z_ref, zp_ref, zn_ref, mu_ref, kk_ref, ka_ref, w0_ref, a0_ref, lora_ref, s0_ref,
         yf_ref, a0f_ref, gup_ref, rk_ref, lng_ref, lnb_ref, y_ref, sT_ref, s_scr) = refs
    else:
        (z_ref, zp_ref, zn_ref, mu_ref, kk_ref, ka_ref, w0_ref, a0_ref, lora_ref, s0_ref,
         y_ref, sT_ref, s_scr) = refs
    c = chunk
    cb = nsub * c
    w2 = 2 * RWKV_HEAD_DIM
    rw = npairs * w2
    i = pl.program_id(1)
    nc = pl.num_programs(1)
    ci = nc - 1 - i if reverse else i

    @pl.when(i == 0)
    def _():
        s_scr[...] = s0_ref[0]

    row = lax.broadcasted_iota(jnp.int32, (cb, cb), 0)
    col = lax.broadcasted_iota(jnp.int32, (cb, cb), 1)
    same_chunk = row // c == col // c
    cum = (same_chunk & (row <= col if reverse else row >= col)).astype(BF16)
    last = 0 if reverse else c - 1
    first_head = lax.broadcasted_iota(jnp.int32, (c, w2), 1) < RWKV_HEAD_DIM
    first_head2 = jnp.concatenate([first_head, first_head], axis=1)
    brow = lax.broadcasted_iota(jnp.int32, (w2, w2), 0) < RWKV_HEAD_DIM
    bcol = lax.broadcasted_iota(jnp.int32, (w2, w2), 1) < RWKV_HEAD_DIM
    same_head = brow == bcol
    head_ones = same_head.astype(BF16)
    trow = lax.broadcasted_iota(jnp.int32, (c, 2 * c), 0)
    tcol = lax.broadcasted_iota(jnp.int32, (c, 2 * c), 1)
    tcol = jnp.where(tcol < c, tcol, tcol - c)
    if reverse:
        strict, incl = trow < tcol, trow <= tcol
    else:
        strict, incl = trow > tcol, trow >= tcol
    pairs = range(npairs)
    sls = [slice(q * w2, (q + 1) * w2) for q in pairs]

    def head_sum(u):
        hi, lo = _split_bf16(u)
        return jnp.concatenate(
            [jnp.dot(hi[:, sl], head_ones, preferred_element_type=F32)
             + jnp.dot(lo[:, sl], head_ones, preferred_element_type=F32) for sl in sls], axis=1)

    z = z_ref[0]
    trow = lax.broadcasted_iota(jnp.int32, (cb, 1), 0)
    z_before = jnp.where(ci == 0, 0.0, zp_ref[0, 7:8, :])
    z_after = jnp.where(ci == nc - 1, 0.0, zn_ref[0, 0:1, :])
    zprev = jnp.where(trow == 0, z_before, pltpu.roll(z, 1, axis=0))
    znext = jnp.where(trow == cb - 1, z_after, pltpu.roll(z, cb - 1, axis=0))
    z = z + (0.5 * (zprev + znext) - z) * mu_ref[...]
    r = z[:, :rw]
    k = z[:, rw:2 * rw]
    v_all = z[:, 2 * rw:3 * rw]
    wa = z[:, 3 * rw:3 * rw + w2]
    gd = z[:, 3 * rw + w2:]

    kkr = k * kk_ref[...]
    kk = kkr / jnp.maximum(jnp.sqrt(head_sum(kkr * kkr)), 1e-12)
    decay_lane = lax.broadcasted_iota(jnp.int32, (cb, w2), 1) < DECAY_LORA
    lora = jnp.dot(jnp.where(decay_lane, jnp.tanh(wa), wa).astype(BF16), lora_ref[...],
                   preferred_element_type=F32)
    xw = w0_ref[...] + lora[:, :rw]
    w_log = -(jnp.maximum(-xw, 0.0) + jnp.log1p(jnp.exp(-jnp.abs(xw)))) - 0.5
    lw = -jnp.exp(w_log)
    rate = jax.nn.sigmoid(a0_ref[...] + lora[:, rw:2 * rw])
    key = k * (1.0 + (rate - 1.0) * ka_ref[...])

    lw_hi, lw_lo = _split_bf16(lw)
    cl = (jnp.dot(cum, lw_hi, preferred_element_type=F32)
          + jnp.dot(cum, lw_lo, preferred_element_type=F32))
    p = jnp.exp(cl)
    pinv = jnp.exp(-cl)
    at_all = -kk * jnp.exp(cl - lw)
    bt_all = kk * rate * pinv
    kt_all = key * pinv
    rt_all = r * p

    def stack(u):
        m = first_head if u.shape[1] == w2 else first_head2
        return jnp.concatenate([jnp.where(m, u, 0.0), jnp.where(m, 0.0, u)], axis=0)

    def pdot(mat, u):
        return _dot1(mat, stack(u))

    units = [(j, q) for j in range(nsub) for q in pairs]
    cut = lambda u: [u[j * c:(j + 1) * c, sls[q]] for j, q in units]
    at, bt, kt, rt, v = cut(at_all), cut(bt_all), cut(kt_all), cut(rt_all), cut(v_all)
    us = range(len(units))

    a_ab, a_ak, m_rb, m_rk = [], [], [], []
    for u in us:
        lhs = jnp.concatenate([at[u], rt[u]], axis=0)
        mb = _dot1(lhs, stack(bt[u]), _NT)
        mk = _dot1(lhs, stack(kt[u]), _NT)
        a_ab.append(jnp.where(strict, mb[:c], 0.0))
        m_rb.append(jnp.where(incl, mb[c:], 0.0))
        a_ak.append(jnp.where(strict, mk[:c], 0.0))
        m_rk.append(jnp.where(incl, mk[c:], 0.0))

    x = [jnp.concatenate([at[u], pdot(a_ak[u], v[u])], axis=1) for u in us]
    ak = a_ab
    steps = max(1, (c - 1).bit_length())
    for s in range(steps):
        x = [x[u] + pdot(ak[u], x[u]) for u in us]
        if s + 1 < steps:
            ak = [pdot(ak[u], ak[u]) for u in us]

    rbx = [pdot(m_rb[u], x[u]) for u in us]
    rbar = [rt[u] + rbx[u][:, :w2] for u in us]
    y0 = [rbx[u][:, w2:] + pdot(m_rk[u], v[u]) for u in us]
    g = [jnp.where(same_head, _dot1(x[u][:, :w2], bt[u], _TN), 0.0) for u in us]
    hh = [jnp.where(same_head,
                    _dot1(jnp.concatenate([x[u][:, w2:], v[u]], axis=0),
                          jnp.concatenate([bt[u], kt[u]], axis=0), _TN),
                    0.0) for u in us]

    ys = [None] * len(units)
    state = [s_scr[q] for q in pairs]
    for j in (reversed(range(nsub)) if reverse else range(nsub)):
        for q in pairs:
            u = j * npairs + q
            s0 = state[q]
            ys[u] = _dot1(rbar[u], s0, _NT) + y0[u]
            state[q] = (s0 + _dot1(s0, g[u]) + hh[u]) * p[j * c + last:j * c + last + 1, sls[q]]
    for q in pairs:
        s_scr[q] = state[q]
    y = jnp.concatenate([jnp.concatenate(ys[j * npairs:(j + 1) * npairs], axis=1)
                         for j in range(nsub)], axis=0)

    if reverse:
        y = yf_ref[0] + y
        inv_n = 1.0 / RWKV_HEAD_DIM
        dev = y - head_sum(y) * inv_n
        yn = dev * lax.rsqrt(head_sum(dev * dev) * inv_n + RWKV_GN_EPS)
        rate_f = jax.nn.sigmoid(a0f_ref[...] + lora[:, 2 * rw:])
        k_mid = 0.5 * (k * (1.0 + (rate_f - 1.0) * ka_ref[...]) + key)
        bonus = head_sum(r * k_mid * rk_ref[...]) * v_all
        gate = jnp.dot(jax.nn.sigmoid(gd).astype(BF16), gup_ref[...], preferred_element_type=F32)
        y = (yn * lng_ref[...] + lnb_ref[...] + bonus) * gate
    y_ref[0] = y

    @pl.when(i == pl.num_programs(1) - 1)
    def _():
        sT_ref[0] = s_scr[...]


def _wkv_scan(z, p, d, s0, y_fwd=None):
    mu, w0, w_up, a0, a_up, g_up, k_k, k_a, r_k, lnx_g, lnx_b = p
    bsz, t, zw = z.shape
    w = k_k.shape[0]
    w2 = 2 * RWKV_HEAD_DIM
    npairs = w // w2
    c = WKV_CHUNK
    assert c == RWKV_HEAD_DIM and t % c == 0 and w % w2 == 0
    nsub = WKV_CHUNKS_PER_STEP if (t // c) % WKV_CHUNKS_PER_STEP == 0 else 1
    cb = nsub * c
    nc = t // cb
    reverse = d == 1
    per = cb // 8
    nhb = t // 8
    pos = (lambda i: nc - 1 - i) if reverse else (lambda i: i)
    tok = lambda bi, i: (bi, pos(i), 0)
    row1 = lambda u: u.reshape(1, -1)
    zero = jnp.zeros((DECAY_LORA, w), F32)
    blocks = [jnp.concatenate([w_up[d], zero], axis=0), jnp.concatenate([zero, a_up[d]], axis=0)]
    if reverse:
        blocks.append(jnp.concatenate([zero, a_up[0]], axis=0))
    lora_w = jnp.concatenate(blocks, axis=1).astype(BF16)
    full = lambda arr: pl.BlockSpec(arr.shape, lambda bi, i: (0, 0))
    st_spec = pl.BlockSpec((1, npairs, w2, w2), lambda bi, i: (bi, 0, 0, 0))
    y_spec = pl.BlockSpec((1, cb, w), tok)
    args = [z, z, z, row1(mu), row1(k_k), row1(k_a), row1(w0[d]), row1(a0[d]), lora_w, s0]
    in_specs = [
        pl.BlockSpec((1, cb, zw), tok),
        pl.BlockSpec((1, 8, zw), lambda bi, i: (bi, jnp.maximum(pos(i) * per - 1, 0), 0)),
        pl.BlockSpec((1, 8, zw), lambda bi, i: (bi, jnp.minimum((pos(i) + 1) * per, nhb - 1), 0)),
    ] + [full(a) for a in args[3:9]] + [st_spec]
    if reverse:
        extra = [row1(a0[0]), g_up.astype(BF16), row1(r_k), row1(lnx_g), row1(lnx_b)]
        args += [y_fwd] + extra
        in_specs += [y_spec] + [full(a) for a in extra]
    return pl.pallas_call(
        functools.partial(_wkv_kernel, reverse=reverse, npairs=npairs, chunk=c, nsub=nsub),
        grid=(bsz, nc),
        in_specs=in_specs,
        out_specs=[y_spec, st_spec],
        out_shape=[jax.ShapeDtypeStruct((bsz, t, w), F32),
                   jax.ShapeDtypeStruct((bsz, npairs, w2, w2), F32)],
        scratch_shapes=[pltpu.VMEM((npairs, w2, w2), F32)],
        compiler_params=pltpu.CompilerParams(
            dimension_semantics=("parallel", "arbitrary"),
            vmem_limit_bytes=VMEM_LIMIT_BYTES),
        name="wkv_bwd" if reverse else "wkv_fwd",
    )(*args)


def _ret_kernel(*refs, nheads, reverse, use_rope):
    refs = list(refs)
    z_ref = refs.pop(0)
    cos_ref, sin_ref = (refs.pop(0), refs.pop(0)) if use_rope else (None, None)
    din_ref, xi_ref, zeta_ref, g_ref, r0_ref = refs[:5]
    refs = refs[5:]
    yf_ref, ng_ref = (refs.pop(0), refs.pop(0)) if reverse else (None, None)
    y_ref, rT_ref, r_scr = refs
    d = RET_HEAD_DIM
    w = nheads * d
    i = pl.program_id(1)

    @pl.when(i == 0)
    def _():
        r_scr[...] = r0_ref[0]

    def rope(u):
        return u * cos_ref[...] + pltpu.roll(u, d // 2, axis=1) * sin_ref[...]

    for h in range(nheads):
        q = z_ref[0, :, h * d:(h + 1) * d] * (d ** -0.5)
        k = z_ref[0, :, w + h * d:w + (h + 1) * d]
        v = z_ref[0, :, 2 * w + h * d:2 * w + (h + 1) * d]
        if use_rope:
            q, k = rope(q), rope(k)
        st = r_scr[h]
        s = _dot1(q, k, _NT) * din_ref[h]
        y = _dot1(s, v) + _dot1(q, st) * xi_ref[h]
        r_scr[h] = st * g_ref[h] + _dot1(k * zeta_ref[h], v, _TN)
        if reverse:
            y = yf_ref[0, :, h * d:(h + 1) * d] + y
            dev = y - jnp.mean(y, axis=-1, keepdims=True)
            yn = dev * lax.rsqrt(jnp.mean(dev * dev, axis=-1, keepdims=True) + NORM_EPS)
            gt = z_ref[0, :, 3 * w + h * d:3 * w + (h + 1) * d]
            y = gt * jax.nn.sigmoid(gt) * (yn * ng_ref[:, h * d:(h + 1) * d])
        y_ref[0, :, h * d:(h + 1) * d] = y

    @pl.when(i == pl.num_programs(1) - 1)
    def _():
        rT_ref[0] = r_scr[...]


def _ret_tables(log_gamma, reverse):
    c, d = RET_CHUNK, RET_HEAD_DIM
    idx = jnp.arange(c, dtype=F32)
    lg = log_gamma[:, None, None]
    if reverse:
        diff = idx[None, :] - idx[:, None]
        keep = diff > 0
        xi = jnp.exp(log_gamma[:, None] * (c - idx))
        zeta = jnp.exp(log_gamma[:, None] * idx)
    else:
        diff = idx[:, None] - idx[None, :]
        keep = diff >= 0
        xi = jnp.exp(log_gamma[:, None] * (idx + 1.0))
        zeta = jnp.exp(log_gamma[:, None] * (c - 1.0 - idx))
    din = jnp.where(keep, jnp.exp(lg * jnp.maximum(diff, 0.0)), 0.0)
    nh = log_gamma.shape[0]
    xi = jnp.broadcast_to(xi[:, :, None], (nh, c, d))
    zeta = jnp.broadcast_to(zeta[:, :, None], (nh, c, d))
    g = jnp.broadcast_to(jnp.exp(log_gamma * c)[:, None, None], (nh, d, d))
    return din, xi, zeta, g


def _ret_scan(z, rope, r0, log_gamma, reverse, y_fwd=None, norm_g=None):
    bsz, t, zw = z.shape
    d = RET_HEAD_DIM
    nh = zw // (4 * d)
    w = nh * d
    c = RET_CHUNK
    nc = t // c
    din, xi, zeta, g = _ret_tables(log_gamma, reverse)
    pos = (lambda i: nc - 1 - i) if reverse else (lambda i: i)
    tok = lambda bi, i: (bi, pos(i), 0)
    y_spec = pl.BlockSpec((1, c, w), tok)
    st_spec = pl.BlockSpec((1, nh, d, d), lambda bi, i: (bi, 0, 0, 0))
    tab_spec = lambda n: pl.BlockSpec((nh, n, d), lambda bi, i: (0, 0, 0))
    args = [z]
    in_specs = [pl.BlockSpec((1, c, zw), tok)]
    if rope is not None:
        args += list(rope)
        in_specs += [pl.BlockSpec((c, d), lambda bi, i: (pos(i), 0))] * 2
    args += [din, xi, zeta, g, r0]
    in_specs += [tab_spec(c), tab_spec(c), tab_spec(c), tab_spec(d), st_spec]
    if reverse:
        args += [y_fwd, norm_g.reshape(1, w)]
        in_specs += [y_spec, pl.BlockSpec((1, w), lambda bi, i: (0, 0))]
    return pl.pallas_call(
        functools.partial(_ret_kernel, nheads=nh, reverse=reverse, use_rope=rope is not None),
        grid=(bsz, nc),
        in_specs=in_specs,
        out_specs=[y_spec, st_spec],
        out_shape=[jax.ShapeDtypeStruct((bsz, t, w), F32),
                   jax.ShapeDtypeStruct((bsz, nh, d, d), F32)],
        scratch_shapes=[pltpu.VMEM((nh, d, d), F32)],
        compiler_params=pltpu.CompilerParams(
            dimension_semantics=("parallel", "arbitrary"),
            vmem_limit_bytes=VMEM_LIMIT_BYTES),
        name="ret_bwd" if reverse else "ret_fwd",
    )(*args)


def _rwkv_mix(z, p, s0_f, s0_b, want_out):
    y_f, s_f = _wkv_scan(z, p, 0, s0_f)
    out, s_b = _wkv_scan(z, p, 1, s0_b, y_f)
    return (out if want_out else None), s_f, s_b


def _ret_mix(z, rope, r0_f, r0_b, lg_f, lg_b, ret_norm_g, want_out):
    y_f, r_f = _ret_scan(z, rope, r0_f, lg_f, False)
    out, r_b = _ret_scan(z, rope, r0_b, lg_b, True, y_f, ret_norm_g)
    return (out if want_out else None), r_f, r_b


def kernel(x, c, ctx, c_ctx, mod_w, mod_b, norm1_g, norm2_g, w_in, tshift_mu, w0, w_up, a0, a_up, g_up,
           k_k, k_a, r_k, lnx_g, lnx_b, conv_w, ret_norm_g, w_out, ffn_up, ffn_conv, ffn_down,
           final_norm_g):
    bsz, n, dm = x.shape
    depth = mod_w.shape[0]
    rwkv_w = k_k.shape[1]
    rwkv_cols = tshift_mu.shape[1]
    conv_cols = 3 * conv_w.shape[1]
    ret_heads = ret_norm_g.shape[1] // RET_HEAD_DIM
    npairs = rwkv_w // (2 * RWKV_HEAD_DIM)

    rows = n // GRID_W
    row = jnp.repeat(jnp.arange(rows, dtype=F32), GRID_W)
    col = jnp.tile(jnp.arange(GRID_W, dtype=F32), rows)
    pairs = RET_HEAD_DIM // 4
    freqs = 1.0 / (ROPE_BASE ** (jnp.arange(pairs, dtype=F32) / pairs))
    ang = jnp.concatenate([row[:, None] * freqs, col[:, None] * freqs], axis=-1)
    cos, sin = jnp.cos(ang), jnp.sin(ang)
    rope = (jnp.concatenate([cos, cos], axis=-1), jnp.concatenate([-sin, sin], axis=-1))
    hidx = jnp.arange(ret_heads, dtype=F32)
    lg_f = jnp.log1p(-jnp.exp2(-5.0 - hidx))
    lg_b = jnp.log1p(-jnp.exp2(-5.5 - hidx))
    s_zero = jnp.zeros((bsz, npairs, 2 * RWKV_HEAD_DIM, 2 * RWKV_HEAD_DIM), F32)
    r_zero = jnp.zeros((bsz, ret_heads, RET_HEAD_DIM, RET_HEAD_DIM), F32)

    xl = x.astype(F32)
    xc = ctx.astype(F32)
    silu_all = jnp.concatenate([jax.nn.silu(c.astype(F32)), jax.nn.silu(c_ctx.astype(F32))[None]], axis=0)

    for l in range(depth):
        last = l == depth - 1
        mod = _mm(silu_all, mod_w[l]) + mod_b[l]
        ml = [m[:, None, :] for m in jnp.split(mod[:bsz], 6, axis=-1)]
        mc = jnp.split(mod[bsz], 6, axis=-1)
        rwkv_p = (tshift_mu[l], w0[l], w_up[l], a0[l], a_up[l], g_up[l], k_k[l], k_a[l],
                  r_k[l], lnx_g[l], lnx_b[l])
        w_in_l = w_in[l].astype(BF16)
        w_out_l = w_out[l].astype(BF16)
        up_l = ffn_up[l].astype(BF16)
        down_l = ffn_down[l].astype(BF16)

        bc = lambda m: jnp.broadcast_to(m, (bsz, 1, dm))
        s1, s2 = rwkv_cols, rwkv_cols + conv_cols
        w_groups = (w_in_l[:, :s1], w_in_l[:, s1:s2], w_in_l[:, s2:])
        gm_c, sh_c = bc(norm1_g[l] * (1.0 + mc[1])), bc(mc[0])
        gm_l, sh_l = norm1_g[l] * (1.0 + ml[1]), ml[0]
        zc_rw, zc_cv, zc_rt = [_mm_norm(xc, gm_c, sh_c, wg) for wg in w_groups]
        zl_rw, zl_cv, zl_rt = [_mm_norm(xl, gm_l, sh_l, wg) for wg in w_groups]

        out_c_rw, s_f, s_b = _rwkv_mix(zc_rw, rwkv_p, s_zero, s_zero, not last)
        out_l_rw, _, _ = _rwkv_mix(zl_rw, rwkv_p, s_f, s_b, True)
        out_c_rt, r_f, r_b = _ret_mix(zc_rt, None, r_zero, r_zero, lg_f, lg_b, ret_norm_g[l], not last)
        out_l_rt, _, _ = _ret_mix(zl_rt, rope, r_f, r_b, lg_f, lg_b, ret_norm_g[l], True)

        xl = _mix_out(out_l_rw, zl_cv, out_l_rt, xl, ml[2], w_out_l, conv_w[l])
        xl = _ffn(xl, norm2_g[l] * (1.0 + ml[4]), ml[3], ml[5], up_l, ffn_conv[l], down_l,
                  final_norm_g if last else None)
        if not last:
            xc = _mix_out(out_c_rw, zc_cv, out_c_rt, xc, bc(mc[2]), w_out_l, conv_w[l])
            xc = _ffn(xc, bc(norm2_g[l] * (1.0 + mc[4])), bc(mc[3]), bc(mc[5]), up_l, ffn_conv[l], down_l)

    return xl.astype(x.dtype)
```

```python
import functools

import jax
import jax.numpy as jnp
from jax import lax
from jax.experimental import pallas as pl
from jax.experimental.pallas import tpu as pltpu

F32 = jnp.float32
BF16 = jnp.bfloat16

GRID_W = 64
RWKV_HEAD_DIM = 64
DECAY_LORA = 64
AAA_LORA = 64
GATE_LORA = 128
RET_HEAD_DIM = 128
RET_CHUNK = 128
RET_CHUNKS_PER_STEP = (4, 2)
WKV_CHUNK = 64
WKV_CHUNKS_PER_STEP = 4
ROPE_BASE = 10000.0
NORM_EPS = 1e-6
RWKV_GN_EPS = 64e-5

VMEM_LIMIT_BYTES = 48 * 1024 * 1024
VMEM_LIMIT_IN_PROJ_BYTES = 56 * 1024 * 1024

_NT = (((1,), (1,)), ((), ()))
_TN = (((0,), (0,)), ((), ()))
_NN = (((1,), (0,)), ((), ()))


def _mm_kernel(a_ref, b_ref, o_ref):
    o_ref[...] = jnp.dot(a_ref[...].astype(BF16), b_ref[...],
                         preferred_element_type=F32).astype(o_ref.dtype)


def _pick_tile(n, candidates):
    for c in candidates:
        if n % c == 0:
            return c
    return n


def _mm(a, b, out_dtype=F32):
    m, k = a.shape
    _, n = b.shape
    b = b.astype(BF16)
    mp = -(-m // 8) * 8
    if mp != m:
        a = jnp.pad(a, ((0, mp - m), (0, 0)))
    tm = _pick_tile(mp, (512, 256, 128, 64, 32, 16, 8))
    tn = _pick_tile(n, (512, 384, 256, 128))
    out = pl.pallas_call(
        _mm_kernel,
        grid=(mp // tm, n // tn),
        in_specs=[pl.BlockSpec((tm, k), lambda i, j: (i, 0)),
                  pl.BlockSpec((k, tn), lambda i, j: (0, j))],
        out_specs=pl.BlockSpec((tm, tn), lambda i, j: (i, j)),
        out_shape=jax.ShapeDtypeStruct((mp, n), out_dtype),
        compiler_params=pltpu.CompilerParams(
            dimension_semantics=("parallel", "arbitrary"),
            vmem_limit_bytes=VMEM_LIMIT_BYTES),
        name="mm",
    )(a, b)
    return out[:m] if mp != m else out


def _mm3(a, b, out_dtype=F32):
    bsz, t, k = a.shape
    return _mm(a.reshape(bsz * t, k), b, out_dtype).reshape(bsz, t, b.shape[1])


def _resident(shape):
    return pl.BlockSpec(shape, lambda *_: (0,) * len(shape), pipeline_mode=pl.Buffered(1))


def _mm_norm_kernel(x_ref, gm_ref, sh_ref, *refs):
    n = len(refs) // 2
    xb = x_ref[0]
    ms = jnp.mean(xb * xb, axis=-1, keepdims=True)
    h = (xb * lax.rsqrt(ms + NORM_EPS) * gm_ref[0] + sh_ref[0]).astype(BF16)
    for w_ref, o_ref in zip(refs[:n], refs[n:]):
        o_ref[0] = jnp.dot(h, w_ref[...], preferred_element_type=F32)


def _mm_norm(x, gmod, shift, ws):
    bsz, t, dm = x.shape
    tm = _pick_tile(t, (256, 128))
    vec_spec = pl.BlockSpec((1, 1, dm), lambda b, i: (b, 0, 0))
    return pl.pallas_call(
        _mm_norm_kernel,
        grid=(bsz, t // tm),
        in_specs=[pl.BlockSpec((1, tm, dm), lambda b, i: (b, i, 0)), vec_spec, vec_spec]
        + [_resident(w.shape) for w in ws],
        out_specs=[pl.BlockSpec((1, tm, w.shape[1]), lambda b, i: (b, i, 0)) for w in ws],
        out_shape=[jax.ShapeDtypeStruct((bsz, t, w.shape[1]), F32) for w in ws],
        compiler_params=pltpu.CompilerParams(
            dimension_semantics=("parallel", "parallel"),
            vmem_limit_bytes=VMEM_LIMIT_IN_PROJ_BYTES),
        name="mm_norm",
    )(x, gmod, shift, *ws)


def _mix_out_kernel(rw_ref, cv_ref, cvp_ref, cvn_ref, rt_ref, x_ref, gate_ref, cw_ref, w_ref, o_ref, *, tm):
    i = pl.program_id(1)
    cw = cv_ref.shape[-1] // 3
    zc = cv_ref[0]
    gch = zc[:, cw:2 * cw] * zc[:, 2 * cw:]
    before = jnp.where(i == 0, 0.0, cvp_ref[0, 7:8, cw:2 * cw] * cvp_ref[0, 7:8, 2 * cw:])
    after = jnp.where(i == pl.num_programs(1) - 1, 0.0,
                      cvn_ref[0, 0:1, cw:2 * cw] * cvn_ref[0, 0:1, 2 * cw:])
    trow = lax.broadcasted_iota(jnp.int32, (tm, 1), 0)
    prev = jnp.where(trow == 0, before, pltpu.roll(gch, 1, axis=0))
    nxt = jnp.where(trow == tm - 1, after, pltpu.roll(gch, tm - 1, axis=0))
    conv = zc[:, :cw] * (prev * cw_ref[0:1] + gch * cw_ref[1:2] + nxt * cw_ref[2:3])
    mix = jnp.concatenate([rw_ref[0].astype(BF16), conv.astype(BF16), rt_ref[0].astype(BF16)], axis=1)
    o_ref[0] = x_ref[0] + gate_ref[0] * jnp.dot(mix, w_ref[...], preferred_element_type=F32)


def _mix_out(out_rw, z_cv, out_rt, x, gate, w_out, conv_w):
    bsz, t, dm = x.shape
    rw, cvw, rtw = out_rw.shape[-1], z_cv.shape[-1], out_rt.shape[-1]
    tm = _pick_tile(t, (512, 256, 128))
    per = tm // 8
    nhb = t // 8
    return pl.pallas_call(
        functools.partial(_mix_out_kernel, tm=tm),
        grid=(bsz, t // tm),
        in_specs=[
            pl.BlockSpec((1, tm, rw), lambda b, i: (b, i, 0)),
            pl.BlockSpec((1, tm, cvw), lambda b, i: (b, i, 0)),
            pl.BlockSpec((1, 8, cvw), lambda b, i: (b, jnp.maximum(i * per - 1, 0), 0)),
            pl.BlockSpec((1, 8, cvw), lambda b, i: (b, jnp.minimum((i + 1) * per, nhb - 1), 0)),
            pl.BlockSpec((1, tm, rtw), lambda b, i: (b, i, 0)),
            pl.BlockSpec((1, tm, dm), lambda b, i: (b, i, 0)),
            pl.BlockSpec((1, 1, dm), lambda b, i: (b, 0, 0)),
            _resident((3, cvw // 3)),
            _resident((rw + cvw // 3 + rtw, dm)),
        ],
        out_specs=pl.BlockSpec((1, tm, dm), lambda b, i: (b, i, 0)),
        out_shape=jax.ShapeDtypeStruct((bsz, t, dm), F32),
        compiler_params=pltpu.CompilerParams(
            dimension_semantics=("parallel", "parallel"),
            vmem_limit_bytes=VMEM_LIMIT_BYTES),
        name="mix_out",
    )(out_rw, z_cv, z_cv, z_cv, out_rt, x, gate, conv_w.T, w_out)


FFN_HALO = 16

def _ffn_kernel(*refs, tm, final):
    if final:
        (x_ref, xp_ref, xn_ref, gm_ref, sh_ref, gate_ref, upg_ref, upv_ref, cwg_ref, cwv_ref, dn_ref,
         fg_ref, o_ref, h_scr, acc_scr) = refs
    else:
        (x_ref, xp_ref, xn_ref, gm_ref, sh_ref, gate_ref, upg_ref, upv_ref, cwg_ref, cwv_ref, dn_ref,
         o_ref, h_scr, acc_scr) = refs
    i = pl.program_id(1)
    j = pl.program_id(2)
    halo = FFN_HALO
    rows = tm + 2 * halo

    @pl.when(j == 0)
    def _():
        def modnorm(xb):
            ms = jnp.mean(xb * xb, axis=-1, keepdims=True)
            return xb * lax.rsqrt(ms + NORM_EPS) * gm_ref[0] + sh_ref[0]

        hp = jnp.where(i == 0, 0.0, modnorm(xp_ref[0]))
        hn = jnp.where(i == pl.num_programs(1) - 1, 0.0, modnorm(xn_ref[0]))
        h_scr[0:halo] = hp.astype(BF16)
        h_scr[halo:halo + tm] = modnorm(x_ref[0]).astype(BF16)
        h_scr[halo + tm:rows] = hn.astype(BF16)
        acc_scr[...] = jnp.zeros_like(acc_scr)

    h = h_scr[...]

    def conv_half(up_ref, cw_ref):
        pm = jnp.dot(h, up_ref[...], preferred_element_type=F32)
        cw = cw_ref[...]
        u = (pltpu.roll(pm, 1, axis=0) * cw[0:1] + pm * cw[1:2]
             + pltpu.roll(pm, rows - 1, axis=0) * cw[2:3])
        return u[halo:halo + tm]

    gt = conv_half(upg_ref, cwg_ref)
    vl = conv_half(upv_ref, cwv_ref)
    act = gt * jax.nn.sigmoid(gt) * vl
    acc_scr[...] += jnp.dot(act.astype(BF16), dn_ref[...], preferred_element_type=F32)

    @pl.when(j == pl.num_programs(2) - 1)
    def _():
        out = x_ref[0] + gate_ref[0] * acc_scr[...]
        if final:
            ms = jnp.mean(out * out, axis=-1, keepdims=True)
            out = out * lax.rsqrt(ms + NORM_EPS) * fg_ref[...]
        o_ref[0] = out


def _ffn(x, gmod, shift, gate, up, cw, down, final_g=None):
    bsz, t, dm = x.shape
    f = down.shape[0]
    tm = _pick_tile(t, (512, 256, 128))
    tf = _pick_tile(f, (512, 256, 128))
    halo = FFN_HALO
    nf = f // tf
    per = tm // halo
    nhb = t // halo
    cwt = cw.T
    final = final_g is not None
    vec_spec = pl.BlockSpec((1, 1, dm), lambda b, i, j: (b, 0, 0))
    in_specs = [
        pl.BlockSpec((1, tm, dm), lambda b, i, j: (b, i, 0)),
        pl.BlockSpec((1, halo, dm), lambda b, i, j: (b, jnp.maximum(i * per - 1, 0), 0)),
        pl.BlockSpec((1, halo, dm), lambda b, i, j: (b, jnp.minimum((i + 1) * per, nhb - 1), 0)),
        vec_spec, vec_spec, vec_spec,
        pl.BlockSpec((dm, tf), lambda b, i, j: (0, j)),
        pl.BlockSpec((dm, tf), lambda b, i, j: (0, nf + j)),
        pl.BlockSpec((3, tf), lambda b, i, j: (0, j)),
        pl.BlockSpec((3, tf), lambda b, i, j: (0, nf + j)),
        pl.BlockSpec((tf, dm), lambda b, i, j: (j, 0)),
    ]
    args = [x, x, x, gmod, shift, gate, up, up, cwt, cwt, down]
    if final:
        in_specs.append(pl.BlockSpec((1, dm), lambda b, i, j: (0, 0)))
        args.append(final_g.reshape(1, dm))
    return pl.pallas_call(
        functools.partial(_ffn_kernel, tm=tm, final=final),
        grid=(bsz, t // tm, nf),
        in_specs=in_specs,
        out_specs=pl.BlockSpec((1, tm, dm), lambda b, i, j: (b, i, 0)),
        out_shape=jax.ShapeDtypeStruct((bsz, t, dm), F32),
        scratch_shapes=[pltpu.VMEM((tm + 2 * halo, dm), BF16), pltpu.VMEM((tm, dm), F32)],
        compiler_params=pltpu.CompilerParams(
            dimension_semantics=("parallel", "parallel", "arbitrary"),
            vmem_limit_bytes=VMEM_LIMIT_BYTES),
        name="ffn",
    )(*args)


def _split_bf16(x):
    hi = x.astype(BF16)
    lo = (x - hi.astype(F32)).astype(BF16)
    return hi, lo


def _dot1(a, b, dn=_NN):
    return lax.dot_general(a.astype(BF16), b.astype(BF16), dn, preferred_element_type=F32)


def _wkv_kernel(*refs, reverse, npairs, chunk, nsub):
    if reverse:
        (z_ref, zp_ref, zn_ref, mu_ref, kk_ref, ka_ref, w0_ref, a0_ref, lora_ref, s0_ref,
         yf_ref, a0f_ref, gup_ref, rk_ref, lng_ref, lnb_ref, y_ref, sT_ref, s_scr) = refs
    else:
        (z_ref, zp_ref, zn_ref, mu_ref, kk_ref, ka_ref, w0_ref, a0_ref, lora_ref, s0_ref,
         y_ref, sT_ref, s_scr) = refs
    c = chunk
    cb = nsub * c
    w2 = 2 * RWKV_HEAD_DIM
    rw = npairs * w2
    i = pl.program_id(1)
    nc = pl.num_programs(1)
    ci = nc - 1 - i if reverse else i

    @pl.when(i == 0)
    def _():
        s_scr[...] = s0_ref[0]

    row = lax.broadcasted_iota(jnp.int32, (cb, cb), 0)
    col = lax.broadcasted_iota(jnp.int32, (cb, cb), 1)
    same_chunk = row // c == col // c
    cum = (same_chunk & (row <= col if reverse else row >= col)).astype(BF16)
    last = 0 if reverse else c - 1
    first_head = lax.broadcasted_iota(jnp.int32, (c, w2), 1) < RWKV_HEAD_DIM
    first_head2 = jnp.concatenate([first_head, first_head], axis=1)
    brow = lax.broadcasted_iota(jnp.int32, (w2, w2), 0) < RWKV_HEAD_DIM
    bcol = lax.broadcasted_iota(jnp.int32, (w2, w2), 1) < RWKV_HEAD_DIM
    same_head = brow == bcol
    head_ones = same_head.astype(BF16)
    trow = lax.broadcasted_iota(jnp.int32, (c, 2 * c), 0)
    tcol = lax.broadcasted_iota(jnp.int32, (c, 2 * c), 1)
    tcol = jnp.where(tcol < c, tcol, tcol - c)
    eye_pair = jnp.where(trow == tcol, 1.0, 0.0)
    if reverse:
        strict, incl = trow < tcol, trow <= tcol
    else:
        strict, incl = trow > tcol, trow >= tcol
    pairs = range(npairs)
    sls = [slice(q * w2, (q + 1) * w2) for q in pairs]

    def head_sum(u):
        hi, lo = _split_bf16(u)
        return jnp.concatenate(
            [jnp.dot(hi[:, sl], head_ones, preferred_element_type=F32)
             + jnp.dot(lo[:, sl], head_ones, preferred_element_type=F32) for sl in sls], axis=1)

    z = z_ref[0]
    trow = lax.broadcasted_iota(jnp.int32, (cb, 1), 0)
    z_before = jnp.where(ci == 0, 0.0, zp_ref[0, 7:8, :])
    z_after = jnp.where(ci == nc - 1, 0.0, zn_ref[0, 0:1, :])
    zprev = jnp.where(trow == 0, z_before, pltpu.roll(z, 1, axis=0))
    znext = jnp.where(trow == cb - 1, z_after, pltpu.roll(z, cb - 1, axis=0))
    z = z + (0.5 * (zprev + znext) - z) * mu_ref[...]
    r = z[:, :rw]
    k = z[:, rw:2 * rw]
    v_all = z[:, 2 * rw:3 * rw]
    wa = z[:, 3 * rw:3 * rw + w2]
    gd = z[:, 3 * rw + w2:]

    kkr = k * kk_ref[...]
    kk = kkr / jnp.maximum(jnp.sqrt(head_sum(kkr * kkr)), 1e-12)
    decay_lane = lax.broadcasted_iota(jnp.int32, (cb, w2), 1) < DECAY_LORA
    lora = jnp.dot(jnp.where(decay_lane, jnp.tanh(wa), wa).astype(BF16), lora_ref[...],
                   preferred_element_type=F32)
    xw = w0_ref[...] + lora[:, :rw]
    w_log = -(jnp.maximum(-xw, 0.0) + jnp.log1p(jnp.exp(-jnp.abs(xw)))) - 0.5
    lw = -jnp.exp(w_log)
    rate = jax.nn.sigmoid(a0_ref[...] + lora[:, rw:2 * rw])
    key = k * (1.0 + (rate - 1.0) * ka_ref[...])

    lw_hi, lw_lo = _split_bf16(lw)
    cl = (jnp.dot(cum, lw_hi, preferred_element_type=F32)
          + jnp.dot(cum, lw_lo, preferred_element_type=F32))
    p = jnp.exp(cl)
    pinv = jnp.exp(-cl)
    at_all = -kk * jnp.exp(cl - lw)
    bt_all = kk * rate * pinv
    kt_all = key * pinv
    rt_all = r * p

    def stack(u):
        m = first_head if u.shape[1] == w2 else first_head2
        return jnp.concatenate([jnp.where(m, u, 0.0), jnp.where(m, 0.0, u)], axis=0)

    def pdot(mat, u):
        return _dot1(mat, stack(u))

    units = [(j, q) for j in range(nsub) for q in pairs]
    cut = lambda u: [u[j * c:(j + 1) * c, sls[q]] for j, q in units]
    at, bt, kt, rt, v = cut(at_all), cut(bt_all), cut(kt_all), cut(rt_all), cut(v_all)
    us = range(len(units))

    a_ab, a_ak, m_rb, m_rk = [], [], [], []
    for u in us:
        lhs = jnp.concatenate([at[u], rt[u]], axis=0)
        mb = _dot1(lhs, stack(bt[u]), _NT)
        mk = _dot1(lhs, stack(kt[u]), _NT)
        a_ab.append(jnp.where(strict, mb[:c], 0.0))
        m_rb.append(jnp.where(incl, mb[c:], 0.0))
        a_ak.append(jnp.where(strict, mk[:c], 0.0))
        m_rk.append(jnp.where(incl, mk[c:], 0.0))

    ak = a_ab
    tinv = [eye_pair + a_ab[u] for u in us]
    for _ in range(max(1, (c - 1).bit_length()) - 1):
        ak = [pdot(ak[u], ak[u]) for u in us]
        tinv = [tinv[u] + pdot(ak[u], tinv[u]) for u in us]
    x = [pdot(tinv[u], jnp.concatenate([at[u], pdot(a_ak[u], v[u])], axis=1)) for u in us]

    rbx = [pdot(m_rb[u], x[u]) for u in us]
    rbar = [rt[u] + rbx[u][:, :w2] for u in us]
    y0 = [rbx[u][:, w2:] + pdot(m_rk[u], v[u]) for u in us]
    g = [jnp.where(same_head, _dot1(x[u][:, :w2], bt[u], _TN), 0.0) for u in us]
    hh = [jnp.where(same_head,
                    _dot1(jnp.concatenate([x[u][:, w2:], v[u]], axis=0),
                          jnp.concatenate([bt[u], kt[u]], axis=0), _TN),
                    0.0) for u in us]

    ys = [None] * len(units)
    state = [s_scr[q] for q in pairs]
    for j in (reversed(range(nsub)) if reverse else range(nsub)):
        for q in pairs:
            u = j * npairs + q
            s0 = state[q]
            ys[u] = _dot1(rbar[u], s0, _NT) + y0[u]
            state[q] = (s0 + _dot1(s0, g[u]) + hh[u]) * p[j * c + last:j * c + last + 1, sls[q]]
    for q in pairs:
        s_scr[q] = state[q]
    y = jnp.concatenate([jnp.concatenate(ys[j * npairs:(j + 1) * npairs], axis=1)
                         for j in range(nsub)], axis=0)

    if reverse:
        y = yf_ref[0] + y
        inv_n = 1.0 / RWKV_HEAD_DIM
        dev = y - head_sum(y) * inv_n
        yn = dev * lax.rsqrt(head_sum(dev * dev) * inv_n + RWKV_GN_EPS)
        rate_f = jax.nn.sigmoid(a0f_ref[...] + lora[:, 2 * rw:])
        k_mid = 0.5 * (k * (1.0 + (rate_f - 1.0) * ka_ref[...]) + key)
        bonus = head_sum(r * k_mid * rk_ref[...]) * v_all
        gate = jnp.dot(jax.nn.sigmoid(gd).astype(BF16), gup_ref[...], preferred_element_type=F32)
        y = (yn * lng_ref[...] + lnb_ref[...] + bonus) * gate
    y_ref[0] = y

    @pl.when(i == pl.num_programs(1) - 1)
    def _():
        sT_ref[0] = s_scr[...]


def _wkv_scan(z, p, d, s0, y_fwd=None):
    mu, w0, w_up, a0, a_up, g_up, k_k, k_a, r_k, lnx_g, lnx_b = p
    bsz, t, zw = z.shape
    w = k_k.shape[0]
    w2 = 2 * RWKV_HEAD_DIM
    npairs = w // w2
    c = WKV_CHUNK
    assert c == RWKV_HEAD_DIM and t % c == 0 and w % w2 == 0
    nsub = WKV_CHUNKS_PER_STEP if (t // c) % WKV_CHUNKS_PER_STEP == 0 else 1
    cb = nsub * c
    nc = t // cb
    reverse = d == 1
    per = cb // 8
    nhb = t // 8
    pos = (lambda i: nc - 1 - i) if reverse else (lambda i: i)
    tok = lambda bi, i: (bi, pos(i), 0)
    row1 = lambda u: u.reshape(1, -1)
    zero = jnp.zeros((DECAY_LORA, w), F32)
    blocks = [jnp.concatenate([w_up[d], zero], axis=0), jnp.concatenate([zero, a_up[d]], axis=0)]
    if reverse:
        blocks.append(jnp.concatenate([zero, a_up[0]], axis=0))
    lora_w = jnp.concatenate(blocks, axis=1).astype(BF16)
    full = lambda arr: pl.BlockSpec(arr.shape, lambda bi, i: (0, 0))
    st_spec = pl.BlockSpec((1, npairs, w2, w2), lambda bi, i: (bi, 0, 0, 0))
    y_spec = pl.BlockSpec((1, cb, w), tok)
    args = [z, z, z, row1(mu), row1(k_k), row1(k_a), row1(w0[d]), row1(a0[d]), lora_w, s0]
    in_specs = [
        pl.BlockSpec((1, cb, zw), tok),
        pl.BlockSpec((1, 8, zw), lambda bi, i: (bi, jnp.maximum(pos(i) * per - 1, 0), 0)),
        pl.BlockSpec((1, 8, zw), lambda bi, i: (bi, jnp.minimum((pos(i) + 1) * per, nhb - 1), 0)),
    ] + [full(a) for a in args[3:9]] + [st_spec]
    if reverse:
        extra = [row1(a0[0]), g_up.astype(BF16), row1(r_k), row1(lnx_g), row1(lnx_b)]
        args += [y_fwd] + extra
        in_specs += [y_spec] + [full(a) for a in extra]
    return pl.pallas_call(
        functools.partial(_wkv_kernel, reverse=reverse, npairs=npairs, chunk=c, nsub=nsub),
        grid=(bsz, nc),
        in_specs=in_specs,
        out_specs=[y_spec, st_spec],
        out_shape=[jax.ShapeDtypeStruct((bsz, t, w), F32),
                   jax.ShapeDtypeStruct((bsz, npairs, w2, w2), F32)],
        scratch_shapes=[pltpu.VMEM((npairs, w2, w2), F32)],
        compiler_params=pltpu.CompilerParams(
            dimension_semantics=("parallel", "arbitrary"),
            vmem_limit_bytes=VMEM_LIMIT_BYTES),
        name="wkv_bwd" if reverse else "wkv_fwd",
    )(*args)


def _ret_kernel(*refs, nheads, reverse, use_rope, nsub):
    refs = list(refs)
    z_ref = refs.pop(0)
    cos_ref, sin_ref = (refs.pop(0), refs.pop(0)) if use_rope else (None, None)
    din_ref, xi_ref, zeta_ref, g_ref, r0_ref = refs[:5]
    refs = refs[5:]
    yf_ref, ng_ref = (refs.pop(0), refs.pop(0)) if reverse else (None, None)
    y_ref, rT_ref, r_scr = refs
    d = RET_HEAD_DIM
    w = nheads * d
    i = pl.program_id(1)

    @pl.when(i == 0)
    def _():
        r_scr[...] = r0_ref[0]

    c = RET_CHUNK
    units = [(j, h) for j in range(nsub) for h in range(nheads)]
    us = range(len(units))
    rows = lambda j: slice(j * c, (j + 1) * c)
    cols = lambda g, h: slice(g * w + h * d, g * w + (h + 1) * d)

    def load(g, j, h, is_qk):
        u = z_ref[0, rows(j), cols(g, h)]
        if g == 0:
            u = u * (d ** -0.5)
        if is_qk and use_rope:
            u = u * cos_ref[rows(j), :] + pltpu.roll(u, d // 2, axis=1) * sin_ref[rows(j), :]
        return u

    q = [load(0, j, h, True) for j, h in units]
    k = [load(1, j, h, True) for j, h in units]
    v = [load(2, j, h, False) for j, h in units]
    s = [_dot1(q[u], k[u], _NT) * din_ref[units[u][1]] for u in us]
    sv = [_dot1(s[u], v[u]) for u in us]
    kv = [_dot1(k[u] * zeta_ref[units[u][1]], v[u], _TN) for u in us]

    ys = [None] * len(units)
    state = [r_scr[h] for h in range(nheads)]
    for j in (reversed(range(nsub)) if reverse else range(nsub)):
        for h in range(nheads):
            u = j * nheads + h
            ys[u] = sv[u] + _dot1(q[u], state[h]) * xi_ref[h]
            state[h] = state[h] * g_ref[h] + kv[u]
    for h in range(nheads):
        r_scr[h] = state[h]

    for u, (j, h) in enumerate(units):
        y = ys[u]
        if reverse:
            y = yf_ref[0, rows(j), cols(0, h)] + y
            dev = y - jnp.mean(y, axis=-1, keepdims=True)
            yn = dev * lax.rsqrt(jnp.mean(dev * dev, axis=-1, keepdims=True) + NORM_EPS)
            gt = z_ref[0, rows(j), cols(3, h)]
            y = gt * jax.nn.sigmoid(gt) * (yn * ng_ref[:, cols(0, h)])
        y_ref[0, rows(j), cols(0, h)] = y

    @pl.when(i == pl.num_programs(1) - 1)
    def _():
        rT_ref[0] = r_scr[...]


def _ret_tables(log_gamma, reverse):
    c, d = RET_CHUNK, RET_HEAD_DIM
    idx = jnp.arange(c, dtype=F32)
    lg = log_gamma[:, None, None]
    if reverse:
        diff = idx[None, :] - idx[:, None]
        keep = diff > 0
        xi = jnp.exp(log_gamma[:, None] * (c - idx))
        zeta = jnp.exp(log_gamma[:, None] * idx)
    else:
        diff = idx[:, None] - idx[None, :]
        keep = diff >= 0
        xi = jnp.exp(log_gamma[:, None] * (idx + 1.0))
        zeta = jnp.exp(log_gamma[:, None] * (c - 1.0 - idx))
    din = jnp.where(keep, jnp.exp(lg * jnp.maximum(diff, 0.0)), 0.0)
    nh = log_gamma.shape[0]
    xi = jnp.broadcast_to(xi[:, :, None], (nh, c, d))
    zeta = jnp.broadcast_to(zeta[:, :, None], (nh, c, d))
    g = jnp.broadcast_to(jnp.exp(log_gamma * c)[:, None, None], (nh, d, d))
    return din, xi, zeta, g


def _ret_scan(z, rope, r0, log_gamma, reverse, y_fwd=None, norm_g=None):
    bsz, t, zw = z.shape
    d = RET_HEAD_DIM
    nh = zw // (4 * d)
    w = nh * d
    c = RET_CHUNK
    assert t % c == 0
    per_step = RET_CHUNKS_PER_STEP[1 if reverse else 0]
    nsub = per_step if (t // c) % per_step == 0 else 1
    cb = nsub * c
    nc = t // cb
    din, xi, zeta, g = _ret_tables(log_gamma, reverse)
    pos = (lambda i: nc - 1 - i) if reverse else (lambda i: i)
    tok = lambda bi, i: (bi, pos(i), 0)
    y_spec = pl.BlockSpec((1, cb, w), tok)
    st_spec = pl.BlockSpec((1, nh, d, d), lambda bi, i: (bi, 0, 0, 0))
    tab_spec = lambda n: pl.BlockSpec((nh, n, d), lambda bi, i: (0, 0, 0))
    args = [z]
    in_specs = [pl.BlockSpec((1, cb, zw), tok)]
    if rope is not None:
        args += list(rope)
        in_specs += [pl.BlockSpec((cb, d), lambda bi, i: (pos(i), 0))] * 2
    args += [din, xi, zeta, g, r0]
    in_specs += [tab_spec(c), tab_spec(c), tab_spec(c), tab_spec(d), st_spec]
    if reverse:
        args += [y_fwd, norm_g.reshape(1, w)]
        in_specs += [y_spec, pl.BlockSpec((1, w), lambda bi, i: (0, 0))]
    return pl.pallas_call(
        functools.partial(_ret_kernel, nheads=nh, reverse=reverse, use_rope=rope is not None, nsub=nsub),
        grid=(bsz, nc),
        in_specs=in_specs,
        out_specs=[y_spec, st_spec],
        out_shape=[jax.ShapeDtypeStruct((bsz, t, w), F32),
                   jax.ShapeDtypeStruct((bsz, nh, d, d), F32)],
        scratch_shapes=[pltpu.VMEM((nh, d, d), F32)],
        compiler_params=pltpu.CompilerParams(
            dimension_semantics=("parallel", "arbitrary"),
            vmem_limit_bytes=VMEM_LIMIT_BYTES),
        name="ret_bwd" if reverse else "ret_fwd",
    )(*args)


def _rwkv_mix(z, p, s0_f, s0_b, want_out):
    y_f, s_f = _wkv_scan(z, p, 0, s0_f)
    out, s_b = _wkv_scan(z, p, 1, s0_b, y_f)
    return (out if want_out else None), s_f, s_b


def _ret_mix(z, rope, r0_f, r0_b, lg_f, lg_b, ret_norm_g, want_out):
    y_f, r_f = _ret_scan(z, rope, r0_f, lg_f, False)
    out, r_b = _ret_scan(z, rope, r0_b, lg_b, True, y_f, ret_norm_g)
    return (out if want_out else None), r_f, r_b


def kernel(x, c, ctx, c_ctx, mod_w, mod_b, norm1_g, norm2_g, w_in, tshift_mu, w0, w_up, a0, a_up, g_up,
           k_k, k_a, r_k, lnx_g, lnx_b, conv_w, ret_norm_g, w_out, ffn_up, ffn_conv, ffn_down,
           final_norm_g):
    bsz, n, dm = x.shape
    depth = mod_w.shape[0]
    rwkv_w = k_k.shape[1]
    rwkv_cols = tshift_mu.shape[1]
    conv_cols = 3 * conv_w.shape[1]
    ret_heads = ret_norm_g.shape[1] // RET_HEAD_DIM
    npairs = rwkv_w // (2 * RWKV_HEAD_DIM)

    rows = n // GRID_W
    row = jnp.repeat(jnp.arange(rows, dtype=F32), GRID_W)
    col = jnp.tile(jnp.arange(GRID_W, dtype=F32), rows)
    pairs = RET_HEAD_DIM // 4
    freqs = 1.0 / (ROPE_BASE ** (jnp.arange(pairs, dtype=F32) / pairs))
    ang = jnp.concatenate([row[:, None] * freqs, col[:, None] * freqs], axis=-1)
    cos, sin = jnp.cos(ang), jnp.sin(ang)
    rope = (jnp.concatenate([cos, cos], axis=-1), jnp.concatenate([-sin, sin], axis=-1))
    hidx = jnp.arange(ret_heads, dtype=F32)
    lg_f = jnp.log1p(-jnp.exp2(-5.0 - hidx))
    lg_b = jnp.log1p(-jnp.exp2(-5.5 - hidx))
    s_zero = jnp.zeros((bsz, npairs, 2 * RWKV_HEAD_DIM, 2 * RWKV_HEAD_DIM), F32)
    r_zero = jnp.zeros((bsz, ret_heads, RET_HEAD_DIM, RET_HEAD_DIM), F32)

    xl = x.astype(F32)
    xc = ctx.astype(F32)
    silu_all = jnp.concatenate([jax.nn.silu(c.astype(F32)), jax.nn.silu(c_ctx.astype(F32))[None]], axis=0)

    for l in range(depth):
        last = l == depth - 1
        mod = _mm(silu_all, mod_w[l]) + mod_b[l]
        ml = [m[:, None, :] for m in jnp.split(mod[:bsz], 6, axis=-1)]
        mc = jnp.split(mod[bsz], 6, axis=-1)
        rwkv_p = (tshift_mu[l], w0[l], w_up[l], a0[l], a_up[l], g_up[l], k_k[l], k_a[l],
                  r_k[l], lnx_g[l], lnx_b[l])
        w_in_l = w_in[l].astype(BF16)
        w_out_l = w_out[l].astype(BF16)
        up_l = ffn_up[l].astype(BF16)
        down_l = ffn_down[l].astype(BF16)

        bc = lambda m: jnp.broadcast_to(m, (bsz, 1, dm))
        s1, s2 = rwkv_cols, rwkv_cols + conv_cols
        w_groups = (w_in_l[:, :s1], w_in_l[:, s1:s2], w_in_l[:, s2:])
        gm_c, sh_c = bc(norm1_g[l] * (1.0 + mc[1])), bc(mc[0])
        gm_l, sh_l = norm1_g[l] * (1.0 + ml[1]), ml[0]
        zc_rw, zc_cv, zc_rt = _mm_norm(xc, gm_c, sh_c, w_groups)
        zl_rw, zl_cv, zl_rt = _mm_norm(xl, gm_l, sh_l, w_groups)

        out_c_rw, s_f, s_b = _rwkv_mix(zc_rw, rwkv_p, s_zero, s_zero, not last)
        out_l_rw, _, _ = _rwkv_mix(zl_rw, rwkv_p, s_f, s_b, True)
        out_c_rt, r_f, r_b = _ret_mix(zc_rt, None, r_zero, r_zero, lg_f, lg_b, ret_norm_g[l], not last)
        out_l_rt, _, _ = _ret_mix(zl_rt, rope, r_f, r_b, lg_f, lg_b, ret_norm_g[l], True)

        xl = _mix_out(out_l_rw, zl_cv, out_l_rt, xl, ml[2], w_out_l, conv_w[l])
        xl = _ffn(xl, norm2_g[l] * (1.0 + ml[4]), ml[3], ml[5], up_l, ffn_conv[l], down_l,
                  final_norm_g if last else None)
        if not last:
            xc = _mix_out(out_c_rw, zc_cv, out_c_rt, xc, bc(mc[2]), w_out_l, conv_w[l])
            xc = _ffn(xc, bc(norm2_g[l] * (1.0 + mc[4])), bc(mc[3]), bc(mc[5]), up_l, ffn_conv[l], down_l)

    return xl.astype(x.dtype)
```

```python
import functools

import jax
import jax.numpy as jnp
from jax import lax
from jax.experimental import pallas as pl
from jax.experimental.pallas import tpu as pltpu

F32 = jnp.float32
BF16 = jnp.bfloat16

GRID_W = 64
RWKV_HEAD_DIM = 64
DECAY_LORA = 64
AAA_LORA = 64
GATE_LORA = 128
RET_HEAD_DIM = 128
RET_CHUNK = 128
RET_CHUNKS_PER_STEP = (4, 2)
WKV_CHUNK = 64
WKV_CHUNKS_PER_STEP = 4
ROPE_BASE = 10000.0
NORM_EPS = 1e-6
RWKV_GN_EPS = 64e-5

VMEM_LIMIT_BYTES = 48 * 1024 * 1024
VMEM_LIMIT_IN_PROJ_BYTES = 56 * 1024 * 1024

_NT = (((1,), (1,)), ((), ()))
_TN = (((0,), (0,)), ((), ()))
_NN = (((1,), (0,)), ((), ()))


def _mm_kernel(a_ref, b_ref, o_ref):
    o_ref[...] = jnp.dot(a_ref[...].astype(BF16), b_ref[...],
                         preferred_element_type=F32).astype(o_ref.dtype)


def _pick_tile(n, candidates):
    for c in candidates:
        if n % c == 0:
            return c
    return n


def _mm(a, b, out_dtype=F32):
    m, k = a.shape
    _, n = b.shape
    b = b.astype(BF16)
    mp = -(-m // 8) * 8
    if mp != m:
        a = jnp.pad(a, ((0, mp - m), (0, 0)))
    tm = _pick_tile(mp, (512, 256, 128, 64, 32, 16, 8))
    tn = _pick_tile(n, (512, 384, 256, 128))
    out = pl.pallas_call(
        _mm_kernel,
        grid=(mp // tm, n // tn),
        in_specs=[pl.BlockSpec((tm, k), lambda i, j: (i, 0)),
                  pl.BlockSpec((k, tn), lambda i, j: (0, j))],
        out_specs=pl.BlockSpec((tm, tn), lambda i, j: (i, j)),
        out_shape=jax.ShapeDtypeStruct((mp, n), out_dtype),
        compiler_params=pltpu.CompilerParams(
            dimension_semantics=("parallel", "arbitrary"),
            vmem_limit_bytes=VMEM_LIMIT_BYTES),
        name="mm",
    )(a, b)
    return out[:m] if mp != m else out


def _mm3(a, b, out_dtype=F32):
    bsz, t, k = a.shape
    return _mm(a.reshape(bsz * t, k), b, out_dtype).reshape(bsz, t, b.shape[1])


def _resident(shape):
    return pl.BlockSpec(shape, lambda *_: (0,) * len(shape), pipeline_mode=pl.Buffered(1))


def _mm_norm_kernel(x_ref, gm_ref, sh_ref, w_ref, *o_refs):
    xb = x_ref[0]
    ms = jnp.mean(xb * xb, axis=-1, keepdims=True)
    h = (xb * lax.rsqrt(ms + NORM_EPS) * gm_ref[0] + sh_ref[0]).astype(BF16)
    start = 0
    for o_ref in o_refs:
        n = o_ref.shape[-1]
        o_ref[0] = jnp.dot(h, w_ref[:, start:start + n], preferred_element_type=F32)
        start += n


def _mm_norm(x, gmod, shift, w, widths):
    bsz, t, dm = x.shape
    assert sum(widths) == w.shape[1]
    tm = _pick_tile(t, (256, 128))
    vec_spec = pl.BlockSpec((1, 1, dm), lambda b, i: (b, 0, 0))
    return pl.pallas_call(
        _mm_norm_kernel,
        grid=(bsz, t // tm),
        in_specs=[pl.BlockSpec((1, tm, dm), lambda b, i: (b, i, 0)), vec_spec, vec_spec, _resident(w.shape)],
        out_specs=[pl.BlockSpec((1, tm, n), lambda b, i: (b, i, 0)) for n in widths],
        out_shape=[jax.ShapeDtypeStruct((bsz, t, n), F32) for n in widths],
        compiler_params=pltpu.CompilerParams(
            dimension_semantics=("parallel", "parallel"),
            vmem_limit_bytes=VMEM_LIMIT_IN_PROJ_BYTES),
        name="mm_norm",
    )(x, gmod, shift, w)


def _mix_out_kernel(rw_ref, cv_ref, cvp_ref, cvn_ref, rt_ref, x_ref, gate_ref, cw_ref, w_ref, o_ref, *, tm):
    i = pl.program_id(1)
    cw = cv_ref.shape[-1] // 3
    zc = cv_ref[0]
    gch = zc[:, cw:2 * cw] * zc[:, 2 * cw:]
    before = jnp.where(i == 0, 0.0, cvp_ref[0, 7:8, cw:2 * cw] * cvp_ref[0, 7:8, 2 * cw:])
    after = jnp.where(i == pl.num_programs(1) - 1, 0.0,
                      cvn_ref[0, 0:1, cw:2 * cw] * cvn_ref[0, 0:1, 2 * cw:])
    trow = lax.broadcasted_iota(jnp.int32, (tm, 1), 0)
    prev = jnp.where(trow == 0, before, pltpu.roll(gch, 1, axis=0))
    nxt = jnp.where(trow == tm - 1, after, pltpu.roll(gch, tm - 1, axis=0))
    conv = zc[:, :cw] * (prev * cw_ref[0:1] + gch * cw_ref[1:2] + nxt * cw_ref[2:3])
    mix = jnp.concatenate([rw_ref[0].astype(BF16), conv.astype(BF16), rt_ref[0].astype(BF16)], axis=1)
    o_ref[0] = x_ref[0] + gate_ref[0] * jnp.dot(mix, w_ref[...], preferred_element_type=F32)


def _mix_out(out_rw, z_cv, out_rt, x, gate, w_out, conv_w):
    bsz, t, dm = x.shape
    rw, cvw, rtw = out_rw.shape[-1], z_cv.shape[-1], out_rt.shape[-1]
    tm = _pick_tile(t, (512, 256, 128))
    per = tm // 8
    nhb = t // 8
    return pl.pallas_call(
        functools.partial(_mix_out_kernel, tm=tm),
        grid=(bsz, t // tm),
        in_specs=[
            pl.BlockSpec((1, tm, rw), lambda b, i: (b, i, 0)),
            pl.BlockSpec((1, tm, cvw), lambda b, i: (b, i, 0)),
            pl.BlockSpec((1, 8, cvw), lambda b, i: (b, jnp.maximum(i * per - 1, 0), 0)),
            pl.BlockSpec((1, 8, cvw), lambda b, i: (b, jnp.minimum((i + 1) * per, nhb - 1), 0)),
            pl.BlockSpec((1, tm, rtw), lambda b, i: (b, i, 0)),
            pl.BlockSpec((1, tm, dm), lambda b, i: (b, i, 0)),
            pl.BlockSpec((1, 1, dm), lambda b, i: (b, 0, 0)),
            _resident((3, cvw // 3)),
            _resident((rw + cvw // 3 + rtw, dm)),
        ],
        out_specs=pl.BlockSpec((1, tm, dm), lambda b, i: (b, i, 0)),
        out_shape=jax.ShapeDtypeStruct((bsz, t, dm), F32),
        compiler_params=pltpu.CompilerParams(
            dimension_semantics=("parallel", "parallel"),
            vmem_limit_bytes=VMEM_LIMIT_BYTES),
        name="mix_out",
    )(out_rw, z_cv, z_cv, z_cv, out_rt, x, gate, conv_w.T, w_out)


FFN_HALO = 16

def _ffn_kernel(*refs, tm, final):
    if final:
        (x_ref, xp_ref, xn_ref, gm_ref, sh_ref, gate_ref, upg_ref, upv_ref, cwg_ref, cwv_ref, dn_ref,
         fg_ref, o_ref, h_scr, acc_scr) = refs
    else:
        (x_ref, xp_ref, xn_ref, gm_ref, sh_ref, gate_ref, upg_ref, upv_ref, cwg_ref, cwv_ref, dn_ref,
         o_ref, h_scr, acc_scr) = refs
    i = pl.program_id(1)
    j = pl.program_id(2)
    halo = FFN_HALO
    rows = tm + 2 * halo

    @pl.when(j == 0)
    def _():
        def modnorm(xb):
            ms = jnp.mean(xb * xb, axis=-1, keepdims=True)
            return xb * lax.rsqrt(ms + NORM_EPS) * gm_ref[0] + sh_ref[0]

        hp = jnp.where(i == 0, 0.0, modnorm(xp_ref[0]))
        hn = jnp.where(i == pl.num_programs(1) - 1, 0.0, modnorm(xn_ref[0]))
        h_scr[0:halo] = hp.astype(BF16)
        h_scr[halo:halo + tm] = modnorm(x_ref[0]).astype(BF16)
        h_scr[halo + tm:rows] = hn.astype(BF16)
        acc_scr[...] = jnp.zeros_like(acc_scr)

    h = h_scr[...]

    def conv_half(up_ref, cw_ref):
        pm = jnp.dot(h, up_ref[...], preferred_element_type=F32)
        cw = cw_ref[...]
        u = (pltpu.roll(pm, 1, axis=0) * cw[0:1] + pm * cw[1:2]
             + pltpu.roll(pm, rows - 1, axis=0) * cw[2:3])
        return u[halo:halo + tm]

    gt = conv_half(upg_ref, cwg_ref)
    vl = conv_half(upv_ref, cwv_ref)
    act = gt * jax.nn.sigmoid(gt) * vl
    acc_scr[...] += jnp.dot(act.astype(BF16), dn_ref[...], preferred_element_type=F32)

    @pl.when(j == pl.num_programs(2) - 1)
    def _():
        out = x_ref[0] + gate_ref[0] * acc_scr[...]
        if final:
            ms = jnp.mean(out * out, axis=-1, keepdims=True)
            out = out * lax.rsqrt(ms + NORM_EPS) * fg_ref[...]
        o_ref[0] = out


def _ffn(x, gmod, shift, gate, up, cw, down, final_g=None):
    bsz, t, dm = x.shape
    f = down.shape[0]
    tm = _pick_tile(t, (512, 256, 128))
    tf = _pick_tile(f, (512, 256, 128))
    halo = FFN_HALO
    nf = f // tf
    per = tm // halo
    nhb = t // halo
    cwt = cw.T
    final = final_g is not None
    vec_spec = pl.BlockSpec((1, 1, dm), lambda b, i, j: (b, 0, 0))
    in_specs = [
        pl.BlockSpec((1, tm, dm), lambda b, i, j: (b, i, 0)),
        pl.BlockSpec((1, halo, dm), lambda b, i, j: (b, jnp.maximum(i * per - 1, 0), 0)),
        pl.BlockSpec((1, halo, dm), lambda b, i, j: (b, jnp.minimum((i + 1) * per, nhb - 1), 0)),
        vec_spec, vec_spec, vec_spec,
        pl.BlockSpec((dm, tf), lambda b, i, j: (0, j)),
        pl.BlockSpec((dm, tf), lambda b, i, j: (0, nf + j)),
        pl.BlockSpec((3, tf), lambda b, i, j: (0, j)),
        pl.BlockSpec((3, tf), lambda b, i, j: (0, nf + j)),
        pl.BlockSpec((tf, dm), lambda b, i, j: (j, 0)),
    ]
    args = [x, x, x, gmod, shift, gate, up, up, cwt, cwt, down]
    if final:
        in_specs.append(pl.BlockSpec((1, dm), lambda b, i, j: (0, 0)))
        args.append(final_g.reshape(1, dm))
    return pl.pallas_call(
        functools.partial(_ffn_kernel, tm=tm, final=final),
        grid=(bsz, t // tm, nf),
        in_specs=in_specs,
        out_specs=pl.BlockSpec((1, tm, dm), lambda b, i, j: (b, i, 0)),
        out_shape=jax.ShapeDtypeStruct((bsz, t, dm), F32),
        scratch_shapes=[pltpu.VMEM((tm + 2 * halo, dm), BF16), pltpu.VMEM((tm, dm), F32)],
        compiler_params=pltpu.CompilerParams(
            dimension_semantics=("parallel", "parallel", "arbitrary"),
            vmem_limit_bytes=VMEM_LIMIT_BYTES),
        name="ffn",
    )(*args)


def _split_bf16(x):
    hi = x.astype(BF16)
    lo = (x - hi.astype(F32)).astype(BF16)
    return hi, lo


def _dot1(a, b, dn=_NN):
    return lax.dot_general(a.astype(BF16), b.astype(BF16), dn, preferred_element_type=F32)


def _wkv_kernel(*refs, reverse, npairs, chunk, nsub):
    if reverse:
        (z_ref, zp_ref, zn_ref, mu_ref, kk_ref, ka_ref, w0_ref, a0_ref, lora_ref, s0_ref,
         yf_ref, a0f_ref, gup_ref, rk_ref, lng_ref, lnb_ref, y_ref, sT_ref, s_scr) = refs
    else:
        (z_ref, zp_ref, zn_ref, mu_ref, kk_ref, ka_ref, w0_ref, a0_ref, lora_ref, s0_ref,
         y_ref, sT_ref, s_scr) = refs
    c = chunk
    cb = nsub * c
    w2 = 2 * RWKV_HEAD_DIM
    rw = npairs * w2
    i = pl.program_id(1)
    nc = pl.num_programs(1)
    ci = nc - 1 - i if reverse else i

    @pl.when(i == 0)
    def _():
        s_scr[...] = s0_ref[0]

    row = lax.broadcasted_iota(jnp.int32, (cb, cb), 0)
    col = lax.broadcasted_iota(jnp.int32, (cb, cb), 1)
    same_chunk = row // c == col // c
    cum = (same_chunk & (row <= col if reverse else row >= col)).astype(BF16)
    last = 0 if reverse else c - 1
    first_head = lax.broadcasted_iota(jnp.int32, (c, w2), 1) < RWKV_HEAD_DIM
    first_head2 = jnp.concatenate([first_head, first_head], axis=1)
    brow = lax.broadcasted_iota(jnp.int32, (w2, w2), 0) < RWKV_HEAD_DIM
    bcol = lax.broadcasted_iota(jnp.int32, (w2, w2), 1) < RWKV_HEAD_DIM
    same_head = brow == bcol
    head_ones = same_head.astype(BF16)
    trow = lax.broadcasted_iota(jnp.int32, (c, 2 * c), 0)
    tcol = lax.broadcasted_iota(jnp.int32, (c, 2 * c), 1)
    tcol = jnp.where(tcol < c, tcol, tcol - c)
    eye_pair = jnp.where(trow == tcol, 1.0, 0.0)
    if reverse:
        strict, incl = trow < tcol, trow <= tcol
    else:
        strict, incl = trow > tcol, trow >= tcol
    pairs = range(npairs)
    sls = [slice(q * w2, (q + 1) * w2) for q in pairs]

    def head_sum(u):
        hi, lo = _split_bf16(u)
        ones2 = jnp.concatenate([head_ones, head_ones], axis=0)
        return jnp.concatenate(
            [jnp.dot(jnp.concatenate([hi[:, sl], lo[:, sl]], axis=1), ones2, preferred_element_type=F32)
             for sl in sls], axis=1)

    z = z_ref[0]
    trow = lax.broadcasted_iota(jnp.int32, (cb, 1), 0)
    z_before = jnp.where(ci == 0, 0.0, zp_ref[0, 7:8, :])
    z_after = jnp.where(ci == nc - 1, 0.0, zn_ref[0, 0:1, :])
    zprev = jnp.where(trow == 0, z_before, pltpu.roll(z, 1, axis=0))
    znext = jnp.where(trow == cb - 1, z_after, pltpu.roll(z, cb - 1, axis=0))
    z = z + (0.5 * (zprev + znext) - z) * mu_ref[...]
    r = z[:, :rw]
    k = z[:, rw:2 * rw]
    v_all = z[:, 2 * rw:3 * rw]
    wa = z[:, 3 * rw:3 * rw + w2]
    gd = z[:, 3 * rw + w2:]

    kkr = k * kk_ref[...]
    kk = kkr / jnp.maximum(jnp.sqrt(head_sum(kkr * kkr)), 1e-12)
    decay_lane = lax.broadcasted_iota(jnp.int32, (cb, w2), 1) < DECAY_LORA
    lora = jnp.dot(jnp.where(decay_lane, jnp.tanh(wa), wa).astype(BF16), lora_ref[...],
                   preferred_element_type=F32)
    xw = w0_ref[...] + lora[:, :rw]
    w_log = -(jnp.maximum(-xw, 0.0) + jnp.log1p(jnp.exp(-jnp.abs(xw)))) - 0.5
    lw = -jnp.exp(w_log)
    rate = jax.nn.sigmoid(a0_ref[...] + lora[:, rw:2 * rw])
    key = k * (1.0 + (rate - 1.0) * ka_ref[...])

    lw_hi, lw_lo = _split_bf16(lw)
    cl = jnp.dot(jnp.concatenate([cum, cum], axis=1), jnp.concatenate([lw_hi, lw_lo], axis=0),
                 preferred_element_type=F32)
    p = jnp.exp(cl)
    pinv = jnp.exp(-cl)
    at_all = -kk * jnp.exp(cl - lw)
    bt_all = kk * rate * pinv
    kt_all = key * pinv
    rt_all = r * p

    def stack(u):
        m = first_head if u.shape[1] == w2 else first_head2
        zero = jnp.zeros_like(u)
        return jnp.concatenate([jnp.where(m, u, zero), jnp.where(m, zero, u)], axis=0)

    def dotb(a, b, dn=_NN):
        return lax.dot_general(a, b, dn, preferred_element_type=F32)

    def pdot(mat, u):
        return dotb(mat, stack(u))

    units = [(j, q) for j in range(nsub) for q in pairs]
    cut = lambda u: [u[j * c:(j + 1) * c, sls[q]] for j, q in units]
    rt = cut(rt_all)
    at, bt, kt, rtb, v = [cut(u.astype(BF16)) for u in (at_all, bt_all, kt_all, rt_all, v_all)]
    us = range(len(units))

    a_ab, a_ak, m_rb, m_rk = [], [], [], []
    for u in us:
        lhs = jnp.concatenate([at[u], rtb[u]], axis=0)
        mbk = dotb(lhs, jnp.concatenate([stack(bt[u]), stack(kt[u])], axis=0), _NT)
        a_ab.append(jnp.where(strict, mbk[:c, :2 * c], 0.0).astype(BF16))
        m_rb.append(jnp.where(incl, mbk[c:, :2 * c], 0.0).astype(BF16))
        a_ak.append(jnp.where(strict, mbk[:c, 2 * c:], 0.0).astype(BF16))
        m_rk.append(jnp.where(incl, mbk[c:, 2 * c:], 0.0).astype(BF16))

    tinv = [eye_pair + a_ab[u].astype(F32) for u in us]
    ak = [pdot(a_ab[u], a_ab[u]).astype(BF16) for u in us]
    nsteps = max(1, (c - 1).bit_length()) - 1
    for s in range(nsteps):
        if s + 1 < nsteps:
            both = [pdot(ak[u], jnp.concatenate([ak[u], tinv[u].astype(BF16)], axis=1)) for u in us]
            ak = [both[u][:, :2 * c].astype(BF16) for u in us]
            tinv = [tinv[u] + both[u][:, 2 * c:] for u in us]
        else:
            tinv = [tinv[u] + pdot(ak[u], tinv[u].astype(BF16)) for u in us]
    xu0 = [pdot(a_ak[u], v[u]).astype(BF16) for u in us]
    x = [pdot(tinv[u].astype(BF16), jnp.concatenate([at[u], xu0[u]], axis=1)).astype(BF16)
         for u in us]

    zero_v = [jnp.zeros_like(v[u]) for u in us]
    xv = [jnp.concatenate([zero_v[u], v[u]], axis=1) for u in us]
    ry = [dotb(jnp.concatenate([m_rb[u], m_rk[u]], axis=1),
               jnp.concatenate([stack(x[u]), stack(xv[u])], axis=0)) for u in us]
    rbar = [rt[u] + ry[u][:, :w2] for u in us]
    y0 = [ry[u][:, w2:] for u in us]
    gh = [dotb(jnp.concatenate([x[u], xv[u]], axis=0), jnp.concatenate([bt[u], kt[u]], axis=0), _TN)
          for u in us]
    g = [jnp.where(same_head, gh[u][:w2], 0.0) for u in us]
    hh = [jnp.where(same_head, gh[u][w2:], 0.0) for u in us]

    ys = [None] * len(units)
    state = [s_scr[q] for q in pairs]
    for j in (reversed(range(nsub)) if reverse else range(nsub)):
        for q in pairs:
            u = j * npairs + q
            s0 = state[q]
            ys[u] = _dot1(rbar[u], s0, _NT) + y0[u]
            state[q] = (s0 + _dot1(s0, g[u]) + hh[u]) * p[j * c + last:j * c + last + 1, sls[q]]
    for q in pairs:
        s_scr[q] = state[q]
    y = jnp.concatenate([jnp.concatenate(ys[j * npairs:(j + 1) * npairs], axis=1)
                         for j in range(nsub)], axis=0)

    if reverse:
        y = yf_ref[0] + y
        inv_n = 1.0 / RWKV_HEAD_DIM
        rate_f = jax.nn.sigmoid(a0f_ref[...] + lora[:, 2 * rw:])
        k_mid = 0.5 * (k * (1.0 + (rate_f - 1.0) * ka_ref[...]) + key)
        sums = head_sum(jnp.concatenate([y, r * k_mid * rk_ref[...]], axis=0))
        dev = y - sums[:cb] * inv_n
        yn = dev * lax.rsqrt(head_sum(dev * dev) * inv_n + RWKV_GN_EPS)
        bonus = sums[cb:] * v_all
        gate = jnp.dot(jax.nn.sigmoid(gd).astype(BF16), gup_ref[...], preferred_element_type=F32)
        y = (yn * lng_ref[...] + lnb_ref[...] + bonus) * gate
    y_ref[0] = y

    @pl.when(i == pl.num_programs(1) - 1)
    def _():
        sT_ref[0] = s_scr[...]


def _wkv_scan(z, p, d, s0, y_fwd=None):
    mu, w0, w_up, a0, a_up, g_up, k_k, k_a, r_k, lnx_g, lnx_b = p
    bsz, t, zw = z.shape
    w = k_k.shape[0]
    w2 = 2 * RWKV_HEAD_DIM
    npairs = w // w2
    c = WKV_CHUNK
    assert c == RWKV_HEAD_DIM and t % c == 0 and w % w2 == 0
    nsub = WKV_CHUNKS_PER_STEP if (t // c) % WKV_CHUNKS_PER_STEP == 0 else 1
    cb = nsub * c
    nc = t // cb
    reverse = d == 1
    per = cb // 8
    nhb = t // 8
    pos = (lambda i: nc - 1 - i) if reverse else (lambda i: i)
    tok = lambda bi, i: (bi, pos(i), 0)
    row1 = lambda u: u.reshape(1, -1)
    zero = jnp.zeros((DECAY_LORA, w), F32)
    blocks = [jnp.concatenate([w_up[d], zero], axis=0), jnp.concatenate([zero, a_up[d]], axis=0)]
    if reverse:
        blocks.append(jnp.concatenate([zero, a_up[0]], axis=0))
    lora_w = jnp.concatenate(blocks, axis=1).astype(BF16)
    full = lambda arr: pl.BlockSpec(arr.shape, lambda bi, i: (0, 0))
    st_spec = pl.BlockSpec((1, npairs, w2, w2), lambda bi, i: (bi, 0, 0, 0))
    y_spec = pl.BlockSpec((1, cb, w), tok)
    args = [z, z, z, row1(mu), row1(k_k), row1(k_a), row1(w0[d]), row1(a0[d]), lora_w, s0]
    in_specs = [
        pl.BlockSpec((1, cb, zw), tok),
        pl.BlockSpec((1, 8, zw), lambda bi, i: (bi, jnp.maximum(pos(i) * per - 1, 0), 0)),
        pl.BlockSpec((1, 8, zw), lambda bi, i: (bi, jnp.minimum((pos(i) + 1) * per, nhb - 1), 0)),
    ] + [full(a) for a in args[3:9]] + [st_spec]
    if reverse:
        extra = [row1(a0[0]), g_up.astype(BF16), row1(r_k), row1(lnx_g), row1(lnx_b)]
        args += [y_fwd] + extra
        in_specs += [y_spec] + [full(a) for a in extra]
    return pl.pallas_call(
        functools.partial(_wkv_kernel, reverse=reverse, npairs=npairs, chunk=c, nsub=nsub),
        grid=(bsz, nc),
        in_specs=in_specs,
        out_specs=[y_spec, st_spec],
        out_shape=[jax.ShapeDtypeStruct((bsz, t, w), F32),
                   jax.ShapeDtypeStruct((bsz, npairs, w2, w2), F32)],
        scratch_shapes=[pltpu.VMEM((npairs, w2, w2), F32)],
        compiler_params=pltpu.CompilerParams(
            dimension_semantics=("parallel", "arbitrary"),
            vmem_limit_bytes=VMEM_LIMIT_BYTES),
        name="wkv_bwd" if reverse else "wkv_fwd",
    )(*args)


def _ret_kernel(*refs, nheads, reverse, use_rope, nsub):
    refs = list(refs)
    z_ref = refs.pop(0)
    cos_ref, sin_ref = (refs.pop(0), refs.pop(0)) if use_rope else (None, None)
    din_ref, xi_ref, zeta_ref, g_ref, r0_ref = refs[:5]
    refs = refs[5:]
    yf_ref, ng_ref = (refs.pop(0), refs.pop(0)) if reverse else (None, None)
    y_ref, rT_ref, r_scr = refs
    d = RET_HEAD_DIM
    w = nheads * d
    i = pl.program_id(1)

    @pl.when(i == 0)
    def _():
        r_scr[...] = r0_ref[0]

    c = RET_CHUNK
    units = [(j, h) for j in range(nsub) for h in range(nheads)]
    us = range(len(units))
    rows = lambda j: slice(j * c, (j + 1) * c)
    cols = lambda g, h: slice(g * w + h * d, g * w + (h + 1) * d)

    def load(g, j, h, is_qk):
        u = z_ref[0, rows(j), cols(g, h)]
        if g == 0:
            u = u * (d ** -0.5)
        if is_qk and use_rope:
            u = u * cos_ref[rows(j), :] + pltpu.roll(u, d // 2, axis=1) * sin_ref[rows(j), :]
        return u

    q = [load(0, j, h, True) for j, h in units]
    k = [load(1, j, h, True) for j, h in units]
    v = [load(2, j, h, False) for j, h in units]
    s = [_dot1(q[u], k[u], _NT) * din_ref[units[u][1]] for u in us]
    sv = [_dot1(s[u], v[u]) for u in us]
    kv = [_dot1(k[u] * zeta_ref[units[u][1]], v[u], _TN) for u in us]

    ys = [None] * len(units)
    state = [r_scr[h] for h in range(nheads)]
    for j in (reversed(range(nsub)) if reverse else range(nsub)):
        for h in range(nheads):
            u = j * nheads + h
            ys[u] = sv[u] + _dot1(q[u], state[h]) * xi_ref[h]
            state[h] = state[h] * g_ref[h] + kv[u]
    for h in range(nheads):
        r_scr[h] = state[h]

    for u, (j, h) in enumerate(units):
        y = ys[u]
        if reverse:
            y = yf_ref[0, rows(j), cols(0, h)] + y
            dev = y - jnp.mean(y, axis=-1, keepdims=True)
            yn = dev * lax.rsqrt(jnp.mean(dev * dev, axis=-1, keepdims=True) + NORM_EPS)
            gt = z_ref[0, rows(j), cols(3, h)]
            y = gt * jax.nn.sigmoid(gt) * (yn * ng_ref[:, cols(0, h)])
        y_ref[0, rows(j), cols(0, h)] = y

    @pl.when(i == pl.num_programs(1) - 1)
    def _():
        rT_ref[0] = r_scr[...]


def _ret_tables(log_gamma, reverse):
    c, d = RET_CHUNK, RET_HEAD_DIM
    idx = jnp.arange(c, dtype=F32)
    lg = log_gamma[:, None, None]
    if reverse:
        diff = idx[None, :] - idx[:, None]
        keep = diff > 0
        xi = jnp.exp(log_gamma[:, None] * (c - idx))
        zeta = jnp.exp(log_gamma[:, None] * idx)
    else:
        diff = idx[:, None] - idx[None, :]
        keep = diff >= 0
        xi = jnp.exp(log_gamma[:, None] * (idx + 1.0))
        zeta = jnp.exp(log_gamma[:, None] * (c - 1.0 - idx))
    din = jnp.where(keep, jnp.exp(lg * jnp.maximum(diff, 0.0)), 0.0)
    nh = log_gamma.shape[0]
    xi = jnp.broadcast_to(xi[:, :, None], (nh, c, d))
    zeta = jnp.broadcast_to(zeta[:, :, None], (nh, c, d))
    g = jnp.broadcast_to(jnp.exp(log_gamma * c)[:, None, None], (nh, d, d))
    return din, xi, zeta, g


def _ret_scan(z, rope, r0, log_gamma, reverse, y_fwd=None, norm_g=None):
    bsz, t, zw = z.shape
    d = RET_HEAD_DIM
    nh = zw // (4 * d)
    w = nh * d
    c = RET_CHUNK
    assert t % c == 0
    per_step = RET_CHUNKS_PER_STEP[1 if reverse else 0]
    nsub = per_step if (t // c) % per_step == 0 else 1
    cb = nsub * c
    nc = t // cb
    din, xi, zeta, g = _ret_tables(log_gamma, reverse)
    pos = (lambda i: nc - 1 - i) if reverse else (lambda i: i)
    tok = lambda bi, i: (bi, pos(i), 0)
    y_spec = pl.BlockSpec((1, cb, w), tok)
    st_spec = pl.BlockSpec((1, nh, d, d), lambda bi, i: (bi, 0, 0, 0))
    tab_spec = lambda n: pl.BlockSpec((nh, n, d), lambda bi, i: (0, 0, 0))
    args = [z]
    in_specs = [pl.BlockSpec((1, cb, zw), tok)]
    if rope is not None:
        args += list(rope)
        in_specs += [pl.BlockSpec((cb, d), lambda bi, i: (pos(i), 0))] * 2
    args += [din, xi, zeta, g, r0]
    in_specs += [tab_spec(c), tab_spec(c), tab_spec(c), tab_spec(d), st_spec]
    if reverse:
        args += [y_fwd, norm_g.reshape(1, w)]
        in_specs += [y_spec, pl.BlockSpec((1, w), lambda bi, i: (0, 0))]
    return pl.pallas_call(
        functools.partial(_ret_kernel, nheads=nh, reverse=reverse, use_rope=rope is not None, nsub=nsub),
        grid=(bsz, nc),
        in_specs=in_specs,
        out_specs=[y_spec, st_spec],
        out_shape=[jax.ShapeDtypeStruct((bsz, t, w), F32),
                   jax.ShapeDtypeStruct((bsz, nh, d, d), F32)],
        scratch_shapes=[pltpu.VMEM((nh, d, d), F32)],
        compiler_params=pltpu.CompilerParams(
            dimension_semantics=("parallel", "arbitrary"),
            vmem_limit_bytes=VMEM_LIMIT_BYTES),
        name="ret_bwd" if reverse else "ret_fwd",
    )(*args)


def _rwkv_mix(z, p, s0_f, s0_b, want_out):
    y_f, s_f = _wkv_scan(z, p, 0, s0_f)
    out, s_b = _wkv_scan(z, p, 1, s0_b, y_f)
    return (out if want_out else None), s_f, s_b


def _ret_mix(z, rope, r0_f, r0_b, lg_f, lg_b, ret_norm_g, want_out):
    y_f, r_f = _ret_scan(z, rope, r0_f, lg_f, False)
    out, r_b = _ret_scan(z, rope, r0_b, lg_b, True, y_f, ret_norm_g)
    return (out if want_out else None), r_f, r_b


def kernel(x, c, ctx, c_ctx, mod_w, mod_b, norm1_g, norm2_g, w_in, tshift_mu, w0, w_up, a0, a_up, g_up,
           k_k, k_a, r_k, lnx_g, lnx_b, conv_w, ret_norm_g, w_out, ffn_up, ffn_conv, ffn_down,
           final_norm_g):
    bsz, n, dm = x.shape
    depth = mod_w.shape[0]
    rwkv_w = k_k.shape[1]
    rwkv_cols = tshift_mu.shape[1]
    conv_cols = 3 * conv_w.shape[1]
    ret_heads = ret_norm_g.shape[1] // RET_HEAD_DIM
    npairs = rwkv_w // (2 * RWKV_HEAD_DIM)

    rows = n // GRID_W
    row = jnp.repeat(jnp.arange(rows, dtype=F32), GRID_W)
    col = jnp.tile(jnp.arange(GRID_W, dtype=F32), rows)
    pairs = RET_HEAD_DIM // 4
    freqs = 1.0 / (ROPE_BASE ** (jnp.arange(pairs, dtype=F32) / pairs))
    ang = jnp.concatenate([row[:, None] * freqs, col[:, None] * freqs], axis=-1)
    cos, sin = jnp.cos(ang), jnp.sin(ang)
    rope = (jnp.concatenate([cos, cos], axis=-1), jnp.concatenate([-sin, sin], axis=-1))
    hidx = jnp.arange(ret_heads, dtype=F32)
    lg_f = jnp.log1p(-jnp.exp2(-5.0 - hidx))
    lg_b = jnp.log1p(-jnp.exp2(-5.5 - hidx))
    s_zero = jnp.zeros((bsz, npairs, 2 * RWKV_HEAD_DIM, 2 * RWKV_HEAD_DIM), F32)
    r_zero = jnp.zeros((bsz, ret_heads, RET_HEAD_DIM, RET_HEAD_DIM), F32)

    xl = x.astype(F32)
    xc = ctx.astype(F32)
    silu_all = jnp.concatenate([jax.nn.silu(c.astype(F32)), jax.nn.silu(c_ctx.astype(F32))[None]], axis=0)

    for l in range(depth):
        last = l == depth - 1
        mod = _mm(silu_all, mod_w[l]) + mod_b[l]
        ml = [m[:, None, :] for m in jnp.split(mod[:bsz], 6, axis=-1)]
        mc = jnp.split(mod[bsz], 6, axis=-1)
        rwkv_p = (tshift_mu[l], w0[l], w_up[l], a0[l], a_up[l], g_up[l], k_k[l], k_a[l],
                  r_k[l], lnx_g[l], lnx_b[l])
        w_in_l = w_in[l].astype(BF16)
        w_out_l = w_out[l].astype(BF16)
        up_l = ffn_up[l].astype(BF16)
        down_l = ffn_down[l].astype(BF16)

        bc = lambda m: jnp.broadcast_to(m, (bsz, 1, dm))
        widths = (rwkv_cols, conv_cols, w_in_l.shape[1] - rwkv_cols - conv_cols)
        gm_c, sh_c = bc(norm1_g[l] * (1.0 + mc[1])), bc(mc[0])
        gm_l, sh_l = norm1_g[l] * (1.0 + ml[1]), ml[0]
        zc_rw, zc_cv, zc_rt = _mm_norm(xc, gm_c, sh_c, w_in_l, widths)
        zl_rw, zl_cv, zl_rt = _mm_norm(xl, gm_l, sh_l, w_in_l, widths)

        out_c_rw, s_f, s_b = _rwkv_mix(zc_rw, rwkv_p, s_zero, s_zero, not last)
        out_l_rw, _, _ = _rwkv_mix(zl_rw, rwkv_p, s_f, s_b, True)
        out_c_rt, r_f, r_b = _ret_mix(zc_rt, None, r_zero, r_zero, lg_f, lg_b, ret_norm_g[l], not last)
        out_l_rt, _, _ = _ret_mix(zl_rt, rope, r_f, r_b, lg_f, lg_b, ret_norm_g[l], True)

        xl = _mix_out(out_l_rw, zl_cv, out_l_rt, xl, ml[2], w_out_l, conv_w[l])
        xl = _ffn(xl, norm2_g[l] * (1.0 + ml[4]), ml[3], ml[5], up_l, ffn_conv[l], down_l,
                  final_norm_g if last else None)
        if not last:
            xc = _mix_out(out_c_rw, zc_cv, out_c_rt, xc, bc(mc[2]), w_out_l, conv_w[l])
            xc = _ffn(xc, bc(norm2_g[l] * (1.0 + mc[4])), bc(mc[3]), bc(mc[5]), up_l, ffn_conv[l], down_l)

    return xl.astype(x.dtype)
```

```python
import functools

import jax
import jax.numpy as jnp
from jax import lax
from jax.experimental import pallas as pl
from jax.experimental.pallas import tpu as pltpu

F32 = jnp.float32
BF16 = jnp.bfloat16

GRID_W = 64
RWKV_HEAD_DIM = 64
DECAY_LORA = 64
AAA_LORA = 64
GATE_LORA = 128
RET_HEAD_DIM = 128
RET_CHUNK = 128
RET_CHUNKS_PER_STEP = (4, 2)
WKV_CHUNK = 64
WKV_CHUNKS_PER_STEP = 4
ROPE_BASE = 10000.0
NORM_EPS = 1e-6
RWKV_GN_EPS = 64e-5

VMEM_LIMIT_BYTES = 48 * 1024 * 1024
VMEM_LIMIT_IN_PROJ_BYTES = 56 * 1024 * 1024

_NT = (((1,), (1,)), ((), ()))
_TN = (((0,), (0,)), ((), ()))
_NN = (((1,), (0,)), ((), ()))


def _mm_kernel(a_ref, b_ref, o_ref):
    o_ref[...] = jnp.dot(a_ref[...].astype(BF16), b_ref[...].astype(BF16), preferred_element_type=F32)


def _pick_tile(n, candidates):
    for c in candidates:
        if n % c == 0:
            return c
    return n


def _mm_rows(a, b_all, layer):
    m, k = a.shape
    n = b_all.shape[-1]
    mp = -(-m // 8) * 8
    a = jnp.pad(a, ((0, mp - m), (0, 0)))
    tn = _pick_tile(n, (512, 384, 256, 128))
    out = pl.pallas_call(
        _mm_kernel,
        grid=(n // tn,),
        in_specs=[pl.BlockSpec((mp, k), lambda j: (0, 0)),
                  pl.BlockSpec((None, k, tn), lambda j: (layer, 0, j))],
        out_specs=pl.BlockSpec((mp, tn), lambda j: (0, j)),
        out_shape=jax.ShapeDtypeStruct((mp, n), F32),
        compiler_params=pltpu.CompilerParams(
            dimension_semantics=("parallel",),
            vmem_limit_bytes=VMEM_LIMIT_BYTES),
        name="mm",
    )(a, b_all)
    return out[:m]


def _resident(shape, layer=None):
    if layer is None:
        return pl.BlockSpec(shape, lambda *_: (0,) * len(shape), pipeline_mode=pl.Buffered(1))
    return pl.BlockSpec((None,) + tuple(shape), lambda *_: (layer,) + (0,) * len(shape),
                        pipeline_mode=pl.Buffered(1))


def _mm_norm_kernel(x_ref, gm_ref, sh_ref, w_ref, *o_refs):
    xb = x_ref[0]
    ms = jnp.mean(xb * xb, axis=-1, keepdims=True)
    h = (xb * lax.rsqrt(ms + NORM_EPS) * gm_ref[0] + sh_ref[0]).astype(BF16)
    start = 0
    for o_ref in o_refs:
        n = o_ref.shape[-1]
        o_ref[0] = jnp.dot(h, w_ref[:, start:start + n], preferred_element_type=F32)
        start += n


def _mm_norm(x, gmod, shift, w_all, layer, widths):
    bsz, t, dm = x.shape
    assert sum(widths) == w_all.shape[2]
    tm = _pick_tile(t, (256, 128))
    vec_spec = pl.BlockSpec((1, 1, dm), lambda b, i: (b, 0, 0))
    return pl.pallas_call(
        _mm_norm_kernel,
        grid=(bsz, t // tm),
        in_specs=[pl.BlockSpec((1, tm, dm), lambda b, i: (b, i, 0)), vec_spec, vec_spec,
                  _resident(w_all.shape[1:], layer)],
        out_specs=[pl.BlockSpec((1, tm, n), lambda b, i: (b, i, 0)) for n in widths],
        out_shape=[jax.ShapeDtypeStruct((bsz, t, n), F32) for n in widths],
        compiler_params=pltpu.CompilerParams(
            dimension_semantics=("parallel", "parallel"),
            vmem_limit_bytes=VMEM_LIMIT_IN_PROJ_BYTES),
        name="mm_norm",
    )(x, gmod, shift, w_all)


def _mix_out_kernel(rw_ref, cv_ref, cvp_ref, cvn_ref, rt_ref, x_ref, gate_ref, cw_ref, w_ref, o_ref, *, tm):
    i = pl.program_id(1)
    cw = cv_ref.shape[-1] // 3
    zc = cv_ref[0]
    gch = zc[:, cw:2 * cw] * zc[:, 2 * cw:]
    before = jnp.where(i == 0, 0.0, cvp_ref[0, 7:8, cw:2 * cw] * cvp_ref[0, 7:8, 2 * cw:])
    after = jnp.where(i == pl.num_programs(1) - 1, 0.0,
                      cvn_ref[0, 0:1, cw:2 * cw] * cvn_ref[0, 0:1, 2 * cw:])
    trow = lax.broadcasted_iota(jnp.int32, (tm, 1), 0)
    prev = jnp.where(trow == 0, before, pltpu.roll(gch, 1, axis=0))
    nxt = jnp.where(trow == tm - 1, after, pltpu.roll(gch, tm - 1, axis=0))
    conv = zc[:, :cw] * (prev * cw_ref[0:1] + gch * cw_ref[1:2] + nxt * cw_ref[2:3])
    mix = jnp.concatenate([rw_ref[0].astype(BF16), conv.astype(BF16), rt_ref[0].astype(BF16)], axis=1)
    o_ref[0] = x_ref[0] + gate_ref[0] * jnp.dot(mix, w_ref[...], preferred_element_type=F32)


def _mix_out(out_rw, z_cv, out_rt, x, gate, w_out_all, layer, conv_w):
    bsz, t, dm = x.shape
    rw, cvw, rtw = out_rw.shape[-1], z_cv.shape[-1], out_rt.shape[-1]
    tm = _pick_tile(t, (512, 256, 128))
    per = tm // 8
    nhb = t // 8
    return pl.pallas_call(
        functools.partial(_mix_out_kernel, tm=tm),
        grid=(bsz, t // tm),
        in_specs=[
            pl.BlockSpec((1, tm, rw), lambda b, i: (b, i, 0)),
            pl.BlockSpec((1, tm, cvw), lambda b, i: (b, i, 0)),
            pl.BlockSpec((1, 8, cvw), lambda b, i: (b, jnp.maximum(i * per - 1, 0), 0)),
            pl.BlockSpec((1, 8, cvw), lambda b, i: (b, jnp.minimum((i + 1) * per, nhb - 1), 0)),
            pl.BlockSpec((1, tm, rtw), lambda b, i: (b, i, 0)),
            pl.BlockSpec((1, tm, dm), lambda b, i: (b, i, 0)),
            pl.BlockSpec((1, 1, dm), lambda b, i: (b, 0, 0)),
            _resident((3, cvw // 3)),
            _resident((rw + cvw // 3 + rtw, dm), layer),
        ],
        out_specs=pl.BlockSpec((1, tm, dm), lambda b, i: (b, i, 0)),
        out_shape=jax.ShapeDtypeStruct((bsz, t, dm), F32),
        compiler_params=pltpu.CompilerParams(
            dimension_semantics=("parallel", "parallel"),
            vmem_limit_bytes=VMEM_LIMIT_BYTES),
        name="mix_out",
    )(out_rw, z_cv, z_cv, z_cv, out_rt, x, gate, conv_w.T, w_out_all)


FFN_HALO = 16

def _ffn_kernel(*refs, tm, final):
    if final:
        (x_ref, xp_ref, xn_ref, gm_ref, sh_ref, gate_ref, upg_ref, upv_ref, cwg_ref, cwv_ref, dn_ref,
         fg_ref, o_ref, h_scr, acc_scr) = refs
    else:
        (x_ref, xp_ref, xn_ref, gm_ref, sh_ref, gate_ref, upg_ref, upv_ref, cwg_ref, cwv_ref, dn_ref,
         o_ref, h_scr, acc_scr) = refs
    i = pl.program_id(1)
    j = pl.program_id(2)
    halo = FFN_HALO
    rows = tm + 2 * halo

    @pl.when(j == 0)
    def _():
        def modnorm(xb):
            ms = jnp.mean(xb * xb, axis=-1, keepdims=True)
            return xb * lax.rsqrt(ms + NORM_EPS) * gm_ref[0] + sh_ref[0]

        hp = jnp.where(i == 0, 0.0, modnorm(xp_ref[0]))
        hn = jnp.where(i == pl.num_programs(1) - 1, 0.0, modnorm(xn_ref[0]))
        h_scr[0:halo] = hp.astype(BF16)
        h_scr[halo:halo + tm] = modnorm(x_ref[0]).astype(BF16)
        h_scr[halo + tm:rows] = hn.astype(BF16)
        acc_scr[...] = jnp.zeros_like(acc_scr)

    h = h_scr[...]

    def conv_half(up_ref, cw_ref):
        pm = jnp.dot(h, up_ref[...], preferred_element_type=F32)
        cw = cw_ref[...]
        u = (pltpu.roll(pm, 1, axis=0) * cw[0:1] + pm * cw[1:2]
             + pltpu.roll(pm, rows - 1, axis=0) * cw[2:3])
        return u[halo:halo + tm]

    gt = conv_half(upg_ref, cwg_ref)
    vl = conv_half(upv_ref, cwv_ref)
    act = gt * jax.nn.sigmoid(gt) * vl
    acc_scr[...] += jnp.dot(act.astype(BF16), dn_ref[...], preferred_element_type=F32)

    @pl.when(j == pl.num_programs(2) - 1)
    def _():
        out = x_ref[0] + gate_ref[0] * acc_scr[...]
        if final:
            ms = jnp.mean(out * out, axis=-1, keepdims=True)
            out = out * lax.rsqrt(ms + NORM_EPS) * fg_ref[...]
        o_ref[0] = out


def _ffn(x, gmod, shift, gate, up, layer, cw, down, final_g=None):
    bsz, t, dm = x.shape
    f = down.shape[1]
    tm = _pick_tile(t, (512, 256, 128))
    tf = _pick_tile(f, (512, 256, 128))
    halo = FFN_HALO
    nf = f // tf
    per = tm // halo
    nhb = t // halo
    cwt = cw.T
    final = final_g is not None
    vec_spec = pl.BlockSpec((1, 1, dm), lambda b, i, j: (b, 0, 0))
    in_specs = [
        pl.BlockSpec((1, tm, dm), lambda b, i, j: (b, i, 0)),
        pl.BlockSpec((1, halo, dm), lambda b, i, j: (b, jnp.maximum(i * per - 1, 0), 0)),
        pl.BlockSpec((1, halo, dm), lambda b, i, j: (b, jnp.minimum((i + 1) * per, nhb - 1), 0)),
        vec_spec, vec_spec, vec_spec,
        pl.BlockSpec((None, dm, tf), lambda b, i, j: (layer, 0, j)),
        pl.BlockSpec((None, dm, tf), lambda b, i, j: (layer, 0, nf + j)),
        pl.BlockSpec((3, tf), lambda b, i, j: (0, j)),
        pl.BlockSpec((3, tf), lambda b, i, j: (0, nf + j)),
        pl.BlockSpec((None, tf, dm), lambda b, i, j: (layer, j, 0)),
    ]
    args = [x, x, x, gmod, shift, gate, up, up, cwt, cwt, down]
    if final:
        in_specs.append(pl.BlockSpec((1, dm), lambda b, i, j: (0, 0)))
        args.append(final_g.reshape(1, dm))
    return pl.pallas_call(
        functools.partial(_ffn_kernel, tm=tm, final=final),
        grid=(bsz, t // tm, nf),
        in_specs=in_specs,
        out_specs=pl.BlockSpec((1, tm, dm), lambda b, i, j: (b, i, 0)),
        out_shape=jax.ShapeDtypeStruct((bsz, t, dm), F32),
        scratch_shapes=[pltpu.VMEM((tm + 2 * halo, dm), BF16), pltpu.VMEM((tm, dm), F32)],
        compiler_params=pltpu.CompilerParams(
            dimension_semantics=("parallel", "parallel", "arbitrary"),
            vmem_limit_bytes=VMEM_LIMIT_BYTES),
        name="ffn",
    )(*args)


def _split_bf16(x):
    hi = x.astype(BF16)
    lo = (x - hi.astype(F32)).astype(BF16)
    return hi, lo


def _dot1(a, b, dn=_NN):
    return lax.dot_general(a.astype(BF16), b.astype(BF16), dn, preferred_element_type=F32)


def _wkv_kernel(*refs, reverse, npairs, chunk, nsub):
    if reverse:
        (z_ref, zp_ref, zn_ref, mu_ref, kk_ref, ka_ref, w0_ref, a0_ref, lora_ref, s0_ref,
         yf_ref, a0f_ref, gup_ref, rk_ref, lng_ref, lnb_ref, y_ref, sT_ref, s_scr) = refs
    else:
        (z_ref, zp_ref, zn_ref, mu_ref, kk_ref, ka_ref, w0_ref, a0_ref, lora_ref, s0_ref,
         y_ref, sT_ref, s_scr) = refs
    c = chunk
    cb = nsub * c
    w2 = 2 * RWKV_HEAD_DIM
    rw = npairs * w2
    i = pl.program_id(1)
    nc = pl.num_programs(1)
    ci = nc - 1 - i if reverse else i

    @pl.when(i == 0)
    def _():
        s_scr[...] = s0_ref[0]

    row = lax.broadcasted_iota(jnp.int32, (cb, cb), 0)
    col = lax.broadcasted_iota(jnp.int32, (cb, cb), 1)
    same_chunk = row // c == col // c
    cum = (same_chunk & (row <= col if reverse else row >= col)).astype(BF16)
    last = 0 if reverse else c - 1
    first_head = lax.broadcasted_iota(jnp.int32, (c, w2), 1) < RWKV_HEAD_DIM
    first_head2 = jnp.concatenate([first_head, first_head], axis=1)
    brow = lax.broadcasted_iota(jnp.int32, (w2, w2), 0) < RWKV_HEAD_DIM
    bcol = lax.broadcasted_iota(jnp.int32, (w2, w2), 1) < RWKV_HEAD_DIM
    same_head = brow == bcol
    head_ones = same_head.astype(BF16)
    trow = lax.broadcasted_iota(jnp.int32, (c, 2 * c), 0)
    tcol = lax.broadcasted_iota(jnp.int32, (c, 2 * c), 1)
    tcol = jnp.where(tcol < c, tcol, tcol - c)
    eye_pair = jnp.where(trow == tcol, 1.0, 0.0)
    if reverse:
        strict, incl = trow < tcol, trow <= tcol
    else:
        strict, incl = trow > tcol, trow >= tcol
    pairs = range(npairs)
    sls = [slice(q * w2, (q + 1) * w2) for q in pairs]

    def head_sum(u):
        hi, lo = _split_bf16(u)
        ones2 = jnp.concatenate([head_ones, head_ones], axis=0)
        return jnp.concatenate(
            [jnp.dot(jnp.concatenate([hi[:, sl], lo[:, sl]], axis=1), ones2, preferred_element_type=F32)
             for sl in sls], axis=1)

    z = z_ref[0]
    trow = lax.broadcasted_iota(jnp.int32, (cb, 1), 0)
    z_before = jnp.where(ci == 0, 0.0, zp_ref[0, 7:8, :])
    z_after = jnp.where(ci == nc - 1, 0.0, zn_ref[0, 0:1, :])
    zprev = jnp.where(trow == 0, z_before, pltpu.roll(z, 1, axis=0))
    znext = jnp.where(trow == cb - 1, z_after, pltpu.roll(z, cb - 1, axis=0))
    z = z + (0.5 * (zprev + znext) - z) * mu_ref[...]
    r = z[:, :rw]
    k = z[:, rw:2 * rw]
    v_all = z[:, 2 * rw:3 * rw]
    wa = z[:, 3 * rw:3 * rw + w2]
    gd = z[:, 3 * rw + w2:]

    kkr = k * kk_ref[...]
    kk = kkr / jnp.maximum(jnp.sqrt(head_sum(kkr * kkr)), 1e-12)
    decay_lane = lax.broadcasted_iota(jnp.int32, (cb, w2), 1) < DECAY_LORA
    lora = jnp.dot(jnp.where(decay_lane, jnp.tanh(wa), wa).astype(BF16), lora_ref[...],
                   preferred_element_type=F32)
    xw = w0_ref[...] + lora[:, :rw]
    w_log = -(jnp.maximum(-xw, 0.0) + jnp.log1p(jnp.exp(-jnp.abs(xw)))) - 0.5
    lw = -jnp.exp(w_log)
    rate = jax.nn.sigmoid(a0_ref[...] + lora[:, rw:2 * rw])
    key = k * (1.0 + (rate - 1.0) * ka_ref[...])

    lw_hi, lw_lo = _split_bf16(lw)
    cl = jnp.dot(jnp.concatenate([cum, cum], axis=1), jnp.concatenate([lw_hi, lw_lo], axis=0),
                 preferred_element_type=F32)
    p = jnp.exp(cl)
    pinv = jnp.exp(-cl)
    at_all = -kk * jnp.exp(cl - lw)
    bt_all = kk * rate * pinv
    kt_all = key * pinv
    rt_all = r * p

    def stack(u):
        m = first_head if u.shape[1] == w2 else first_head2
        zero = jnp.zeros_like(u)
        return jnp.concatenate([jnp.where(m, u, zero), jnp.where(m, zero, u)], axis=0)

    def dotb(a, b, dn=_NN):
        return lax.dot_general(a, b, dn, preferred_element_type=F32)

    def pdot(mat, u):
        return dotb(mat, stack(u))

    units = [(j, q) for j in range(nsub) for q in pairs]
    cut = lambda u: [u[j * c:(j + 1) * c, sls[q]] for j, q in units]
    rt = cut(rt_all)
    at, bt, kt, rtb, v = [cut(u.astype(BF16)) for u in (at_all, bt_all, kt_all, rt_all, v_all)]
    us = range(len(units))

    a_ab, a_ak, m_rb, m_rk = [], [], [], []
    for u in us:
        lhs = jnp.concatenate([at[u], rtb[u]], axis=0)
        mbk = dotb(lhs, jnp.concatenate([stack(bt[u]), stack(kt[u])], axis=0), _NT)
        a_ab.append(jnp.where(strict, mbk[:c, :2 * c], 0.0).astype(BF16))
        m_rb.append(jnp.where(incl, mbk[c:, :2 * c], 0.0).astype(BF16))
        a_ak.append(jnp.where(strict, mbk[:c, 2 * c:], 0.0).astype(BF16))
        m_rk.append(jnp.where(incl, mbk[c:, 2 * c:], 0.0).astype(BF16))

    tinv = [eye_pair + a_ab[u].astype(F32) for u in us]
    ak = [pdot(a_ab[u], a_ab[u]).astype(BF16) for u in us]
    nsteps = max(1, (c - 1).bit_length()) - 1
    for s in range(nsteps):
        if s + 1 < nsteps:
            both = [pdot(ak[u], jnp.concatenate([ak[u], tinv[u].astype(BF16)], axis=1)) for u in us]
            ak = [both[u][:, :2 * c].astype(BF16) for u in us]
            tinv = [tinv[u] + both[u][:, 2 * c:] for u in us]
        else:
            tinv = [tinv[u] + pdot(ak[u], tinv[u].astype(BF16)) for u in us]
    xu0 = [pdot(a_ak[u], v[u]).astype(BF16) for u in us]
    x = [pdot(tinv[u].astype(BF16), jnp.concatenate([at[u], xu0[u]], axis=1)).astype(BF16)
         for u in us]

    zero_v = [jnp.zeros_like(v[u]) for u in us]
    xv = [jnp.concatenate([zero_v[u], v[u]], axis=1) for u in us]
    ry = [dotb(jnp.concatenate([m_rb[u], m_rk[u]], axis=1),
               jnp.concatenate([stack(x[u]), stack(xv[u])], axis=0)) for u in us]
    rbar = [rt[u] + ry[u][:, :w2] for u in us]
    y0 = [ry[u][:, w2:] for u in us]
    gh = [dotb(jnp.concatenate([x[u], xv[u]], axis=0), jnp.concatenate([bt[u], kt[u]], axis=0), _TN)
          for u in us]
    g = [jnp.where(same_head, gh[u][:w2], 0.0) for u in us]
    hh = [jnp.where(same_head, gh[u][w2:], 0.0) for u in us]

    ys = [None] * len(units)
    state = [s_scr[q] for q in pairs]
    for j in (reversed(range(nsub)) if reverse else range(nsub)):
        for q in pairs:
            u = j * npairs + q
            s0 = state[q]
            ys[u] = _dot1(rbar[u], s0, _NT) + y0[u]
            state[q] = (s0 + _dot1(s0, g[u]) + hh[u]) * p[j * c + last:j * c + last + 1, sls[q]]
    for q in pairs:
        s_scr[q] = state[q]
    y = jnp.concatenate([jnp.concatenate(ys[j * npairs:(j + 1) * npairs], axis=1)
                         for j in range(nsub)], axis=0)

    if reverse:
        y = yf_ref[0] + y
        inv_n = 1.0 / RWKV_HEAD_DIM
        rate_f = jax.nn.sigmoid(a0f_ref[...] + lora[:, 2 * rw:])
        k_mid = 0.5 * (k * (1.0 + (rate_f - 1.0) * ka_ref[...]) + key)
        sums = head_sum(jnp.concatenate([y, r * k_mid * rk_ref[...]], axis=0))
        dev = y - sums[:cb] * inv_n
        yn = dev * lax.rsqrt(head_sum(dev * dev) * inv_n + RWKV_GN_EPS)
        bonus = sums[cb:] * v_all
        gate = jnp.dot(jax.nn.sigmoid(gd).astype(BF16), gup_ref[...], preferred_element_type=F32)
        y = (yn * lng_ref[...] + lnb_ref[...] + bonus) * gate
    y_ref[0] = y

    @pl.when(i == pl.num_programs(1) - 1)
    def _():
        sT_ref[0] = s_scr[...]


def _wkv_scan(z, p, d, s0, y_fwd=None):
    mu, w0, w_up, a0, a_up, g_up, k_k, k_a, r_k, lnx_g, lnx_b = p
    bsz, t, zw = z.shape
    w = k_k.shape[0]
    w2 = 2 * RWKV_HEAD_DIM
    npairs = w // w2
    c = WKV_CHUNK
    assert c == RWKV_HEAD_DIM and t % c == 0 and w % w2 == 0
    nsub = WKV_CHUNKS_PER_STEP if (t // c) % WKV_CHUNKS_PER_STEP == 0 else 1
    cb = nsub * c
    nc = t // cb
    reverse = d == 1
    per = cb // 8
    nhb = t // 8
    pos = (lambda i: nc - 1 - i) if reverse else (lambda i: i)
    tok = lambda bi, i: (bi, pos(i), 0)
    row1 = lambda u: u.reshape(1, -1)
    zero = jnp.zeros((DECAY_LORA, w), F32)
    blocks = [jnp.concatenate([w_up[d], zero], axis=0), jnp.concatenate([zero, a_up[d]], axis=0)]
    if reverse:
        blocks.append(jnp.concatenate([zero, a_up[0]], axis=0))
    lora_w = jnp.concatenate(blocks, axis=1).astype(BF16)
    full = lambda arr: pl.BlockSpec(arr.shape, lambda bi, i: (0, 0))
    st_spec = pl.BlockSpec((1, npairs, w2, w2), lambda bi, i: (bi, 0, 0, 0))
    y_spec = pl.BlockSpec((1, cb, w), tok)
    args = [z, z, z, row1(mu), row1(k_k), row1(k_a), row1(w0[d]), row1(a0[d]), lora_w, s0]
    in_specs = [
        pl.BlockSpec((1, cb, zw), tok),
        pl.BlockSpec((1, 8, zw), lambda bi, i: (bi, jnp.maximum(pos(i) * per - 1, 0), 0)),
        pl.BlockSpec((1, 8, zw), lambda bi, i: (bi, jnp.minimum((pos(i) + 1) * per, nhb - 1), 0)),
    ] + [full(a) for a in args[3:9]] + [st_spec]
    if reverse:
        extra = [row1(a0[0]), g_up.astype(BF16), row1(r_k), row1(lnx_g), row1(lnx_b)]
        args += [y_fwd] + extra
        in_specs += [y_spec] + [full(a) for a in extra]
    return pl.pallas_call(
        functools.partial(_wkv_kernel, reverse=reverse, npairs=npairs, chunk=c, nsub=nsub),
        grid=(bsz, nc),
        in_specs=in_specs,
        out_specs=[y_spec, st_spec],
        out_shape=[jax.ShapeDtypeStruct((bsz, t, w), F32),
                   jax.ShapeDtypeStruct((bsz, npairs, w2, w2), F32)],
        scratch_shapes=[pltpu.VMEM((npairs, w2, w2), F32)],
        compiler_params=pltpu.CompilerParams(
            dimension_semantics=("parallel", "arbitrary"),
            vmem_limit_bytes=VMEM_LIMIT_BYTES),
        name="wkv_bwd" if reverse else "wkv_fwd",
    )(*args)


def _ret_kernel(*refs, nheads, reverse, use_rope, nsub):
    refs = list(refs)
    z_ref = refs.pop(0)
    cos_ref, sin_ref = (refs.pop(0), refs.pop(0)) if use_rope else (None, None)
    din_ref, xi_ref, zeta_ref, g_ref, r0_ref = refs[:5]
    refs = refs[5:]
    yf_ref, ng_ref = (refs.pop(0), refs.pop(0)) if reverse else (None, None)
    y_ref, rT_ref, r_scr = refs
    d = RET_HEAD_DIM
    w = nheads * d
    i = pl.program_id(1)

    @pl.when(i == 0)
    def _():
        r_scr[...] = r0_ref[0]

    c = RET_CHUNK
    units = [(j, h) for j in range(nsub) for h in range(nheads)]
    us = range(len(units))
    rows = lambda j: slice(j * c, (j + 1) * c)
    cols = lambda g, h: slice(g * w + h * d, g * w + (h + 1) * d)

    def load(g, j, h, is_qk):
        u = z_ref[0, rows(j), cols(g, h)]
        if g == 0:
            u = u * (d ** -0.5)
        if is_qk and use_rope:
            u = u * cos_ref[rows(j), :] + pltpu.roll(u, d // 2, axis=1) * sin_ref[rows(j), :]
        return u

    q = [load(0, j, h, True) for j, h in units]
    k = [load(1, j, h, True) for j, h in units]
    v = [load(2, j, h, False) for j, h in units]
    s = [_dot1(q[u], k[u], _NT) * din_ref[units[u][1]] for u in us]
    sv = [_dot1(s[u], v[u]) for u in us]
    kv = [_dot1(k[u] * zeta_ref[units[u][1]], v[u], _TN) for u in us]

    ys = [None] * len(units)
    state = [r_scr[h] for h in range(nheads)]
    for j in (reversed(range(nsub)) if reverse else range(nsub)):
        for h in range(nheads):
            u = j * nheads + h
            ys[u] = sv[u] + _dot1(q[u], state[h]) * xi_ref[h]
            state[h] = state[h] * g_ref[h] + kv[u]
    for h in range(nheads):
        r_scr[h] = state[h]

    for u, (j, h) in enumerate(units):
        y = ys[u]
        if reverse:
            y = yf_ref[0, rows(j), cols(0, h)] + y
            dev = y - jnp.mean(y, axis=-1, keepdims=True)
            yn = dev * lax.rsqrt(jnp.mean(dev * dev, axis=-1, keepdims=True) + NORM_EPS)
            gt = z_ref[0, rows(j), cols(3, h)]
            y = gt * jax.nn.sigmoid(gt) * (yn * ng_ref[:, cols(0, h)])
        y_ref[0, rows(j), cols(0, h)] = y

    @pl.when(i == pl.num_programs(1) - 1)
    def _():
        rT_ref[0] = r_scr[...]


def _ret_tables(log_gamma, reverse):
    c, d = RET_CHUNK, RET_HEAD_DIM
    idx = jnp.arange(c, dtype=F32)
    lg = log_gamma[:, None, None]
    if reverse:
        diff = idx[None, :] - idx[:, None]
        keep = diff > 0
        xi = jnp.exp(log_gamma[:, None] * (c - idx))
        zeta = jnp.exp(log_gamma[:, None] * idx)
    else:
        diff = idx[:, None] - idx[None, :]
        keep = diff >= 0
        xi = jnp.exp(log_gamma[:, None] * (idx + 1.0))
        zeta = jnp.exp(log_gamma[:, None] * (c - 1.0 - idx))
    din = jnp.where(keep, jnp.exp(lg * jnp.maximum(diff, 0.0)), 0.0)
    nh = log_gamma.shape[0]
    xi = jnp.broadcast_to(xi[:, :, None], (nh, c, d))
    zeta = jnp.broadcast_to(zeta[:, :, None], (nh, c, d))
    g = jnp.broadcast_to(jnp.exp(log_gamma * c)[:, None, None], (nh, d, d))
    return din, xi, zeta, g


def _ret_scan(z, rope, r0, log_gamma, reverse, y_fwd=None, norm_g=None):
    bsz, t, zw = z.shape
    d = RET_HEAD_DIM
    nh = zw // (4 * d)
    w = nh * d
    c = RET_CHUNK
    assert t % c == 0
    per_step = RET_CHUNKS_PER_STEP[1 if reverse else 0]
    nsub = per_step if (t // c) % per_step == 0 else 1
    cb = nsub * c
    nc = t // cb
    din, xi, zeta, g = _ret_tables(log_gamma, reverse)
    pos = (lambda i: nc - 1 - i) if reverse else (lambda i: i)
    tok = lambda bi, i: (bi, pos(i), 0)
    y_spec = pl.BlockSpec((1, cb, w), tok)
    st_spec = pl.BlockSpec((1, nh, d, d), lambda bi, i: (bi, 0, 0, 0))
    tab_spec = lambda n: pl.BlockSpec((nh, n, d), lambda bi, i: (0, 0, 0))
    args = [z]
    in_specs = [pl.BlockSpec((1, cb, zw), tok)]
    if rope is not None:
        args += list(rope)
        in_specs += [pl.BlockSpec((cb, d), lambda bi, i: (pos(i), 0))] * 2
    args += [din, xi, zeta, g, r0]
    in_specs += [tab_spec(c), tab_spec(c), tab_spec(c), tab_spec(d), st_spec]
    if reverse:
        args += [y_fwd, norm_g.reshape(1, w)]
        in_specs += [y_spec, pl.BlockSpec((1, w), lambda bi, i: (0, 0))]
    return pl.pallas_call(
        functools.partial(_ret_kernel, nheads=nh, reverse=reverse, use_rope=rope is not None, nsub=nsub),
        grid=(bsz, nc),
        in_specs=in_specs,
        out_specs=[y_spec, st_spec],
        out_shape=[jax.ShapeDtypeStruct((bsz, t, w), F32),
                   jax.ShapeDtypeStruct((bsz, nh, d, d), F32)],
        scratch_shapes=[pltpu.VMEM((nh, d, d), F32)],
        compiler_params=pltpu.CompilerParams(
            dimension_semantics=("parallel", "arbitrary"),
            vmem_limit_bytes=VMEM_LIMIT_BYTES),
        name="ret_bwd" if reverse else "ret_fwd",
    )(*args)


def _rwkv_mix(z, p, s0_f, s0_b, want_out):
    y_f, s_f = _wkv_scan(z, p, 0, s0_f)
    out, s_b = _wkv_scan(z, p, 1, s0_b, y_f)
    return (out if want_out else None), s_f, s_b


def _ret_mix(z, rope, r0_f, r0_b, lg_f, lg_b, ret_norm_g, want_out):
    y_f, r_f = _ret_scan(z, rope, r0_f, lg_f, False)
    out, r_b = _ret_scan(z, rope, r0_b, lg_b, True, y_f, ret_norm_g)
    return (out if want_out else None), r_f, r_b


def kernel(x, c, ctx, c_ctx, mod_w, mod_b, norm1_g, norm2_g, w_in, tshift_mu, w0, w_up, a0, a_up, g_up,
           k_k, k_a, r_k, lnx_g, lnx_b, conv_w, ret_norm_g, w_out, ffn_up, ffn_conv, ffn_down,
           final_norm_g):
    bsz, n, dm = x.shape
    depth = mod_w.shape[0]
    rwkv_w = k_k.shape[1]
    rwkv_cols = tshift_mu.shape[1]
    conv_cols = 3 * conv_w.shape[1]
    ret_heads = ret_norm_g.shape[1] // RET_HEAD_DIM
    npairs = rwkv_w // (2 * RWKV_HEAD_DIM)

    rows = n // GRID_W
    row = jnp.repeat(jnp.arange(rows, dtype=F32), GRID_W)
    col = jnp.tile(jnp.arange(GRID_W, dtype=F32), rows)
    pairs = RET_HEAD_DIM // 4
    freqs = 1.0 / (ROPE_BASE ** (jnp.arange(pairs, dtype=F32) / pairs))
    ang = jnp.concatenate([row[:, None] * freqs, col[:, None] * freqs], axis=-1)
    cos, sin = jnp.cos(ang), jnp.sin(ang)
    rope = (jnp.concatenate([cos, cos], axis=-1), jnp.concatenate([-sin, sin], axis=-1))
    hidx = jnp.arange(ret_heads, dtype=F32)
    lg_f = jnp.log1p(-jnp.exp2(-5.0 - hidx))
    lg_b = jnp.log1p(-jnp.exp2(-5.5 - hidx))
    s_zero = jnp.zeros((bsz, npairs, 2 * RWKV_HEAD_DIM, 2 * RWKV_HEAD_DIM), F32)
    r_zero = jnp.zeros((bsz, ret_heads, RET_HEAD_DIM, RET_HEAD_DIM), F32)

    xl = x.astype(F32)
    xc = ctx.astype(F32)
    silu_all = jnp.concatenate([jax.nn.silu(c.astype(F32)), jax.nn.silu(c_ctx.astype(F32))[None]], axis=0)

    w_in_b, w_out_b, up_b, down_b = (w.astype(BF16) for w in (w_in, w_out, ffn_up, ffn_down))
    widths = (rwkv_cols, conv_cols, w_in.shape[2] - rwkv_cols - conv_cols)
    bc = lambda m: jnp.broadcast_to(m, (bsz, 1, dm))

    for l in range(depth):
        last = l == depth - 1
        mod = _mm_rows(silu_all, mod_w, l) + mod_b[l]
        ml = [m[:, None, :] for m in jnp.split(mod[:bsz], 6, axis=-1)]
        mc = jnp.split(mod[bsz], 6, axis=-1)
        rwkv_p = (tshift_mu[l], w0[l], w_up[l], a0[l], a_up[l], g_up[l], k_k[l], k_a[l],
                  r_k[l], lnx_g[l], lnx_b[l])

        gm_c, sh_c = bc(norm1_g[l] * (1.0 + mc[1])), bc(mc[0])
        gm_l, sh_l = norm1_g[l] * (1.0 + ml[1]), ml[0]
        zc_rw, zc_cv, zc_rt = _mm_norm(xc, gm_c, sh_c, w_in_b, l, widths)
        zl_rw, zl_cv, zl_rt = _mm_norm(xl, gm_l, sh_l, w_in_b, l, widths)

        out_c_rw, s_f, s_b = _rwkv_mix(zc_rw, rwkv_p, s_zero, s_zero, not last)
        out_l_rw, _, _ = _rwkv_mix(zl_rw, rwkv_p, s_f, s_b, True)
        out_c_rt, r_f, r_b = _ret_mix(zc_rt, None, r_zero, r_zero, lg_f, lg_b, ret_norm_g[l], not last)
        out_l_rt, _, _ = _ret_mix(zl_rt, rope, r_f, r_b, lg_f, lg_b, ret_norm_g[l], True)

        xl = _mix_out(out_l_rw, zl_cv, out_l_rt, xl, ml[2], w_out_b, l, conv_w[l])
        xl = _ffn(xl, norm2_g[l] * (1.0 + ml[4]), ml[3], ml[5], up_b, l, ffn_conv[l], down_b,
                  final_norm_g if last else None)
        if not last:
            xc = _mix_out(out_c_rw, zc_cv, out_c_rt, xc, bc(mc[2]), w_out_b, l, conv_w[l])
            xc = _ffn(xc, bc(norm2_g[l] * (1.0 + mc[4])), bc(mc[3]), bc(mc[5]), up_b, l, ffn_conv[l],
                      down_b)

    return xl.astype(x.dtype)
```

```python
import functools

import jax
import jax.numpy as jnp
from jax import lax
from jax.experimental import pallas as pl
from jax.experimental.pallas import tpu as pltpu

F32 = jnp.float32
BF16 = jnp.bfloat16

GRID_W = 64
RWKV_HEAD_DIM = 64
DECAY_LORA = 64
AAA_LORA = 64
GATE_LORA = 128
RET_HEAD_DIM = 128
RET_CHUNK = 128
RET_CHUNKS_PER_STEP = (4, 2)
WKV_CHUNK = 64
WKV_CHUNKS_PER_STEP = 4
ROPE_BASE = 10000.0
NORM_EPS = 1e-6
RWKV_GN_EPS = 64e-5

VMEM_LIMIT_BYTES = 48 * 1024 * 1024
VMEM_LIMIT_IN_PROJ_BYTES = 56 * 1024 * 1024

_NT = (((1,), (1,)), ((), ()))
_TN = (((0,), (0,)), ((), ()))
_NN = (((1,), (0,)), ((), ()))


def _mm_kernel(a_ref, b_ref, o_ref):
    o_ref[...] = jnp.dot(a_ref[...].astype(BF16), b_ref[...].astype(BF16), preferred_element_type=F32)


def _pick_tile(n, candidates):
    for c in candidates:
        if n % c == 0:
            return c
    return n


def _mm_rows(a, b_all, layer):
    m, k = a.shape
    n = b_all.shape[-1]
    mp = -(-m // 8) * 8
    a = jnp.pad(a, ((0, mp - m), (0, 0)))
    tn = _pick_tile(n, (512, 384, 256, 128))
    out = pl.pallas_call(
        _mm_kernel,
        grid=(n // tn,),
        in_specs=[pl.BlockSpec((mp, k), lambda j: (0, 0)),
                  pl.BlockSpec((None, k, tn), lambda j: (layer, 0, j))],
        out_specs=pl.BlockSpec((mp, tn), lambda j: (0, j)),
        out_shape=jax.ShapeDtypeStruct((mp, n), F32),
        compiler_params=pltpu.CompilerParams(
            dimension_semantics=("parallel",),
            vmem_limit_bytes=VMEM_LIMIT_BYTES),
        name="mm",
    )(a, b_all)
    return out[:m]


def _resident(shape, layer=None):
    if layer is None:
        return pl.BlockSpec(shape, lambda *_: (0,) * len(shape), pipeline_mode=pl.Buffered(1))
    return pl.BlockSpec((None,) + tuple(shape), lambda *_: (layer,) + (0,) * len(shape),
                        pipeline_mode=pl.Buffered(1))


def _mm_norm_kernel(x_ref, gm_ref, sh_ref, w_ref, *o_refs):
    xb = x_ref[0]
    ms = jnp.mean(xb * xb, axis=-1, keepdims=True)
    h = (xb * lax.rsqrt(ms + NORM_EPS) * gm_ref[0] + sh_ref[0]).astype(BF16)
    start = 0
    for o_ref in o_refs:
        n = o_ref.shape[-1]
        o_ref[0] = jnp.dot(h, w_ref[:, start:start + n], preferred_element_type=F32)
        start += n


def _mm_norm(x, gmod, shift, w_all, layer, widths):
    bsz, t, dm = x.shape
    assert sum(widths) == w_all.shape[2]
    tm = _pick_tile(t, (256, 128))
    vec_spec = pl.BlockSpec((1, 1, dm), lambda b, i: (b, 0, 0))
    return pl.pallas_call(
        _mm_norm_kernel,
        grid=(bsz, t // tm),
        in_specs=[pl.BlockSpec((1, tm, dm), lambda b, i: (b, i, 0)), vec_spec, vec_spec,
                  _resident(w_all.shape[1:], layer)],
        out_specs=[pl.BlockSpec((1, tm, n), lambda b, i: (b, i, 0)) for n in widths],
        out_shape=[jax.ShapeDtypeStruct((bsz, t, n), F32) for n in widths],
        compiler_params=pltpu.CompilerParams(
            dimension_semantics=("parallel", "parallel"),
            vmem_limit_bytes=VMEM_LIMIT_IN_PROJ_BYTES),
        name="mm_norm",
    )(x, gmod, shift, w_all)


def _mix_out_kernel(rw_ref, cv_ref, cvp_ref, cvn_ref, rt_ref, x_ref, gate_ref, cw_ref, w_ref, o_ref, *, tm):
    i = pl.program_id(1)
    cw = cv_ref.shape[-1] // 3
    zc = cv_ref[0]
    gch = zc[:, cw:2 * cw] * zc[:, 2 * cw:]
    before = jnp.where(i == 0, 0.0, cvp_ref[0, 7:8, cw:2 * cw] * cvp_ref[0, 7:8, 2 * cw:])
    after = jnp.where(i == pl.num_programs(1) - 1, 0.0,
                      cvn_ref[0, 0:1, cw:2 * cw] * cvn_ref[0, 0:1, 2 * cw:])
    trow = lax.broadcasted_iota(jnp.int32, (tm, 1), 0)
    prev = jnp.where(trow == 0, before, pltpu.roll(gch, 1, axis=0))
    nxt = jnp.where(trow == tm - 1, after, pltpu.roll(gch, tm - 1, axis=0))
    conv = zc[:, :cw] * (prev * cw_ref[0:1] + gch * cw_ref[1:2] + nxt * cw_ref[2:3])
    mix = jnp.concatenate([rw_ref[0].astype(BF16), conv.astype(BF16), rt_ref[0].astype(BF16)], axis=1)
    o_ref[0] = x_ref[0] + gate_ref[0] * jnp.dot(mix, w_ref[...], preferred_element_type=F32)


def _mix_out(out_rw, z_cv, out_rt, x, gate, w_out_all, layer, conv_w):
    bsz, t, dm = x.shape
    rw, cvw, rtw = out_rw.shape[-1], z_cv.shape[-1], out_rt.shape[-1]
    tm = _pick_tile(t, (512, 256, 128))
    per = tm // 8
    nhb = t // 8
    return pl.pallas_call(
        functools.partial(_mix_out_kernel, tm=tm),
        grid=(bsz, t // tm),
        in_specs=[
            pl.BlockSpec((1, tm, rw), lambda b, i: (b, i, 0)),
            pl.BlockSpec((1, tm, cvw), lambda b, i: (b, i, 0)),
            pl.BlockSpec((1, 8, cvw), lambda b, i: (b, jnp.maximum(i * per - 1, 0), 0)),
            pl.BlockSpec((1, 8, cvw), lambda b, i: (b, jnp.minimum((i + 1) * per, nhb - 1), 0)),
            pl.BlockSpec((1, tm, rtw), lambda b, i: (b, i, 0)),
            pl.BlockSpec((1, tm, dm), lambda b, i: (b, i, 0)),
            pl.BlockSpec((1, 1, dm), lambda b, i: (b, 0, 0)),
            _resident((3, cvw // 3)),
            _resident((rw + cvw // 3 + rtw, dm), layer),
        ],
        out_specs=pl.BlockSpec((1, tm, dm), lambda b, i: (b, i, 0)),
        out_shape=jax.ShapeDtypeStruct((bsz, t, dm), F32),
        compiler_params=pltpu.CompilerParams(
            dimension_semantics=("parallel", "parallel"),
            vmem_limit_bytes=VMEM_LIMIT_BYTES),
        name="mix_out",
    )(out_rw, z_cv, z_cv, z_cv, out_rt, x, gate, conv_w.T, w_out_all)


FFN_HALO = 16

def _ffn_kernel(*refs, tm, final):
    if final:
        (x_ref, xp_ref, xn_ref, gm_ref, sh_ref, gate_ref, upg_ref, upv_ref, cwg_ref, cwv_ref, dn_ref,
         fg_ref, o_ref, h_scr, acc_scr) = refs
    else:
        (x_ref, xp_ref, xn_ref, gm_ref, sh_ref, gate_ref, upg_ref, upv_ref, cwg_ref, cwv_ref, dn_ref,
         o_ref, h_scr, acc_scr) = refs
    i = pl.program_id(1)
    j = pl.program_id(2)
    halo = FFN_HALO
    rows = tm + 2 * halo

    @pl.when(j == 0)
    def _():
        def modnorm(xb):
            ms = jnp.mean(xb * xb, axis=-1, keepdims=True)
            return xb * lax.rsqrt(ms + NORM_EPS) * gm_ref[0] + sh_ref[0]

        hp = jnp.where(i == 0, 0.0, modnorm(xp_ref[0]))
        hn = jnp.where(i == pl.num_programs(1) - 1, 0.0, modnorm(xn_ref[0]))
        h_scr[0:halo] = hp.astype(BF16)
        h_scr[halo:halo + tm] = modnorm(x_ref[0]).astype(BF16)
        h_scr[halo + tm:rows] = hn.astype(BF16)
        acc_scr[...] = jnp.zeros_like(acc_scr)

    h = h_scr[...]

    def conv_half(up_ref, cw_ref):
        pm = jnp.dot(h, up_ref[...], preferred_element_type=F32)
        cw = cw_ref[...]
        u = (pltpu.roll(pm, 1, axis=0) * cw[0:1] + pm * cw[1:2]
             + pltpu.roll(pm, rows - 1, axis=0) * cw[2:3])
        return u[halo:halo + tm]

    gt = conv_half(upg_ref, cwg_ref)
    vl = conv_half(upv_ref, cwv_ref)
    act = gt * jax.nn.sigmoid(gt) * vl
    acc_scr[...] += jnp.dot(act.astype(BF16), dn_ref[...], preferred_element_type=F32)

    @pl.when(j == pl.num_programs(2) - 1)
    def _():
        out = x_ref[0] + gate_ref[0] * acc_scr[...]
        if final:
            ms = jnp.mean(out * out, axis=-1, keepdims=True)
            out = out * lax.rsqrt(ms + NORM_EPS) * fg_ref[...]
        o_ref[0] = out


def _ffn(x, gmod, shift, gate, up, layer, cw, down, final_g=None):
    bsz, t, dm = x.shape
    f = down.shape[1]
    tm = _pick_tile(t, (512, 256, 128))
    tf = _pick_tile(f, (512, 256, 128))
    halo = FFN_HALO
    nf = f // tf
    per = tm // halo
    nhb = t // halo
    cwt = cw.T
    final = final_g is not None
    vec_spec = pl.BlockSpec((1, 1, dm), lambda b, i, j: (b, 0, 0))
    in_specs = [
        pl.BlockSpec((1, tm, dm), lambda b, i, j: (b, i, 0)),
        pl.BlockSpec((1, halo, dm), lambda b, i, j: (b, jnp.maximum(i * per - 1, 0), 0)),
        pl.BlockSpec((1, halo, dm), lambda b, i, j: (b, jnp.minimum((i + 1) * per, nhb - 1), 0)),
        vec_spec, vec_spec, vec_spec,
        pl.BlockSpec((None, dm, tf), lambda b, i, j: (layer, 0, j)),
        pl.BlockSpec((None, dm, tf), lambda b, i, j: (layer, 0, nf + j)),
        pl.BlockSpec((3, tf), lambda b, i, j: (0, j)),
        pl.BlockSpec((3, tf), lambda b, i, j: (0, nf + j)),
        pl.BlockSpec((None, tf, dm), lambda b, i, j: (layer, j, 0)),
    ]
    args = [x, x, x, gmod, shift, gate, up, up, cwt, cwt, down]
    if final:
        in_specs.append(pl.BlockSpec((1, dm), lambda b, i, j: (0, 0)))
        args.append(final_g.reshape(1, dm))
    return pl.pallas_call(
        functools.partial(_ffn_kernel, tm=tm, final=final),
        grid=(bsz, t // tm, nf),
        in_specs=in_specs,
        out_specs=pl.BlockSpec((1, tm, dm), lambda b, i, j: (b, i, 0)),
        out_shape=jax.ShapeDtypeStruct((bsz, t, dm), F32),
        scratch_shapes=[pltpu.VMEM((tm + 2 * halo, dm), BF16), pltpu.VMEM((tm, dm), F32)],
        compiler_params=pltpu.CompilerParams(
            dimension_semantics=("parallel", "parallel", "arbitrary"),
            vmem_limit_bytes=VMEM_LIMIT_BYTES),
        name="ffn",
    )(*args)


def _split_bf16(x):
    hi = x.astype(BF16)
    lo = (x - hi.astype(F32)).astype(BF16)
    return hi, lo


def _dot1(a, b, dn=_NN):
    return lax.dot_general(a.astype(BF16), b.astype(BF16), dn, preferred_element_type=F32)


def _wkv_kernel(*refs, reverse, npairs, chunk, nsub):
    if reverse:
        (feat_ref, ka_ref, w0_ref, a0_ref, lora_ref, s0_ref,
         yf_ref, a0f_ref, gup_ref, rk_ref, lng_ref, lnb_ref, y_ref, sT_ref, s_scr) = refs
    else:
        (z_ref, zp_ref, zn_ref, mu_ref, kk_ref, ka_ref, w0_ref, a0_ref, lora_ref, s0_ref,
         y_ref, feat_ref, sT_ref, s_scr) = refs
    c = chunk
    cb = nsub * c
    w2 = 2 * RWKV_HEAD_DIM
    rw = npairs * w2
    i = pl.program_id(1)
    nc = pl.num_programs(1)
    ci = nc - 1 - i if reverse else i

    @pl.when(i == 0)
    def _():
        s_scr[...] = s0_ref[0]

    row = lax.broadcasted_iota(jnp.int32, (cb, cb), 0)
    col = lax.broadcasted_iota(jnp.int32, (cb, cb), 1)
    same_chunk = row // c == col // c
    cum = (same_chunk & (row <= col if reverse else row >= col)).astype(BF16)
    last = 0 if reverse else c - 1
    first_head = lax.broadcasted_iota(jnp.int32, (c, w2), 1) < RWKV_HEAD_DIM
    first_head2 = jnp.concatenate([first_head, first_head], axis=1)
    brow = lax.broadcasted_iota(jnp.int32, (w2, w2), 0) < RWKV_HEAD_DIM
    bcol = lax.broadcasted_iota(jnp.int32, (w2, w2), 1) < RWKV_HEAD_DIM
    same_head = brow == bcol
    head_ones = same_head.astype(BF16)
    trow = lax.broadcasted_iota(jnp.int32, (c, 2 * c), 0)
    tcol = lax.broadcasted_iota(jnp.int32, (c, 2 * c), 1)
    tcol = jnp.where(tcol < c, tcol, tcol - c)
    eye_pair = jnp.where(trow == tcol, 1.0, 0.0)
    if reverse:
        strict, incl = trow < tcol, trow <= tcol
    else:
        strict, incl = trow > tcol, trow >= tcol
    pairs = range(npairs)
    sls = [slice(q * w2, (q + 1) * w2) for q in pairs]

    def head_sum(u):
        hi, lo = _split_bf16(u)
        ones2 = jnp.concatenate([head_ones, head_ones], axis=0)
        return jnp.concatenate(
            [jnp.dot(jnp.concatenate([hi[:, sl], lo[:, sl]], axis=1), ones2, preferred_element_type=F32)
             for sl in sls], axis=1)

    zw = 3 * rw + 2 * w2
    if reverse:
        z = feat_ref[0, :, :zw]
        kk = feat_ref[0, :, zw:]
    else:
        z = z_ref[0]
        trow = lax.broadcasted_iota(jnp.int32, (cb, 1), 0)
        z_before = jnp.where(ci == 0, 0.0, zp_ref[0, 7:8, :])
        z_after = jnp.where(ci == nc - 1, 0.0, zn_ref[0, 0:1, :])
        zprev = jnp.where(trow == 0, z_before, pltpu.roll(z, 1, axis=0))
        znext = jnp.where(trow == cb - 1, z_after, pltpu.roll(z, cb - 1, axis=0))
        z = z + (0.5 * (zprev + znext) - z) * mu_ref[...]
    r = z[:, :rw]
    k = z[:, rw:2 * rw]
    v_all = z[:, 2 * rw:3 * rw]
    wa = z[:, 3 * rw:3 * rw + w2]
    gd = z[:, 3 * rw + w2:]

    if not reverse:
        kkr = k * kk_ref[...]
        kk = kkr / jnp.maximum(jnp.sqrt(head_sum(kkr * kkr)), 1e-12)
        feat_ref[0, :, :zw] = z
        feat_ref[0, :, zw:] = kk
    decay_lane = lax.broadcasted_iota(jnp.int32, (cb, w2), 1) < DECAY_LORA
    lora = jnp.dot(jnp.where(decay_lane, jnp.tanh(wa), wa).astype(BF16), lora_ref[...],
                   preferred_element_type=F32)
    xw = w0_ref[...] + lora[:, :rw]
    w_log = -(jnp.maximum(-xw, 0.0) + jnp.log1p(jnp.exp(-jnp.abs(xw)))) - 0.5
    lw = -jnp.exp(w_log)
    rate = jax.nn.sigmoid(a0_ref[...] + lora[:, rw:2 * rw])
    key = k * (1.0 + (rate - 1.0) * ka_ref[...])

    lw_hi, lw_lo = _split_bf16(lw)
    cl = jnp.dot(jnp.concatenate([cum, cum], axis=1), jnp.concatenate([lw_hi, lw_lo], axis=0),
                 preferred_element_type=F32)
    p = jnp.exp(cl)
    pinv = jnp.exp(-cl)
    at_all = -kk * jnp.exp(cl - lw)
    bt_all = kk * rate * pinv
    kt_all = key * pinv
    rt_all = r * p

    def stack(u):
        m = first_head if u.shape[1] == w2 else first_head2
        zero = jnp.zeros_like(u)
        return jnp.concatenate([jnp.where(m, u, zero), jnp.where(m, zero, u)], axis=0)

    def dotb(a, b, dn=_NN):
        return lax.dot_general(a, b, dn, preferred_element_type=F32)

    def pdot(mat, u):
        return dotb(mat, stack(u))

    units = [(j, q) for j in range(nsub) for q in pairs]
    cut = lambda u: [u[j * c:(j + 1) * c, sls[q]] for j, q in units]
    rt = cut(rt_all)
    at, bt, kt, rtb, v = [cut(u.astype(BF16)) for u in (at_all, bt_all, kt_all, rt_all, v_all)]
    us = range(len(units))

    a_ab, a_ak, m_rb, m_rk = [], [], [], []
    for u in us:
        lhs = jnp.concatenate([at[u], rtb[u]], axis=0)
        mbk = dotb(lhs, jnp.concatenate([stack(bt[u]), stack(kt[u])], axis=0), _NT)
        a_ab.append(jnp.where(strict, mbk[:c, :2 * c], 0.0).astype(BF16))
        m_rb.append(jnp.where(incl, mbk[c:, :2 * c], 0.0).astype(BF16))
        a_ak.append(jnp.where(strict, mbk[:c, 2 * c:], 0.0).astype(BF16))
        m_rk.append(jnp.where(incl, mbk[c:, 2 * c:], 0.0).astype(BF16))

    tinv = [eye_pair + a_ab[u].astype(F32) for u in us]
    ak = [pdot(a_ab[u], a_ab[u]).astype(BF16) for u in us]
    nsteps = max(1, (c - 1).bit_length()) - 1
    for s in range(nsteps):
        if s + 1 < nsteps:
            both = [pdot(ak[u], jnp.concatenate([ak[u], tinv[u].astype(BF16)], axis=1)) for u in us]
            ak = [both[u][:, :2 * c].astype(BF16) for u in us]
            tinv = [tinv[u] + both[u][:, 2 * c:] for u in us]
        else:
            tinv = [tinv[u] + pdot(ak[u], tinv[u].astype(BF16)) for u in us]
    xu0 = [pdot(a_ak[u], v[u]).astype(BF16) for u in us]
    x = [pdot(tinv[u].astype(BF16), jnp.concatenate([at[u], xu0[u]], axis=1)).astype(BF16)
         for u in us]

    zero_v = [jnp.zeros_like(v[u]) for u in us]
    xv = [jnp.concatenate([zero_v[u], v[u]], axis=1) for u in us]
    ry = [dotb(jnp.concatenate([m_rb[u], m_rk[u]], axis=1),
               jnp.concatenate([stack(x[u]), stack(xv[u])], axis=0)) for u in us]
    rbar = [rt[u] + ry[u][:, :w2] for u in us]
    y0 = [ry[u][:, w2:] for u in us]
    gh = [dotb(jnp.concatenate([x[u], xv[u]], axis=0), jnp.concatenate([bt[u], kt[u]], axis=0), _TN)
          for u in us]
    g = [jnp.where(same_head, gh[u][:w2], 0.0) for u in us]
    hh = [jnp.where(same_head, gh[u][w2:], 0.0) for u in us]

    ys = [None] * len(units)
    state = [s_scr[q] for q in pairs]
    for j in (reversed(range(nsub)) if reverse else range(nsub)):
        for q in pairs:
            u = j * npairs + q
            s0 = state[q]
            ys[u] = _dot1(rbar[u], s0, _NT) + y0[u]
            state[q] = (s0 + _dot1(s0, g[u]) + hh[u]) * p[j * c + last:j * c + last + 1, sls[q]]
    for q in pairs:
        s_scr[q] = state[q]
    y = jnp.concatenate([jnp.concatenate(ys[j * npairs:(j + 1) * npairs], axis=1)
                         for j in range(nsub)], axis=0)

    if reverse:
        y = yf_ref[0] + y
        inv_n = 1.0 / RWKV_HEAD_DIM
        rate_f = jax.nn.sigmoid(a0f_ref[...] + lora[:, 2 * rw:])
        k_mid = 0.5 * (k * (1.0 + (rate_f - 1.0) * ka_ref[...]) + key)
        sums = head_sum(jnp.concatenate([y, r * k_mid * rk_ref[...]], axis=0))
        dev = y - sums[:cb] * inv_n
        yn = dev * lax.rsqrt(head_sum(dev * dev) * inv_n + RWKV_GN_EPS)
        bonus = sums[cb:] * v_all
        gate = jnp.dot(jax.nn.sigmoid(gd).astype(BF16), gup_ref[...], preferred_element_type=F32)
        y = (yn * lng_ref[...] + lnb_ref[...] + bonus) * gate
    y_ref[0] = y

    @pl.when(i == pl.num_programs(1) - 1)
    def _():
        sT_ref[0] = s_scr[...]


def _wkv_scan(z, p, d, s0, y_fwd=None):
    mu, w0, w_up, a0, a_up, g_up, k_k, k_a, r_k, lnx_g, lnx_b = p
    bsz, t, zw = z.shape
    w = k_k.shape[0]
    w2 = 2 * RWKV_HEAD_DIM
    npairs = w // w2
    c = WKV_CHUNK
    assert c == RWKV_HEAD_DIM and t % c == 0 and w % w2 == 0
    nsub = WKV_CHUNKS_PER_STEP if (t // c) % WKV_CHUNKS_PER_STEP == 0 else 1
    cb = nsub * c
    nc = t // cb
    reverse = d == 1
    per = cb // 8
    nhb = t // 8
    pos = (lambda i: nc - 1 - i) if reverse else (lambda i: i)
    tok = lambda bi, i: (bi, pos(i), 0)
    row1 = lambda u: u.reshape(1, -1)
    zero = jnp.zeros((DECAY_LORA, w), F32)
    blocks = [jnp.concatenate([w_up[d], zero], axis=0), jnp.concatenate([zero, a_up[d]], axis=0)]
    if reverse:
        blocks.append(jnp.concatenate([zero, a_up[0]], axis=0))
    lora_w = jnp.concatenate(blocks, axis=1).astype(BF16)
    full = lambda arr: pl.BlockSpec(arr.shape, lambda bi, i: (0, 0))
    st_spec = pl.BlockSpec((1, npairs, w2, w2), lambda bi, i: (bi, 0, 0, 0))
    y_spec = pl.BlockSpec((1, cb, w), tok)
    common = [row1(k_a), row1(w0[d]), row1(a0[d]), lora_w]
    out_specs = [y_spec, st_spec]
    out_shape = [jax.ShapeDtypeStruct((bsz, t, w), F32), jax.ShapeDtypeStruct((bsz, npairs, w2, w2), F32)]
    if reverse:
        extra = [row1(a0[0]), g_up.astype(BF16), row1(r_k), row1(lnx_g), row1(lnx_b)]
        args = [z] + common + [s0, y_fwd] + extra
        in_specs = ([pl.BlockSpec((1, cb, zw), tok)] + [full(a) for a in common] + [st_spec, y_spec]
                    + [full(a) for a in extra])
    else:
        head = [row1(mu), row1(k_k)]
        args = [z, z, z] + head + common + [s0]
        in_specs = [
            pl.BlockSpec((1, cb, zw), tok),
            pl.BlockSpec((1, 8, zw), lambda bi, i: (bi, jnp.maximum(pos(i) * per - 1, 0), 0)),
            pl.BlockSpec((1, 8, zw), lambda bi, i: (bi, jnp.minimum((pos(i) + 1) * per, nhb - 1), 0)),
        ] + [full(a) for a in head + common] + [st_spec]
        out_specs.insert(1, pl.BlockSpec((1, cb, zw + w), tok))
        out_shape.insert(1, jax.ShapeDtypeStruct((bsz, t, zw + w), F32))
    return pl.pallas_call(
        functools.partial(_wkv_kernel, reverse=reverse, npairs=npairs, chunk=c, nsub=nsub),
        grid=(bsz, nc),
        in_specs=in_specs,
        out_specs=out_specs,
        out_shape=out_shape,
        scratch_shapes=[pltpu.VMEM((npairs, w2, w2), F32)],
        compiler_params=pltpu.CompilerParams(
            dimension_semantics=("parallel", "arbitrary"),
            vmem_limit_bytes=VMEM_LIMIT_BYTES),
        name="wkv_bwd" if reverse else "wkv_fwd",
    )(*args)


def _ret_kernel(*refs, nheads, reverse, use_rope, nsub):
    refs = list(refs)
    z_ref = refs.pop(0)
    cos_ref, sin_ref = (refs.pop(0), refs.pop(0)) if use_rope else (None, None)
    din_ref, xi_ref, zeta_ref, g_ref, r0_ref = refs[:5]
    refs = refs[5:]
    yf_ref, ng_ref = (refs.pop(0), refs.pop(0)) if reverse else (None, None)
    y_ref, rT_ref, r_scr = refs
    d = RET_HEAD_DIM
    w = nheads * d
    i = pl.program_id(1)

    @pl.when(i == 0)
    def _():
        r_scr[...] = r0_ref[0]

    c = RET_CHUNK
    units = [(j, h) for j in range(nsub) for h in range(nheads)]
    us = range(len(units))
    rows = lambda j: slice(j * c, (j + 1) * c)
    cols = lambda g, h: slice(g * w + h * d, g * w + (h + 1) * d)

    def load(g, j, h, is_qk):
        u = z_ref[0, rows(j), cols(g, h)]
        if g == 0:
            u = u * (d ** -0.5)
        if is_qk and use_rope:
            u = u * cos_ref[rows(j), :] + pltpu.roll(u, d // 2, axis=1) * sin_ref[rows(j), :]
        return u

    q = [load(0, j, h, True) for j, h in units]
    k = [load(1, j, h, True) for j, h in units]
    v = [load(2, j, h, False) for j, h in units]
    s = [_dot1(q[u], k[u], _NT) * din_ref[units[u][1]] for u in us]
    sv = [_dot1(s[u], v[u]) for u in us]
    kv = [_dot1(k[u] * zeta_ref[units[u][1]], v[u], _TN) for u in us]

    ys = [None] * len(units)
    state = [r_scr[h] for h in range(nheads)]
    for j in (reversed(range(nsub)) if reverse else range(nsub)):
        for h in range(nheads):
            u = j * nheads + h
            ys[u] = sv[u] + _dot1(q[u], state[h]) * xi_ref[h]
            state[h] = state[h] * g_ref[h] + kv[u]
    for h in range(nheads):
        r_scr[h] = state[h]

    for u, (j, h) in enumerate(units):
        y = ys[u]
        if reverse:
            y = yf_ref[0, rows(j), cols(0, h)] + y
            dev = y - jnp.mean(y, axis=-1, keepdims=True)
            yn = dev * lax.rsqrt(jnp.mean(dev * dev, axis=-1, keepdims=True) + NORM_EPS)
            gt = z_ref[0, rows(j), cols(3, h)]
            y = gt * jax.nn.sigmoid(gt) * (yn * ng_ref[:, cols(0, h)])
        y_ref[0, rows(j), cols(0, h)] = y

    @pl.when(i == pl.num_programs(1) - 1)
    def _():
        rT_ref[0] = r_scr[...]


def _ret_tables(log_gamma, reverse):
    c, d = RET_CHUNK, RET_HEAD_DIM
    idx = jnp.arange(c, dtype=F32)
    lg = log_gamma[:, None, None]
    if reverse:
        diff = idx[None, :] - idx[:, None]
        keep = diff > 0
        xi = jnp.exp(log_gamma[:, None] * (c - idx))
        zeta = jnp.exp(log_gamma[:, None] * idx)
    else:
        diff = idx[:, None] - idx[None, :]
        keep = diff >= 0
        xi = jnp.exp(log_gamma[:, None] * (idx + 1.0))
        zeta = jnp.exp(log_gamma[:, None] * (c - 1.0 - idx))
    din = jnp.where(keep, jnp.exp(lg * jnp.maximum(diff, 0.0)), 0.0)
    nh = log_gamma.shape[0]
    xi = jnp.broadcast_to(xi[:, :, None], (nh, c, d))
    zeta = jnp.broadcast_to(zeta[:, :, None], (nh, c, d))
    g = jnp.broadcast_to(jnp.exp(log_gamma * c)[:, None, None], (nh, d, d))
    return din, xi, zeta, g


def _ret_scan(z, rope, r0, log_gamma, reverse, y_fwd=None, norm_g=None):
    bsz, t, zw = z.shape
    d = RET_HEAD_DIM
    nh = zw // (4 * d)
    w = nh * d
    c = RET_CHUNK
    assert t % c == 0
    per_step = RET_CHUNKS_PER_STEP[1 if reverse else 0]
    nsub = per_step if (t // c) % per_step == 0 else 1
    cb = nsub * c
    nc = t // cb
    din, xi, zeta, g = _ret_tables(log_gamma, reverse)
    pos = (lambda i: nc - 1 - i) if reverse else (lambda i: i)
    tok = lambda bi, i: (bi, pos(i), 0)
    y_spec = pl.BlockSpec((1, cb, w), tok)
    st_spec = pl.BlockSpec((1, nh, d, d), lambda bi, i: (bi, 0, 0, 0))
    tab_spec = lambda n: pl.BlockSpec((nh, n, d), lambda bi, i: (0, 0, 0))
    args = [z]
    in_specs = [pl.BlockSpec((1, cb, zw), tok)]
    if rope is not None:
        args += list(rope)
        in_specs += [pl.BlockSpec((cb, d), lambda bi, i: (pos(i), 0))] * 2
    args += [din, xi, zeta, g, r0]
    in_specs += [tab_spec(c), tab_spec(c), tab_spec(c), tab_spec(d), st_spec]
    if reverse:
        args += [y_fwd, norm_g.reshape(1, w)]
        in_specs += [y_spec, pl.BlockSpec((1, w), lambda bi, i: (0, 0))]
    return pl.pallas_call(
        functools.partial(_ret_kernel, nheads=nh, reverse=reverse, use_rope=rope is not None, nsub=nsub),
        grid=(bsz, nc),
        in_specs=in_specs,
        out_specs=[y_spec, st_spec],
        out_shape=[jax.ShapeDtypeStruct((bsz, t, w), F32),
                   jax.ShapeDtypeStruct((bsz, nh, d, d), F32)],
        scratch_shapes=[pltpu.VMEM((nh, d, d), F32)],
        compiler_params=pltpu.CompilerParams(
            dimension_semantics=("parallel", "arbitrary"),
            vmem_limit_bytes=VMEM_LIMIT_BYTES),
        name="ret_bwd" if reverse else "ret_fwd",
    )(*args)


def _rwkv_mix(z, p, s0_f, s0_b, want_out):
    y_f, feat, s_f = _wkv_scan(z, p, 0, s0_f)
    out, s_b = _wkv_scan(feat, p, 1, s0_b, y_f)
    return (out if want_out else None), s_f, s_b


def _ret_mix(z, rope, r0_f, r0_b, lg_f, lg_b, ret_norm_g, want_out):
    y_f, r_f = _ret_scan(z, rope, r0_f, lg_f, False)
    out, r_b = _ret_scan(z, rope, r0_b, lg_b, True, y_f, ret_norm_g)
    return (out if want_out else None), r_f, r_b


def kernel(x, c, ctx, c_ctx, mod_w, mod_b, norm1_g, norm2_g, w_in, tshift_mu, w0, w_up, a0, a_up, g_up,
           k_k, k_a, r_k, lnx_g, lnx_b, conv_w, ret_norm_g, w_out, ffn_up, ffn_conv, ffn_down,
           final_norm_g):
    bsz, n, dm = x.shape
    depth = mod_w.shape[0]
    rwkv_w = k_k.shape[1]
    rwkv_cols = tshift_mu.shape[1]
    conv_cols = 3 * conv_w.shape[1]
    ret_heads = ret_norm_g.shape[1] // RET_HEAD_DIM
    npairs = rwkv_w // (2 * RWKV_HEAD_DIM)

    rows = n // GRID_W
    row = jnp.repeat(jnp.arange(rows, dtype=F32), GRID_W)
    col = jnp.tile(jnp.arange(GRID_W, dtype=F32), rows)
    pairs = RET_HEAD_DIM // 4
    freqs = 1.0 / (ROPE_BASE ** (jnp.arange(pairs, dtype=F32) / pairs))
    ang = jnp.concatenate([row[:, None] * freqs, col[:, None] * freqs], axis=-1)
    cos, sin = jnp.cos(ang), jnp.sin(ang)
    rope = (jnp.concatenate([cos, cos], axis=-1), jnp.concatenate([-sin, sin], axis=-1))
    hidx = jnp.arange(ret_heads, dtype=F32)
    lg_f = jnp.log1p(-jnp.exp2(-5.0 - hidx))
    lg_b = jnp.log1p(-jnp.exp2(-5.5 - hidx))
    s_zero = jnp.zeros((bsz, npairs, 2 * RWKV_HEAD_DIM, 2 * RWKV_HEAD_DIM), F32)
    r_zero = jnp.zeros((bsz, ret_heads, RET_HEAD_DIM, RET_HEAD_DIM), F32)

    xl = x.astype(F32)
    xc = ctx.astype(F32)
    silu_all = jnp.concatenate([jax.nn.silu(c.astype(F32)), jax.nn.silu(c_ctx.astype(F32))[None]], axis=0)

    w_in_b, w_out_b, up_b, down_b = (w.astype(BF16) for w in (w_in, w_out, ffn_up, ffn_down))
    widths = (rwkv_cols, conv_cols, w_in.shape[2] - rwkv_cols - conv_cols)
    bc = lambda m: jnp.broadcast_to(m, (bsz, 1, dm))

    for l in range(depth):
        last = l == depth - 1
        mod = _mm_rows(silu_all, mod_w, l) + mod_b[l]
        ml = [m[:, None, :] for m in jnp.split(mod[:bsz], 6, axis=-1)]
        mc = jnp.split(mod[bsz], 6, axis=-1)
        rwkv_p = (tshift_mu[l], w0[l], w_up[l], a0[l], a_up[l], g_up[l], k_k[l], k_a[l],
                  r_k[l], lnx_g[l], lnx_b[l])

        gm_c, sh_c = bc(norm1_g[l] * (1.0 + mc[1])), bc(mc[0])
        gm_l, sh_l = norm1_g[l] * (1.0 + ml[1]), ml[0]
        zc_rw, zc_cv, zc_rt = _mm_norm(xc, gm_c, sh_c, w_in_b, l, widths)
        zl_rw, zl_cv, zl_rt = _mm_norm(xl, gm_l, sh_l, w_in_b, l, widths)

        out_c_rw, s_f, s_b = _rwkv_mix(zc_rw, rwkv_p, s_zero, s_zero, not last)
        out_l_rw, _, _ = _rwkv_mix(zl_rw, rwkv_p, s_f, s_b, True)
        out_c_rt, r_f, r_b = _ret_mix(zc_rt, None, r_zero, r_zero, lg_f, lg_b, ret_norm_g[l], not last)
        out_l_rt, _, _ = _ret_mix(zl_rt, rope, r_f, r_b, lg_f, lg_b, ret_norm_g[l], True)

        xl = _mix_out(out_l_rw, zl_cv, out_l_rt, xl, ml[2], w_out_b, l, conv_w[l])
        xl = _ffn(xl, norm2_g[l] * (1.0 + ml[4]), ml[3], ml[5], up_b, l, ffn_conv[l], down_b,
                  final_norm_g if last else None)
        if not last:
            xc = _mix_out(out_c_rw, zc_cv, out_c_rt, xc, bc(mc[2]), w_out_b, l, conv_w[l])
            xc = _ffn(xc, bc(norm2_g[l] * (1.0 + mc[4])), bc(mc[3]), bc(mc[5]), up_b, l, ffn_conv[l],
                      down_b)

    return xl.astype(x.dtype)
```

```python
import functools

import jax
import jax.numpy as jnp
from jax import lax
from jax.experimental import pallas as pl
from jax.experimental.pallas import tpu as pltpu

F32 = jnp.float32
BF16 = jnp.bfloat16

GRID_W = 64
RWKV_HEAD_DIM = 64
DECAY_LORA = 64
AAA_LORA = 64
GATE_LORA = 128
RET_HEAD_DIM = 128
RET_CHUNK = 128
RET_CHUNKS_PER_STEP = (4, 2)
WKV_CHUNK = 64
WKV_CHUNKS_PER_STEP = 4
ROPE_BASE = 10000.0
NORM_EPS = 1e-6
RWKV_GN_EPS = 64e-5

VMEM_LIMIT_BYTES = 48 * 1024 * 1024
VMEM_LIMIT_IN_PROJ_BYTES = 56 * 1024 * 1024

_NT = (((1,), (1,)), ((), ()))
_TN = (((0,), (0,)), ((), ()))
_NN = (((1,), (0,)), ((), ()))


def _mm_kernel(a_ref, b_ref, o_ref):
    o_ref[...] = jnp.dot(a_ref[...].astype(BF16), b_ref[...].astype(BF16), preferred_element_type=F32)


def _pick_tile(n, candidates):
    for c in candidates:
        if n % c == 0:
            return c
    return n


def _mm_rows(a, b_all, layer):
    m, k = a.shape
    n = b_all.shape[-1]
    mp = -(-m // 8) * 8
    a = jnp.pad(a, ((0, mp - m), (0, 0)))
    tn = _pick_tile(n, (512, 384, 256, 128))
    out = pl.pallas_call(
        _mm_kernel,
        grid=(n // tn,),
        in_specs=[pl.BlockSpec((mp, k), lambda j: (0, 0)),
                  pl.BlockSpec((None, k, tn), lambda j: (layer, 0, j))],
        out_specs=pl.BlockSpec((mp, tn), lambda j: (0, j)),
        out_shape=jax.ShapeDtypeStruct((mp, n), F32),
        compiler_params=pltpu.CompilerParams(
            dimension_semantics=("parallel",),
            vmem_limit_bytes=VMEM_LIMIT_BYTES),
        name="mm",
    )(a, b_all)
    return out[:m]


def _resident(shape, layer=None):
    if layer is None:
        return pl.BlockSpec(shape, lambda *_: (0,) * len(shape), pipeline_mode=pl.Buffered(1))
    return pl.BlockSpec((None,) + tuple(shape), lambda *_: (layer,) + (0,) * len(shape),
                        pipeline_mode=pl.Buffered(1))


def _mm_norm_kernel(x_ref, gm_ref, sh_ref, w_ref, *refs, rope_cols):
    if rope_cols:
        cos_ref, sin_ref, *o_refs = refs
    else:
        o_refs = refs
    xb = x_ref[0]
    ms = jnp.mean(xb * xb, axis=-1, keepdims=True)
    h = (xb * lax.rsqrt(ms + NORM_EPS) * gm_ref[0] + sh_ref[0]).astype(BF16)
    start = 0
    for o_ref in o_refs:
        n = o_ref.shape[-1]
        z = jnp.dot(h, w_ref[:, start:start + n], preferred_element_type=F32)
        start += n
        if rope_cols and o_ref is o_refs[-1]:
            d = RET_HEAD_DIM
            for c0 in range(0, rope_cols, d):
                u = z[:, c0:c0 + d]
                o_ref[0, :, c0:c0 + d] = u * cos_ref[...] + pltpu.roll(u, d // 2, axis=1) * sin_ref[...]
            o_ref[0, :, rope_cols:] = z[:, rope_cols:]
        else:
            o_ref[0] = z


def _mm_norm(x, gmod, shift, w_all, layer, widths, rope=None, rope_cols=0):
    bsz, t, dm = x.shape
    assert sum(widths) == w_all.shape[2]
    tm = _pick_tile(t, (256, 128))
    vec_spec = pl.BlockSpec((1, 1, dm), lambda b, i: (b, 0, 0))
    in_specs = [pl.BlockSpec((1, tm, dm), lambda b, i: (b, i, 0)), vec_spec, vec_spec,
                _resident(w_all.shape[1:], layer)]
    args = [x, gmod, shift, w_all]
    if rope is not None:
        in_specs += [pl.BlockSpec((tm, RET_HEAD_DIM), lambda b, i: (i, 0))] * 2
        args += list(rope)
    return pl.pallas_call(
        functools.partial(_mm_norm_kernel, rope_cols=rope_cols if rope is not None else 0),
        grid=(bsz, t // tm),
        in_specs=in_specs,
        out_specs=[pl.BlockSpec((1, tm, n), lambda b, i: (b, i, 0)) for n in widths],
        out_shape=[jax.ShapeDtypeStruct((bsz, t, n), F32) for n in widths],
        compiler_params=pltpu.CompilerParams(
            dimension_semantics=("parallel", "parallel"),
            vmem_limit_bytes=VMEM_LIMIT_IN_PROJ_BYTES),
        name="mm_norm",
    )(*args)


def _mix_out_kernel(rw_ref, cv_ref, cvp_ref, cvn_ref, rt_ref, x_ref, gate_ref, cw_ref, w_ref, o_ref, *, tm):
    i = pl.program_id(1)
    cw = cv_ref.shape[-1] // 3
    zc = cv_ref[0]
    gch = zc[:, cw:2 * cw] * zc[:, 2 * cw:]
    before = jnp.where(i == 0, 0.0, cvp_ref[0, 7:8, cw:2 * cw] * cvp_ref[0, 7:8, 2 * cw:])
    after = jnp.where(i == pl.num_programs(1) - 1, 0.0,
                      cvn_ref[0, 0:1, cw:2 * cw] * cvn_ref[0, 0:1, 2 * cw:])
    trow = lax.broadcasted_iota(jnp.int32, (tm, 1), 0)
    prev = jnp.where(trow == 0, before, pltpu.roll(gch, 1, axis=0))
    nxt = jnp.where(trow == tm - 1, after, pltpu.roll(gch, tm - 1, axis=0))
    conv = zc[:, :cw] * (prev * cw_ref[0:1] + gch * cw_ref[1:2] + nxt * cw_ref[2:3])
    mix = jnp.concatenate([rw_ref[0].astype(BF16), conv.astype(BF16), rt_ref[0].astype(BF16)], axis=1)
    o_ref[0] = x_ref[0] + gate_ref[0] * jnp.dot(mix, w_ref[...], preferred_element_type=F32)


def _mix_out(out_rw, z_cv, out_rt, x, gate, w_out_all, layer, conv_w):
    bsz, t, dm = x.shape
    rw, cvw, rtw = out_rw.shape[-1], z_cv.shape[-1], out_rt.shape[-1]
    tm = _pick_tile(t, (512, 256, 128))
    per = tm // 8
    nhb = t // 8
    return pl.pallas_call(
        functools.partial(_mix_out_kernel, tm=tm),
        grid=(bsz, t // tm),
        in_specs=[
            pl.BlockSpec((1, tm, rw), lambda b, i: (b, i, 0)),
            pl.BlockSpec((1, tm, cvw), lambda b, i: (b, i, 0)),
            pl.BlockSpec((1, 8, cvw), lambda b, i: (b, jnp.maximum(i * per - 1, 0), 0)),
            pl.BlockSpec((1, 8, cvw), lambda b, i: (b, jnp.minimum((i + 1) * per, nhb - 1), 0)),
            pl.BlockSpec((1, tm, rtw), lambda b, i: (b, i, 0)),
            pl.BlockSpec((1, tm, dm), lambda b, i: (b, i, 0)),
            pl.BlockSpec((1, 1, dm), lambda b, i: (b, 0, 0)),
            _resident((3, cvw // 3)),
            _resident((rw + cvw // 3 + rtw, dm), layer),
        ],
        out_specs=pl.BlockSpec((1, tm, dm), lambda b, i: (b, i, 0)),
        out_shape=jax.ShapeDtypeStruct((bsz, t, dm), F32),
        compiler_params=pltpu.CompilerParams(
            dimension_semantics=("parallel", "parallel"),
            vmem_limit_bytes=VMEM_LIMIT_BYTES),
        name="mix_out",
    )(out_rw, z_cv, z_cv, z_cv, out_rt, x, gate, conv_w.T, w_out_all)


FFN_HALO = 16

def _ffn_kernel(*refs, tm, final):
    if final:
        (x_ref, xp_ref, xn_ref, gm_ref, sh_ref, gate_ref, upg_ref, upv_ref, cwg_ref, cwv_ref, dn_ref,
         fg_ref, o_ref, h_scr, acc_scr) = refs
    else:
        (x_ref, xp_ref, xn_ref, gm_ref, sh_ref, gate_ref, upg_ref, upv_ref, cwg_ref, cwv_ref, dn_ref,
         o_ref, h_scr, acc_scr) = refs
    i = pl.program_id(1)
    j = pl.program_id(2)
    halo = FFN_HALO
    rows = tm + 2 * halo

    @pl.when(j == 0)
    def _():
        def modnorm(xb):
            ms = jnp.mean(xb * xb, axis=-1, keepdims=True)
            return xb * lax.rsqrt(ms + NORM_EPS) * gm_ref[0] + sh_ref[0]

        hp = jnp.where(i == 0, 0.0, modnorm(xp_ref[0]))
        hn = jnp.where(i == pl.num_programs(1) - 1, 0.0, modnorm(xn_ref[0]))
        h_scr[0:halo] = hp.astype(BF16)
        h_scr[halo:halo + tm] = modnorm(x_ref[0]).astype(BF16)
        h_scr[halo + tm:rows] = hn.astype(BF16)
        acc_scr[...] = jnp.zeros_like(acc_scr)

    h = h_scr[...]

    def conv_half(up_ref, cw_ref):
        pm = jnp.dot(h, up_ref[...], preferred_element_type=F32)
        cw = cw_ref[...]
        u = (pltpu.roll(pm, 1, axis=0) * cw[0:1] + pm * cw[1:2]
             + pltpu.roll(pm, rows - 1, axis=0) * cw[2:3])
        return u[halo:halo + tm]

    gt = conv_half(upg_ref, cwg_ref)
    vl = conv_half(upv_ref, cwv_ref)
    act = gt * jax.nn.sigmoid(gt) * vl
    acc_scr[...] += jnp.dot(act.astype(BF16), dn_ref[...], preferred_element_type=F32)

    @pl.when(j == pl.num_programs(2) - 1)
    def _():
        out = x_ref[0] + gate_ref[0] * acc_scr[...]
        if final:
            ms = jnp.mean(out * out, axis=-1, keepdims=True)
            out = out * lax.rsqrt(ms + NORM_EPS) * fg_ref[...]
        o_ref[0] = out


def _ffn(x, gmod, shift, gate, up, layer, cw, down, final_g=None):
    bsz, t, dm = x.shape
    f = down.shape[1]
    tm = _pick_tile(t, (512, 256, 128))
    tf = _pick_tile(f, (512, 256, 128))
    halo = FFN_HALO
    nf = f // tf
    per = tm // halo
    nhb = t // halo
    cwt = cw.T
    final = final_g is not None
    vec_spec = pl.BlockSpec((1, 1, dm), lambda b, i, j: (b, 0, 0))
    in_specs = [
        pl.BlockSpec((1, tm, dm), lambda b, i, j: (b, i, 0)),
        pl.BlockSpec((1, halo, dm), lambda b, i, j: (b, jnp.maximum(i * per - 1, 0), 0)),
        pl.BlockSpec((1, halo, dm), lambda b, i, j: (b, jnp.minimum((i + 1) * per, nhb - 1), 0)),
        vec_spec, vec_spec, vec_spec,
        pl.BlockSpec((None, dm, tf), lambda b, i, j: (layer, 0, j)),
        pl.BlockSpec((None, dm, tf), lambda b, i, j: (layer, 0, nf + j)),
        pl.BlockSpec((3, tf), lambda b, i, j: (0, j)),
        pl.BlockSpec((3, tf), lambda b, i, j: (0, nf + j)),
        pl.BlockSpec((None, tf, dm), lambda b, i, j: (layer, j, 0)),
    ]
    args = [x, x, x, gmod, shift, gate, up, up, cwt, cwt, down]
    if final:
        in_specs.append(pl.BlockSpec((1, dm), lambda b, i, j: (0, 0)))
        args.append(final_g.reshape(1, dm))
    return pl.pallas_call(
        functools.partial(_ffn_kernel, tm=tm, final=final),
        grid=(bsz, t // tm, nf),
        in_specs=in_specs,
        out_specs=pl.BlockSpec((1, tm, dm), lambda b, i, j: (b, i, 0)),
        out_shape=jax.ShapeDtypeStruct((bsz, t, dm), F32),
        scratch_shapes=[pltpu.VMEM((tm + 2 * halo, dm), BF16), pltpu.VMEM((tm, dm), F32)],
        compiler_params=pltpu.CompilerParams(
            dimension_semantics=("parallel", "parallel", "arbitrary"),
            vmem_limit_bytes=VMEM_LIMIT_BYTES),
        name="ffn",
    )(*args)


def _split_bf16(x):
    hi = x.astype(BF16)
    lo = (x - hi.astype(F32)).astype(BF16)
    return hi, lo


def _dot1(a, b, dn=_NN):
    return lax.dot_general(a.astype(BF16), b.astype(BF16), dn, preferred_element_type=F32)


def _wkv_kernel(*refs, reverse, npairs, chunk, nsub):
    if reverse:
        (feat_ref, ka_ref, w0_ref, a0_ref, lora_ref, s0_ref,
         yf_ref, a0f_ref, gup_ref, rk_ref, lng_ref, lnb_ref, y_ref, sT_ref, s_scr) = refs
    else:
        (z_ref, zp_ref, zn_ref, mu_ref, kk_ref, ka_ref, w0_ref, a0_ref, lora_ref, s0_ref,
         y_ref, feat_ref, sT_ref, s_scr) = refs
    c = chunk
    cb = nsub * c
    w2 = 2 * RWKV_HEAD_DIM
    rw = npairs * w2
    i = pl.program_id(1)
    nc = pl.num_programs(1)
    ci = nc - 1 - i if reverse else i

    @pl.when(i == 0)
    def _():
        s_scr[...] = s0_ref[0]

    row = lax.broadcasted_iota(jnp.int32, (cb, cb), 0)
    col = lax.broadcasted_iota(jnp.int32, (cb, cb), 1)
    same_chunk = row // c == col // c
    cum = (same_chunk & (row <= col if reverse else row >= col)).astype(BF16)
    last = 0 if reverse else c - 1
    first_head = lax.broadcasted_iota(jnp.int32, (c, w2), 1) < RWKV_HEAD_DIM
    first_head2 = jnp.concatenate([first_head, first_head], axis=1)
    brow = lax.broadcasted_iota(jnp.int32, (w2, w2), 0) < RWKV_HEAD_DIM
    bcol = lax.broadcasted_iota(jnp.int32, (w2, w2), 1) < RWKV_HEAD_DIM
    same_head = brow == bcol
    head_ones = same_head.astype(BF16)
    trow = lax.broadcasted_iota(jnp.int32, (c, 2 * c), 0)
    tcol = lax.broadcasted_iota(jnp.int32, (c, 2 * c), 1)
    tcol = jnp.where(tcol < c, tcol, tcol - c)
    eye_pair = jnp.where(trow == tcol, 1.0, 0.0)
    if reverse:
        strict, incl = trow < tcol, trow <= tcol
    else:
        strict, incl = trow > tcol, trow >= tcol
    pairs = range(npairs)
    sls = [slice(q * w2, (q + 1) * w2) for q in pairs]

    def head_sum(u):
        hi, lo = _split_bf16(u)
        ones2 = jnp.concatenate([head_ones, head_ones], axis=0)
        return jnp.concatenate(
            [jnp.dot(jnp.concatenate([hi[:, sl], lo[:, sl]], axis=1), ones2, preferred_element_type=F32)
             for sl in sls], axis=1)

    zw = 3 * rw + 2 * w2
    if reverse:
        z = feat_ref[0, :, :zw]
        kk = feat_ref[0, :, zw:]
    else:
        z = z_ref[0]
        trow = lax.broadcasted_iota(jnp.int32, (cb, 1), 0)
        z_before = jnp.where(ci == 0, 0.0, zp_ref[0, 7:8, :])
        z_after = jnp.where(ci == nc - 1, 0.0, zn_ref[0, 0:1, :])
        zprev = jnp.where(trow == 0, z_before, pltpu.roll(z, 1, axis=0))
        znext = jnp.where(trow == cb - 1, z_after, pltpu.roll(z, cb - 1, axis=0))
        z = z + (0.5 * (zprev + znext) - z) * mu_ref[...]
    r = z[:, :rw]
    k = z[:, rw:2 * rw]
    v_all = z[:, 2 * rw:3 * rw]
    wa = z[:, 3 * rw:3 * rw + w2]
    gd = z[:, 3 * rw + w2:]

    if not reverse:
        kkr = k * kk_ref[...]
        kk = kkr / jnp.maximum(jnp.sqrt(head_sum(kkr * kkr)), 1e-12)
        feat_ref[0, :, :zw] = z
        feat_ref[0, :, zw:] = kk
    decay_lane = lax.broadcasted_iota(jnp.int32, (cb, w2), 1) < DECAY_LORA
    lora = jnp.dot(jnp.where(decay_lane, jnp.tanh(wa), wa).astype(BF16), lora_ref[...],
                   preferred_element_type=F32)
    xw = w0_ref[...] + lora[:, :rw]
    w_log = -(jnp.maximum(-xw, 0.0) + jnp.log1p(jnp.exp(-jnp.abs(xw)))) - 0.5
    lw = -jnp.exp(w_log)
    rate = jax.nn.sigmoid(a0_ref[...] + lora[:, rw:2 * rw])
    key = k * (1.0 + (rate - 1.0) * ka_ref[...])

    lw_hi, lw_lo = _split_bf16(lw)
    cl = jnp.dot(jnp.concatenate([cum, cum], axis=1), jnp.concatenate([lw_hi, lw_lo], axis=0),
                 preferred_element_type=F32)
    p = jnp.exp(cl)
    pinv = jnp.exp(-cl)
    at_all = -kk * jnp.exp(cl - lw)
    bt_all = kk * rate * pinv
    kt_all = key * pinv
    rt_all = r * p

    def stack(u):
        m = first_head if u.shape[1] == w2 else first_head2
        zero = jnp.zeros_like(u)
        return jnp.concatenate([jnp.where(m, u, zero), jnp.where(m, zero, u)], axis=0)

    def dotb(a, b, dn=_NN):
        return lax.dot_general(a, b, dn, preferred_element_type=F32)

    def pdot(mat, u):
        return dotb(mat, stack(u))

    units = [(j, q) for j in range(nsub) for q in pairs]
    cut = lambda u: [u[j * c:(j + 1) * c, sls[q]] for j, q in units]
    rt = cut(rt_all)
    at, bt, kt, rtb, v = [cut(u.astype(BF16)) for u in (at_all, bt_all, kt_all, rt_all, v_all)]
    us = range(len(units))

    a_ab, a_ak, m_rb, m_rk = [], [], [], []
    for u in us:
        lhs = jnp.concatenate([at[u], rtb[u]], axis=0)
        mbk = dotb(lhs, jnp.concatenate([stack(bt[u]), stack(kt[u])], axis=0), _NT)
        a_ab.append(jnp.where(strict, mbk[:c, :2 * c], 0.0).astype(BF16))
        m_rb.append(jnp.where(incl, mbk[c:, :2 * c], 0.0).astype(BF16))
        a_ak.append(jnp.where(strict, mbk[:c, 2 * c:], 0.0).astype(BF16))
        m_rk.append(jnp.where(incl, mbk[c:, 2 * c:], 0.0).astype(BF16))

    tinv = [eye_pair + a_ab[u].astype(F32) for u in us]
    ak = [pdot(a_ab[u], a_ab[u]).astype(BF16) for u in us]
    nsteps = max(1, (c - 1).bit_length()) - 1
    for s in range(nsteps):
        if s + 1 < nsteps:
            both = [pdot(ak[u], jnp.concatenate([ak[u], tinv[u].astype(BF16)], axis=1)) for u in us]
            ak = [both[u][:, :2 * c].astype(BF16) for u in us]
            tinv = [tinv[u] + both[u][:, 2 * c:] for u in us]
        else:
            tinv = [tinv[u] + pdot(ak[u], tinv[u].astype(BF16)) for u in us]
    xu0 = [pdot(a_ak[u], v[u]).astype(BF16) for u in us]
    x = [pdot(tinv[u].astype(BF16), jnp.concatenate([at[u], xu0[u]], axis=1)).astype(BF16)
         for u in us]

    zero_v = [jnp.zeros_like(v[u]) for u in us]
    xv = [jnp.concatenate([zero_v[u], v[u]], axis=1) for u in us]
    ry = [dotb(jnp.concatenate([m_rb[u], m_rk[u]], axis=1),
               jnp.concatenate([stack(x[u]), stack(xv[u])], axis=0)) for u in us]
    rbar = [rt[u] + ry[u][:, :w2] for u in us]
    y0 = [ry[u][:, w2:] for u in us]
    gh = [dotb(jnp.concatenate([x[u], xv[u]], axis=0), jnp.concatenate([bt[u], kt[u]], axis=0), _TN)
          for u in us]
    g = [jnp.where(same_head, gh[u][:w2], 0.0) for u in us]
    hh = [jnp.where(same_head, gh[u][w2:], 0.0) for u in us]

    ys = [None] * len(units)
    state = [s_scr[q] for q in pairs]
    for j in (reversed(range(nsub)) if reverse else range(nsub)):
        for q in pairs:
            u = j * npairs + q
            s0 = state[q]
            ys[u] = _dot1(rbar[u], s0, _NT) + y0[u]
            state[q] = (s0 + _dot1(s0, g[u]) + hh[u]) * p[j * c + last:j * c + last + 1, sls[q]]
    for q in pairs:
        s_scr[q] = state[q]
    y = jnp.concatenate([jnp.concatenate(ys[j * npairs:(j + 1) * npairs], axis=1)
                         for j in range(nsub)], axis=0)

    if reverse:
        y = yf_ref[0] + y
        inv_n = 1.0 / RWKV_HEAD_DIM
        rate_f = jax.nn.sigmoid(a0f_ref[...] + lora[:, 2 * rw:])
        k_mid = 0.5 * (k * (1.0 + (rate_f - 1.0) * ka_ref[...]) + key)
        sums = head_sum(jnp.concatenate([y, r * k_mid * rk_ref[...]], axis=0))
        dev = y - sums[:cb] * inv_n
        yn = dev * lax.rsqrt(head_sum(dev * dev) * inv_n + RWKV_GN_EPS)
        bonus = sums[cb:] * v_all
        gate = jnp.dot(jax.nn.sigmoid(gd).astype(BF16), gup_ref[...], preferred_element_type=F32)
        y = (yn * lng_ref[...] + lnb_ref[...] + bonus) * gate
    y_ref[0] = y

    @pl.when(i == pl.num_programs(1) - 1)
    def _():
        sT_ref[0] = s_scr[...]


def _wkv_scan(z, p, d, s0, y_fwd=None):
    mu, w0, w_up, a0, a_up, g_up, k_k, k_a, r_k, lnx_g, lnx_b = p
    bsz, t, zw = z.shape
    w = k_k.shape[0]
    w2 = 2 * RWKV_HEAD_DIM
    npairs = w // w2
    c = WKV_CHUNK
    assert c == RWKV_HEAD_DIM and t % c == 0 and w % w2 == 0
    nsub = WKV_CHUNKS_PER_STEP if (t // c) % WKV_CHUNKS_PER_STEP == 0 else 1
    cb = nsub * c
    nc = t // cb
    reverse = d == 1
    per = cb // 8
    nhb = t // 8
    pos = (lambda i: nc - 1 - i) if reverse else (lambda i: i)
    tok = lambda bi, i: (bi, pos(i), 0)
    row1 = lambda u: u.reshape(1, -1)
    zero = jnp.zeros((DECAY_LORA, w), F32)
    blocks = [jnp.concatenate([w_up[d], zero], axis=0), jnp.concatenate([zero, a_up[d]], axis=0)]
    if reverse:
        blocks.append(jnp.concatenate([zero, a_up[0]], axis=0))
    lora_w = jnp.concatenate(blocks, axis=1).astype(BF16)
    full = lambda arr: pl.BlockSpec(arr.shape, lambda bi, i: (0, 0))
    st_spec = pl.BlockSpec((1, npairs, w2, w2), lambda bi, i: (bi, 0, 0, 0))
    y_spec = pl.BlockSpec((1, cb, w), tok)
    common = [row1(k_a), row1(w0[d]), row1(a0[d]), lora_w]
    out_specs = [y_spec, st_spec]
    out_shape = [jax.ShapeDtypeStruct((bsz, t, w), F32), jax.ShapeDtypeStruct((bsz, npairs, w2, w2), F32)]
    if reverse:
        extra = [row1(a0[0]), g_up.astype(BF16), row1(r_k), row1(lnx_g), row1(lnx_b)]
        args = [z] + common + [s0, y_fwd] + extra
        in_specs = ([pl.BlockSpec((1, cb, zw), tok)] + [full(a) for a in common] + [st_spec, y_spec]
                    + [full(a) for a in extra])
    else:
        head = [row1(mu), row1(k_k)]
        args = [z, z, z] + head + common + [s0]
        in_specs = [
            pl.BlockSpec((1, cb, zw), tok),
            pl.BlockSpec((1, 8, zw), lambda bi, i: (bi, jnp.maximum(pos(i) * per - 1, 0), 0)),
            pl.BlockSpec((1, 8, zw), lambda bi, i: (bi, jnp.minimum((pos(i) + 1) * per, nhb - 1), 0)),
        ] + [full(a) for a in head + common] + [st_spec]
        out_specs.insert(1, pl.BlockSpec((1, cb, zw + w), tok))
        out_shape.insert(1, jax.ShapeDtypeStruct((bsz, t, zw + w), F32))
    return pl.pallas_call(
        functools.partial(_wkv_kernel, reverse=reverse, npairs=npairs, chunk=c, nsub=nsub),
        grid=(bsz, nc),
        in_specs=in_specs,
        out_specs=out_specs,
        out_shape=out_shape,
        scratch_shapes=[pltpu.VMEM((npairs, w2, w2), F32)],
        compiler_params=pltpu.CompilerParams(
            dimension_semantics=("parallel", "arbitrary"),
            vmem_limit_bytes=VMEM_LIMIT_BYTES),
        name="wkv_bwd" if reverse else "wkv_fwd",
    )(*args)


def _ret_kernel(*refs, nheads, reverse, nsub):
    refs = list(refs)
    z_ref = refs.pop(0)
    din_ref, xi_ref, zeta_ref, g_ref, r0_ref = refs[:5]
    refs = refs[5:]
    yf_ref, ng_ref = (refs.pop(0), refs.pop(0)) if reverse else (None, None)
    y_ref, rT_ref, r_scr = refs
    d = RET_HEAD_DIM
    w = nheads * d
    i = pl.program_id(1)

    @pl.when(i == 0)
    def _():
        r_scr[...] = r0_ref[0]

    c = RET_CHUNK
    units = [(j, h) for j in range(nsub) for h in range(nheads)]
    us = range(len(units))
    rows = lambda j: slice(j * c, (j + 1) * c)
    cols = lambda g, h: slice(g * w + h * d, g * w + (h + 1) * d)

    q = [z_ref[0, rows(j), cols(0, h)] * (d ** -0.5) for j, h in units]
    k = [z_ref[0, rows(j), cols(1, h)] for j, h in units]
    v = [z_ref[0, rows(j), cols(2, h)] for j, h in units]
    s = [_dot1(q[u], k[u], _NT) * din_ref[units[u][1]] for u in us]
    sv = [_dot1(s[u], v[u]) for u in us]
    kv = [_dot1(k[u] * zeta_ref[units[u][1]], v[u], _TN) for u in us]

    ys = [None] * len(units)
    state = [r_scr[h] for h in range(nheads)]
    for j in (reversed(range(nsub)) if reverse else range(nsub)):
        for h in range(nheads):
            u = j * nheads + h
            ys[u] = sv[u] + _dot1(q[u], state[h]) * xi_ref[h]
            state[h] = state[h] * g_ref[h] + kv[u]
    for h in range(nheads):
        r_scr[h] = state[h]

    for u, (j, h) in enumerate(units):
        y = ys[u]
        if reverse:
            y = yf_ref[0, rows(j), cols(0, h)] + y
            dev = y - jnp.mean(y, axis=-1, keepdims=True)
            yn = dev * lax.rsqrt(jnp.mean(dev * dev, axis=-1, keepdims=True) + NORM_EPS)
            gt = z_ref[0, rows(j), cols(3, h)]
            y = gt * jax.nn.sigmoid(gt) * (yn * ng_ref[:, cols(0, h)])
        y_ref[0, rows(j), cols(0, h)] = y

    @pl.when(i == pl.num_programs(1) - 1)
    def _():
        rT_ref[0] = r_scr[...]


def _ret_tables(log_gamma, reverse):
    c, d = RET_CHUNK, RET_HEAD_DIM
    idx = jnp.arange(c, dtype=F32)
    lg = log_gamma[:, None, None]
    if reverse:
        diff = idx[None, :] - idx[:, None]
        keep = diff > 0
        xi = jnp.exp(log_gamma[:, None] * (c - idx))
        zeta = jnp.exp(log_gamma[:, None] * idx)
    else:
        diff = idx[:, None] - idx[None, :]
        keep = diff >= 0
        xi = jnp.exp(log_gamma[:, None] * (idx + 1.0))
        zeta = jnp.exp(log_gamma[:, None] * (c - 1.0 - idx))
    din = jnp.where(keep, jnp.exp(lg * jnp.maximum(diff, 0.0)), 0.0)
    nh = log_gamma.shape[0]
    xi = jnp.broadcast_to(xi[:, :, None], (nh, c, d))
    zeta = jnp.broadcast_to(zeta[:, :, None], (nh, c, d))
    g = jnp.broadcast_to(jnp.exp(log_gamma * c)[:, None, None], (nh, d, d))
    return din, xi, zeta, g


def _ret_scan(z, r0, log_gamma, reverse, y_fwd=None, norm_g=None):
    bsz, t, zw = z.shape
    d = RET_HEAD_DIM
    nh = zw // (4 * d)
    w = nh * d
    c = RET_CHUNK
    assert t % c == 0
    per_step = RET_CHUNKS_PER_STEP[1 if reverse else 0]
    nsub = per_step if (t // c) % per_step == 0 else 1
    cb = nsub * c
    nc = t // cb
    din, xi, zeta, g = _ret_tables(log_gamma, reverse)
    pos = (lambda i: nc - 1 - i) if reverse else (lambda i: i)
    tok = lambda bi, i: (bi, pos(i), 0)
    y_spec = pl.BlockSpec((1, cb, w), tok)
    st_spec = pl.BlockSpec((1, nh, d, d), lambda bi, i: (bi, 0, 0, 0))
    tab_spec = lambda n: pl.BlockSpec((nh, n, d), lambda bi, i: (0, 0, 0))
    args = [z]
    in_specs = [pl.BlockSpec((1, cb, zw), tok)]
    args += [din, xi, zeta, g, r0]
    in_specs += [tab_spec(c), tab_spec(c), tab_spec(c), tab_spec(d), st_spec]
    if reverse:
        args += [y_fwd, norm_g.reshape(1, w)]
        in_specs += [y_spec, pl.BlockSpec((1, w), lambda bi, i: (0, 0))]
    return pl.pallas_call(
        functools.partial(_ret_kernel, nheads=nh, reverse=reverse, nsub=nsub),
        grid=(bsz, nc),
        in_specs=in_specs,
        out_specs=[y_spec, st_spec],
        out_shape=[jax.ShapeDtypeStruct((bsz, t, w), F32),
                   jax.ShapeDtypeStruct((bsz, nh, d, d), F32)],
        scratch_shapes=[pltpu.VMEM((nh, d, d), F32)],
        compiler_params=pltpu.CompilerParams(
            dimension_semantics=("parallel", "arbitrary"),
            vmem_limit_bytes=VMEM_LIMIT_BYTES),
        name="ret_bwd" if reverse else "ret_fwd",
    )(*args)


def _rwkv_mix(z, p, s0_f, s0_b, want_out):
    y_f, feat, s_f = _wkv_scan(z, p, 0, s0_f)
    out, s_b = _wkv_scan(feat, p, 1, s0_b, y_f)
    return (out if want_out else None), s_f, s_b


def _ret_mix(z, r0_f, r0_b, lg_f, lg_b, ret_norm_g, want_out):
    y_f, r_f = _ret_scan(z, r0_f, lg_f, False)
    out, r_b = _ret_scan(z, r0_b, lg_b, True, y_f, ret_norm_g)
    return (out if want_out else None), r_f, r_b


def kernel(x, c, ctx, c_ctx, mod_w, mod_b, norm1_g, norm2_g, w_in, tshift_mu, w0, w_up, a0, a_up, g_up,
           k_k, k_a, r_k, lnx_g, lnx_b, conv_w, ret_norm_g, w_out, ffn_up, ffn_conv, ffn_down,
           final_norm_g):
    bsz, n, dm = x.shape
    depth = mod_w.shape[0]
    rwkv_w = k_k.shape[1]
    rwkv_cols = tshift_mu.shape[1]
    conv_cols = 3 * conv_w.shape[1]
    ret_heads = ret_norm_g.shape[1] // RET_HEAD_DIM
    npairs = rwkv_w // (2 * RWKV_HEAD_DIM)

    rows = n // GRID_W
    row = jnp.repeat(jnp.arange(rows, dtype=F32), GRID_W)
    col = jnp.tile(jnp.arange(GRID_W, dtype=F32), rows)
    pairs = RET_HEAD_DIM // 4
    freqs = 1.0 / (ROPE_BASE ** (jnp.arange(pairs, dtype=F32) / pairs))
    ang = jnp.concatenate([row[:, None] * freqs, col[:, None] * freqs], axis=-1)
    cos, sin = jnp.cos(ang), jnp.sin(ang)
    rope = (jnp.concatenate([cos, cos], axis=-1), jnp.concatenate([-sin, sin], axis=-1))
    hidx = jnp.arange(ret_heads, dtype=F32)
    lg_f = jnp.log1p(-jnp.exp2(-5.0 - hidx))
    lg_b = jnp.log1p(-jnp.exp2(-5.5 - hidx))
    s_zero = jnp.zeros((bsz, npairs, 2 * RWKV_HEAD_DIM, 2 * RWKV_HEAD_DIM), F32)
    r_zero = jnp.zeros((bsz, ret_heads, RET_HEAD_DIM, RET_HEAD_DIM), F32)

    xl = x.astype(F32)
    xc = ctx.astype(F32)
    silu_all = jnp.concatenate([jax.nn.silu(c.astype(F32)), jax.nn.silu(c_ctx.astype(F32))[None]], axis=0)

    w_in_b, w_out_b, up_b, down_b = (w.astype(BF16) for w in (w_in, w_out, ffn_up, ffn_down))
    widths = (rwkv_cols, conv_cols, w_in.shape[2] - rwkv_cols - conv_cols)
    bc = lambda m: jnp.broadcast_to(m, (bsz, 1, dm))

    for l in range(depth):
        last = l == depth - 1
        mod = _mm_rows(silu_all, mod_w, l) + mod_b[l]
        ml = [m[:, None, :] for m in jnp.split(mod[:bsz], 6, axis=-1)]
        mc = jnp.split(mod[bsz], 6, axis=-1)
        rwkv_p = (tshift_mu[l], w0[l], w_up[l], a0[l], a_up[l], g_up[l], k_k[l], k_a[l],
                  r_k[l], lnx_g[l], lnx_b[l])

        gm_c, sh_c = bc(norm1_g[l] * (1.0 + mc[1])), bc(mc[0])
        gm_l, sh_l = norm1_g[l] * (1.0 + ml[1]), ml[0]
        zc_rw, zc_cv, zc_rt = _mm_norm(xc, gm_c, sh_c, w_in_b, l, widths)
        zl_rw, zl_cv, zl_rt = _mm_norm(xl, gm_l, sh_l, w_in_b, l, widths, rope,
                                       2 * ret_heads * RET_HEAD_DIM)

        out_c_rw, s_f, s_b = _rwkv_mix(zc_rw, rwkv_p, s_zero, s_zero, not last)
        out_l_rw, _, _ = _rwkv_mix(zl_rw, rwkv_p, s_f, s_b, True)
        out_c_rt, r_f, r_b = _ret_mix(zc_rt, r_zero, r_zero, lg_f, lg_b, ret_norm_g[l], not last)
        out_l_rt, _, _ = _ret_mix(zl_rt, r_f, r_b, lg_f, lg_b, ret_norm_g[l], True)

        xl = _mix_out(out_l_rw, zl_cv, out_l_rt, xl, ml[2], w_out_b, l, conv_w[l])
        xl = _ffn(xl, norm2_g[l] * (1.0 + ml[4]), ml[3], ml[5], up_b, l, ffn_conv[l], down_b,
                  final_norm_g if last else None)
        if not last:
            xc = _mix_out(out_c_rw, zc_cv, out_c_rt, xc, bc(mc[2]), w_out_b, l, conv_w[l])
            xc = _ffn(xc, bc(norm2_g[l] * (1.0 + mc[4])), bc(mc[3]), bc(mc[5]), up_b, l, ffn_conv[l],
                      down_b)

    return xl.astype(x.dtype)
```

```python
import functools

import jax
import jax.numpy as jnp
from jax import lax
from jax.experimental import pallas as pl
from jax.experimental.pallas import tpu as pltpu

F32 = jnp.float32
BF16 = jnp.bfloat16

GRID_W = 64
RWKV_HEAD_DIM = 64
DECAY_LORA = 64
AAA_LORA = 64
GATE_LORA = 128
RET_HEAD_DIM = 128
RET_CHUNK = 128
RET_CHUNKS_PER_STEP = (4, 2)
WKV_CHUNK = 64
WKV_CHUNKS_PER_STEP = 4
ROPE_BASE = 10000.0
NORM_EPS = 1e-6
RWKV_GN_EPS = 64e-5

VMEM_LIMIT_BYTES = 48 * 1024 * 1024
VMEM_LIMIT_IN_PROJ_BYTES = 56 * 1024 * 1024

_NT = (((1,), (1,)), ((), ()))
_TN = (((0,), (0,)), ((), ()))
_NN = (((1,), (0,)), ((), ()))


def _mm_kernel(a_ref, b_ref, o_ref):
    o_ref[...] = jnp.dot(a_ref[...].astype(BF16), b_ref[...].astype(BF16), preferred_element_type=F32)


def _pick_tile(n, candidates):
    for c in candidates:
        if n % c == 0:
            return c
    return n


def _mm_rows(a, b_all, layer):
    m, k = a.shape
    n = b_all.shape[-1]
    mp = -(-m // 8) * 8
    a = jnp.pad(a, ((0, mp - m), (0, 0)))
    tn = _pick_tile(n, (512, 384, 256, 128))
    out = pl.pallas_call(
        _mm_kernel,
        grid=(n // tn,),
        in_specs=[pl.BlockSpec((mp, k), lambda j: (0, 0)),
                  pl.BlockSpec((None, k, tn), lambda j: (layer, 0, j))],
        out_specs=pl.BlockSpec((mp, tn), lambda j: (0, j)),
        out_shape=jax.ShapeDtypeStruct((mp, n), F32),
        compiler_params=pltpu.CompilerParams(
            dimension_semantics=("parallel",),
            vmem_limit_bytes=VMEM_LIMIT_BYTES),
        name="mm",
    )(a, b_all)
    return out[:m]


def _resident(shape, layer=None):
    if layer is None:
        return pl.BlockSpec(shape, lambda *_: (0,) * len(shape), pipeline_mode=pl.Buffered(1))
    return pl.BlockSpec((None,) + tuple(shape), lambda *_: (layer,) + (0,) * len(shape),
                        pipeline_mode=pl.Buffered(1))


def _mm_norm_kernel(x_ref, gm_ref, sh_ref, w_ref, *refs, use_rope, ret_w):
    if use_rope:
        cos_ref, sin_ref, *o_refs = refs
    else:
        o_refs = refs
    xb = x_ref[0]
    ms = jnp.mean(xb * xb, axis=-1, keepdims=True)
    h = (xb * lax.rsqrt(ms + NORM_EPS) * gm_ref[0] + sh_ref[0]).astype(BF16)
    start = 0
    for o_ref in o_refs:
        n = o_ref.shape[-1]
        z = jnp.dot(h, w_ref[:, start:start + n], preferred_element_type=F32)
        start += n
        if o_ref is o_refs[-1]:
            d = RET_HEAD_DIM
            for c0 in range(0, 2 * ret_w, d):
                u = z[:, c0:c0 + d]
                if use_rope:
                    u = u * cos_ref[...] + pltpu.roll(u, d // 2, axis=1) * sin_ref[...]
                if c0 < ret_w:
                    u = u * (d ** -0.5)
                o_ref[0, :, c0:c0 + d] = u.astype(o_ref.dtype)
            o_ref[0, :, 2 * ret_w:] = z[:, 2 * ret_w:].astype(o_ref.dtype)
        else:
            o_ref[0] = z.astype(o_ref.dtype)


def _mm_norm(x, gmod, shift, w_all, layer, widths, dtypes, ret_w, rope=None):
    bsz, t, dm = x.shape
    assert sum(widths) == w_all.shape[2] and widths[-1] == 4 * ret_w
    tm = _pick_tile(t, (256, 128))
    vec_spec = pl.BlockSpec((1, 1, dm), lambda b, i: (b, 0, 0))
    in_specs = [pl.BlockSpec((1, tm, dm), lambda b, i: (b, i, 0)), vec_spec, vec_spec,
                _resident(w_all.shape[1:], layer)]
    args = [x, gmod, shift, w_all]
    if rope is not None:
        in_specs += [pl.BlockSpec((tm, RET_HEAD_DIM), lambda b, i: (i, 0))] * 2
        args += list(rope)
    return pl.pallas_call(
        functools.partial(_mm_norm_kernel, use_rope=rope is not None, ret_w=ret_w),
        grid=(bsz, t // tm),
        in_specs=in_specs,
        out_specs=[pl.BlockSpec((1, tm, n), lambda b, i: (b, i, 0)) for n in widths],
        out_shape=[jax.ShapeDtypeStruct((bsz, t, n), dt) for n, dt in zip(widths, dtypes)],
        compiler_params=pltpu.CompilerParams(
            dimension_semantics=("parallel", "parallel"),
            vmem_limit_bytes=VMEM_LIMIT_IN_PROJ_BYTES),
        name="mm_norm",
    )(*args)


def _mix_out_kernel(rw_ref, cv_ref, cvp_ref, cvn_ref, rt_ref, x_ref, gate_ref, cw_ref, w_ref, o_ref, *, tm):
    i = pl.program_id(1)
    cw = cv_ref.shape[-1] // 3
    hl = cvp_ref.shape[1]
    zc = cv_ref[0].astype(F32)
    gch = zc[:, cw:2 * cw] * zc[:, 2 * cw:]
    row_before = cvp_ref[0].astype(F32)[hl - 1:hl]
    row_after = cvn_ref[0].astype(F32)[0:1]
    before = jnp.where(i == 0, 0.0, row_before[:, cw:2 * cw] * row_before[:, 2 * cw:])
    after = jnp.where(i == pl.num_programs(1) - 1, 0.0, row_after[:, cw:2 * cw] * row_after[:, 2 * cw:])
    trow = lax.broadcasted_iota(jnp.int32, (tm, 1), 0)
    prev = jnp.where(trow == 0, before, pltpu.roll(gch, 1, axis=0))
    nxt = jnp.where(trow == tm - 1, after, pltpu.roll(gch, tm - 1, axis=0))
    conv = zc[:, :cw] * (prev * cw_ref[0:1] + gch * cw_ref[1:2] + nxt * cw_ref[2:3])
    mix = jnp.concatenate([rw_ref[0].astype(BF16), conv.astype(BF16), rt_ref[0].astype(BF16)], axis=1)
    o_ref[0] = x_ref[0] + gate_ref[0] * jnp.dot(mix, w_ref[...], preferred_element_type=F32)


def _mix_out(out_rw, z_cv, out_rt, x, gate, w_out_all, layer, conv_w):
    bsz, t, dm = x.shape
    rw, cvw, rtw = out_rw.shape[-1], z_cv.shape[-1], out_rt.shape[-1]
    tm = _pick_tile(t, (512, 256, 128))
    hl = 8 * 4 // z_cv.dtype.itemsize
    per = tm // hl
    nhb = t // hl
    return pl.pallas_call(
        functools.partial(_mix_out_kernel, tm=tm),
        grid=(bsz, t // tm),
        in_specs=[
            pl.BlockSpec((1, tm, rw), lambda b, i: (b, i, 0)),
            pl.BlockSpec((1, tm, cvw), lambda b, i: (b, i, 0)),
            pl.BlockSpec((1, hl, cvw), lambda b, i: (b, jnp.maximum(i * per - 1, 0), 0)),
            pl.BlockSpec((1, hl, cvw), lambda b, i: (b, jnp.minimum((i + 1) * per, nhb - 1), 0)),
            pl.BlockSpec((1, tm, rtw), lambda b, i: (b, i, 0)),
            pl.BlockSpec((1, tm, dm), lambda b, i: (b, i, 0)),
            pl.BlockSpec((1, 1, dm), lambda b, i: (b, 0, 0)),
            _resident((3, cvw // 3)),
            _resident((rw + cvw // 3 + rtw, dm), layer),
        ],
        out_specs=pl.BlockSpec((1, tm, dm), lambda b, i: (b, i, 0)),
        out_shape=jax.ShapeDtypeStruct((bsz, t, dm), F32),
        compiler_params=pltpu.CompilerParams(
            dimension_semantics=("parallel", "parallel"),
            vmem_limit_bytes=VMEM_LIMIT_BYTES),
        name="mix_out",
    )(out_rw, z_cv, z_cv, z_cv, out_rt, x, gate, conv_w.T, w_out_all)


FFN_HALO = 16

def _ffn_kernel(*refs, tm, final):
    if final:
        (x_ref, xp_ref, xn_ref, gm_ref, sh_ref, gate_ref, upg_ref, upv_ref, cwg_ref, cwv_ref, dn_ref,
         fg_ref, o_ref, h_scr, acc_scr) = refs
    else:
        (x_ref, xp_ref, xn_ref, gm_ref, sh_ref, gate_ref, upg_ref, upv_ref, cwg_ref, cwv_ref, dn_ref,
         o_ref, h_scr, acc_scr) = refs
    i = pl.program_id(1)
    j = pl.program_id(2)
    halo = FFN_HALO
    rows = tm + 2 * halo

    @pl.when(j == 0)
    def _():
        def modnorm(xb):
            ms = jnp.mean(xb * xb, axis=-1, keepdims=True)
            return xb * lax.rsqrt(ms + NORM_EPS) * gm_ref[0] + sh_ref[0]

        hp = jnp.where(i == 0, 0.0, modnorm(xp_ref[0]))
        hn = jnp.where(i == pl.num_programs(1) - 1, 0.0, modnorm(xn_ref[0]))
        h_scr[0:halo] = hp.astype(BF16)
        h_scr[halo:halo + tm] = modnorm(x_ref[0]).astype(BF16)
        h_scr[halo + tm:rows] = hn.astype(BF16)
        acc_scr[...] = jnp.zeros_like(acc_scr)

    h = h_scr[...]

    def conv_half(up_ref, cw_ref):
        pm = jnp.dot(h, up_ref[...], preferred_element_type=F32)
        cw = cw_ref[...]
        u = (pltpu.roll(pm, 1, axis=0) * cw[0:1] + pm * cw[1:2]
             + pltpu.roll(pm, rows - 1, axis=0) * cw[2:3])
        return u[halo:halo + tm]

    gt = conv_half(upg_ref, cwg_ref)
    vl = conv_half(upv_ref, cwv_ref)
    act = gt * jax.nn.sigmoid(gt) * vl
    acc_scr[...] += jnp.dot(act.astype(BF16), dn_ref[...], preferred_element_type=F32)

    @pl.when(j == pl.num_programs(2) - 1)
    def _():
        out = x_ref[0] + gate_ref[0] * acc_scr[...]
        if final:
            ms = jnp.mean(out * out, axis=-1, keepdims=True)
            out = out * lax.rsqrt(ms + NORM_EPS) * fg_ref[...]
        o_ref[0] = out


def _ffn(x, gmod, shift, gate, up, layer, cw, down, final_g=None):
    bsz, t, dm = x.shape
    f = down.shape[1]
    tm = _pick_tile(t, (512, 256, 128))
    tf = _pick_tile(f, (512, 256, 128))
    halo = FFN_HALO
    nf = f // tf
    per = tm // halo
    nhb = t // halo
    cwt = cw.T
    final = final_g is not None
    vec_spec = pl.BlockSpec((1, 1, dm), lambda b, i, j: (b, 0, 0))
    in_specs = [
        pl.BlockSpec((1, tm, dm), lambda b, i, j: (b, i, 0)),
        pl.BlockSpec((1, halo, dm), lambda b, i, j: (b, jnp.maximum(i * per - 1, 0), 0)),
        pl.BlockSpec((1, halo, dm), lambda b, i, j: (b, jnp.minimum((i + 1) * per, nhb - 1), 0)),
        vec_spec, vec_spec, vec_spec,
        pl.BlockSpec((None, dm, tf), lambda b, i, j: (layer, 0, j)),
        pl.BlockSpec((None, dm, tf), lambda b, i, j: (layer, 0, nf + j)),
        pl.BlockSpec((3, tf), lambda b, i, j: (0, j)),
        pl.BlockSpec((3, tf), lambda b, i, j: (0, nf + j)),
        pl.BlockSpec((None, tf, dm), lambda b, i, j: (layer, j, 0)),
    ]
    args = [x, x, x, gmod, shift, gate, up, up, cwt, cwt, down]
    if final:
        in_specs.append(pl.BlockSpec((1, dm), lambda b, i, j: (0, 0)))
        args.append(final_g.reshape(1, dm))
    return pl.pallas_call(
        functools.partial(_ffn_kernel, tm=tm, final=final),
        grid=(bsz, t // tm, nf),
        in_specs=in_specs,
        out_specs=pl.BlockSpec((1, tm, dm), lambda b, i, j: (b, i, 0)),
        out_shape=jax.ShapeDtypeStruct((bsz, t, dm), F32),
        scratch_shapes=[pltpu.VMEM((tm + 2 * halo, dm), BF16), pltpu.VMEM((tm, dm), F32)],
        compiler_params=pltpu.CompilerParams(
            dimension_semantics=("parallel", "parallel", "arbitrary"),
            vmem_limit_bytes=VMEM_LIMIT_BYTES),
        name="ffn",
    )(*args)


def _split_bf16(x):
    hi = x.astype(BF16)
    lo = (x - hi.astype(F32)).astype(BF16)
    return hi, lo


def _dot1(a, b, dn=_NN):
    return lax.dot_general(a.astype(BF16), b.astype(BF16), dn, preferred_element_type=F32)


def _wkv_kernel(*refs, reverse, npairs, chunk, nsub):
    if reverse:
        (feat_ref, ka_ref, w0_ref, a0_ref, lora_ref, s0_ref,
         yf_ref, a0f_ref, gup_ref, rk_ref, lng_ref, lnb_ref, y_ref, sT_ref, s_scr) = refs
    else:
        (z_ref, zp_ref, zn_ref, mu_ref, kk_ref, ka_ref, w0_ref, a0_ref, lora_ref, s0_ref,
         y_ref, feat_ref, sT_ref, s_scr) = refs
    c = chunk
    cb = nsub * c
    w2 = 2 * RWKV_HEAD_DIM
    rw = npairs * w2
    i = pl.program_id(1)
    nc = pl.num_programs(1)
    ci = nc - 1 - i if reverse else i

    @pl.when(i == 0)
    def _():
        s_scr[...] = s0_ref[0]

    row = lax.broadcasted_iota(jnp.int32, (cb, cb), 0)
    col = lax.broadcasted_iota(jnp.int32, (cb, cb), 1)
    same_chunk = row // c == col // c
    cum = (same_chunk & (row <= col if reverse else row >= col)).astype(BF16)
    last = 0 if reverse else c - 1
    first_head = lax.broadcasted_iota(jnp.int32, (c, w2), 1) < RWKV_HEAD_DIM
    first_head2 = jnp.concatenate([first_head, first_head], axis=1)
    brow = lax.broadcasted_iota(jnp.int32, (w2, w2), 0) < RWKV_HEAD_DIM
    bcol = lax.broadcasted_iota(jnp.int32, (w2, w2), 1) < RWKV_HEAD_DIM
    same_head = brow == bcol
    head_ones = same_head.astype(BF16)
    trow = lax.broadcasted_iota(jnp.int32, (c, 2 * c), 0)
    tcol = lax.broadcasted_iota(jnp.int32, (c, 2 * c), 1)
    tcol = jnp.where(tcol < c, tcol, tcol - c)
    eye_pair = jnp.where(trow == tcol, 1.0, 0.0)
    if reverse:
        strict, incl = trow < tcol, trow <= tcol
    else:
        strict, incl = trow > tcol, trow >= tcol
    pairs = range(npairs)
    sls = [slice(q * w2, (q + 1) * w2) for q in pairs]

    def head_sum(u):
        hi, lo = _split_bf16(u)
        ones2 = jnp.concatenate([head_ones, head_ones], axis=0)
        return jnp.concatenate(
            [jnp.dot(jnp.concatenate([hi[:, sl], lo[:, sl]], axis=1), ones2, preferred_element_type=F32)
             for sl in sls], axis=1)

    zw = 3 * rw + 2 * w2
    if reverse:
        z = feat_ref[0, :, :zw]
        kk = feat_ref[0, :, zw:]
    else:
        z = z_ref[0]
        trow = lax.broadcasted_iota(jnp.int32, (cb, 1), 0)
        z_before = jnp.where(ci == 0, 0.0, zp_ref[0, 7:8, :])
        z_after = jnp.where(ci == nc - 1, 0.0, zn_ref[0, 0:1, :])
        zprev = jnp.where(trow == 0, z_before, pltpu.roll(z, 1, axis=0))
        znext = jnp.where(trow == cb - 1, z_after, pltpu.roll(z, cb - 1, axis=0))
        z = z + (0.5 * (zprev + znext) - z) * mu_ref[...]
    r = z[:, :rw]
    k = z[:, rw:2 * rw]
    v_all = z[:, 2 * rw:3 * rw]
    wa = z[:, 3 * rw:3 * rw + w2]
    gd = z[:, 3 * rw + w2:]

    if not reverse:
        kkr = k * kk_ref[...]
        kk = kkr / jnp.maximum(jnp.sqrt(head_sum(kkr * kkr)), 1e-12)
        feat_ref[0, :, :zw] = z
        feat_ref[0, :, zw:] = kk
    decay_lane = lax.broadcasted_iota(jnp.int32, (cb, w2), 1) < DECAY_LORA
    lora = jnp.dot(jnp.where(decay_lane, jnp.tanh(wa), wa).astype(BF16), lora_ref[...],
                   preferred_element_type=F32)
    xw = w0_ref[...] + lora[:, :rw]
    w_log = -(jnp.maximum(-xw, 0.0) + jnp.log1p(jnp.exp(-jnp.abs(xw)))) - 0.5
    lw = -jnp.exp(w_log)
    rate = jax.nn.sigmoid(a0_ref[...] + lora[:, rw:2 * rw])
    key = k * (1.0 + (rate - 1.0) * ka_ref[...])

    lw_hi, lw_lo = _split_bf16(lw)
    cl = jnp.dot(jnp.concatenate([cum, cum], axis=1), jnp.concatenate([lw_hi, lw_lo], axis=0),
                 preferred_element_type=F32)
    p = jnp.exp(cl)
    pinv = jnp.exp(-cl)
    at_all = -kk * jnp.exp(cl - lw)
    bt_all = kk * rate * pinv
    kt_all = key * pinv
    rt_all = r * p

    def stack(u):
        m = first_head if u.shape[1] == w2 else first_head2
        zero = jnp.zeros_like(u)
        return jnp.concatenate([jnp.where(m, u, zero), jnp.where(m, zero, u)], axis=0)

    def dotb(a, b, dn=_NN):
        return lax.dot_general(a, b, dn, preferred_element_type=F32)

    def pdot(mat, u):
        return dotb(mat, stack(u))

    units = [(j, q) for j in range(nsub) for q in pairs]
    cut = lambda u: [u[j * c:(j + 1) * c, sls[q]] for j, q in units]
    rt = cut(rt_all)
    at, bt, kt, rtb, v = [cut(u.astype(BF16)) for u in (at_all, bt_all, kt_all, rt_all, v_all)]
    us = range(len(units))

    a_ab, a_ak, m_rb, m_rk = [], [], [], []
    for u in us:
        lhs = jnp.concatenate([at[u], rtb[u]], axis=0)
        mbk = dotb(lhs, jnp.concatenate([stack(bt[u]), stack(kt[u])], axis=0), _NT)
        a_ab.append(jnp.where(strict, mbk[:c, :2 * c], 0.0).astype(BF16))
        m_rb.append(jnp.where(incl, mbk[c:, :2 * c], 0.0).astype(BF16))
        a_ak.append(jnp.where(strict, mbk[:c, 2 * c:], 0.0).astype(BF16))
        m_rk.append(jnp.where(incl, mbk[c:, 2 * c:], 0.0).astype(BF16))

    tinv = [eye_pair + a_ab[u].astype(F32) for u in us]
    ak = [pdot(a_ab[u], a_ab[u]).astype(BF16) for u in us]
    nsteps = max(1, (c - 1).bit_length()) - 1
    for s in range(nsteps):
        if s + 1 < nsteps:
            both = [pdot(ak[u], jnp.concatenate([ak[u], tinv[u].astype(BF16)], axis=1)) for u in us]
            ak = [both[u][:, :2 * c].astype(BF16) for u in us]
            tinv = [tinv[u] + both[u][:, 2 * c:] for u in us]
        else:
            tinv = [tinv[u] + pdot(ak[u], tinv[u].astype(BF16)) for u in us]
    xu0 = [pdot(a_ak[u], v[u]).astype(BF16) for u in us]
    x = [pdot(tinv[u].astype(BF16), jnp.concatenate([at[u], xu0[u]], axis=1)).astype(BF16)
         for u in us]

    zero_v = [jnp.zeros_like(v[u]) for u in us]
    xv = [jnp.concatenate([zero_v[u], v[u]], axis=1) for u in us]
    ry = [dotb(jnp.concatenate([m_rb[u], m_rk[u]], axis=1),
               jnp.concatenate([stack(x[u]), stack(xv[u])], axis=0)) for u in us]
    rbar = [rt[u] + ry[u][:, :w2] for u in us]
    y0 = [ry[u][:, w2:] for u in us]
    gh = [dotb(jnp.concatenate([x[u], xv[u]], axis=0), jnp.concatenate([bt[u], kt[u]], axis=0), _TN)
          for u in us]
    g = [jnp.where(same_head, gh[u][:w2], 0.0) for u in us]
    hh = [jnp.where(same_head, gh[u][w2:], 0.0) for u in us]

    ys = [None] * len(units)
    state = [s_scr[q] for q in pairs]
    for j in (reversed(range(nsub)) if reverse else range(nsub)):
        for q in pairs:
            u = j * npairs + q
            s0 = state[q]
            ys[u] = _dot1(rbar[u], s0, _NT) + y0[u]
            state[q] = (s0 + _dot1(s0, g[u]) + hh[u]) * p[j * c + last:j * c + last + 1, sls[q]]
    for q in pairs:
        s_scr[q] = state[q]
    y = jnp.concatenate([jnp.concatenate(ys[j * npairs:(j + 1) * npairs], axis=1)
                         for j in range(nsub)], axis=0)

    if reverse:
        y = yf_ref[0] + y
        inv_n = 1.0 / RWKV_HEAD_DIM
        rate_f = jax.nn.sigmoid(a0f_ref[...] + lora[:, 2 * rw:])
        k_mid = 0.5 * (k * (1.0 + (rate_f - 1.0) * ka_ref[...]) + key)
        sums = head_sum(jnp.concatenate([y, r * k_mid * rk_ref[...]], axis=0))
        dev = y - sums[:cb] * inv_n
        yn = dev * lax.rsqrt(head_sum(dev * dev) * inv_n + RWKV_GN_EPS)
        bonus = sums[cb:] * v_all
        gate = jnp.dot(jax.nn.sigmoid(gd).astype(BF16), gup_ref[...], preferred_element_type=F32)
        y = (yn * lng_ref[...] + lnb_ref[...] + bonus) * gate
    y_ref[0] = y.astype(y_ref.dtype)

    @pl.when(i == pl.num_programs(1) - 1)
    def _():
        sT_ref[0] = s_scr[...]


def _wkv_scan(z, p, d, s0, y_fwd=None):
    mu, w0, w_up, a0, a_up, g_up, k_k, k_a, r_k, lnx_g, lnx_b = p
    bsz, t, zw = z.shape
    w = k_k.shape[0]
    w2 = 2 * RWKV_HEAD_DIM
    npairs = w // w2
    c = WKV_CHUNK
    assert c == RWKV_HEAD_DIM and t % c == 0 and w % w2 == 0
    nsub = WKV_CHUNKS_PER_STEP if (t // c) % WKV_CHUNKS_PER_STEP == 0 else 1
    cb = nsub * c
    nc = t // cb
    reverse = d == 1
    per = cb // 8
    nhb = t // 8
    pos = (lambda i: nc - 1 - i) if reverse else (lambda i: i)
    tok = lambda bi, i: (bi, pos(i), 0)
    row1 = lambda u: u.reshape(1, -1)
    zero = jnp.zeros((DECAY_LORA, w), F32)
    blocks = [jnp.concatenate([w_up[d], zero], axis=0), jnp.concatenate([zero, a_up[d]], axis=0)]
    if reverse:
        blocks.append(jnp.concatenate([zero, a_up[0]], axis=0))
    lora_w = jnp.concatenate(blocks, axis=1).astype(BF16)
    full = lambda arr: pl.BlockSpec(arr.shape, lambda bi, i: (0, 0))
    st_spec = pl.BlockSpec((1, npairs, w2, w2), lambda bi, i: (bi, 0, 0, 0))
    y_spec = pl.BlockSpec((1, cb, w), tok)
    common = [row1(k_a), row1(w0[d]), row1(a0[d]), lora_w]
    out_specs = [y_spec, st_spec]
    out_shape = [jax.ShapeDtypeStruct((bsz, t, w), BF16 if reverse else F32),
                 jax.ShapeDtypeStruct((bsz, npairs, w2, w2), F32)]
    if reverse:
        extra = [row1(a0[0]), g_up.astype(BF16), row1(r_k), row1(lnx_g), row1(lnx_b)]
        args = [z] + common + [s0, y_fwd] + extra
        in_specs = ([pl.BlockSpec((1, cb, zw), tok)] + [full(a) for a in common] + [st_spec, y_spec]
                    + [full(a) for a in extra])
    else:
        head = [row1(mu), row1(k_k)]
        args = [z, z, z] + head + common + [s0]
        in_specs = [
            pl.BlockSpec((1, cb, zw), tok),
            pl.BlockSpec((1, 8, zw), lambda bi, i: (bi, jnp.maximum(pos(i) * per - 1, 0), 0)),
            pl.BlockSpec((1, 8, zw), lambda bi, i: (bi, jnp.minimum((pos(i) + 1) * per, nhb - 1), 0)),
        ] + [full(a) for a in head + common] + [st_spec]
        out_specs.insert(1, pl.BlockSpec((1, cb, zw + w), tok))
        out_shape.insert(1, jax.ShapeDtypeStruct((bsz, t, zw + w), F32))
    return pl.pallas_call(
        functools.partial(_wkv_kernel, reverse=reverse, npairs=npairs, chunk=c, nsub=nsub),
        grid=(bsz, nc),
        in_specs=in_specs,
        out_specs=out_specs,
        out_shape=out_shape,
        scratch_shapes=[pltpu.VMEM((npairs, w2, w2), F32)],
        compiler_params=pltpu.CompilerParams(
            dimension_semantics=("parallel", "arbitrary"),
            vmem_limit_bytes=VMEM_LIMIT_BYTES),
        name="wkv_bwd" if reverse else "wkv_fwd",
    )(*args)


def _ret_kernel(*refs, nheads, reverse, nsub):
    refs = list(refs)
    z_ref = refs.pop(0)
    din_ref, xi_ref, zeta_ref, g_ref, r0_ref = refs[:5]
    refs = refs[5:]
    yf_ref, ng_ref = (refs.pop(0), refs.pop(0)) if reverse else (None, None)
    y_ref, rT_ref, r_scr = refs
    d = RET_HEAD_DIM
    w = nheads * d
    i = pl.program_id(1)

    @pl.when(i == 0)
    def _():
        r_scr[...] = r0_ref[0]

    c = RET_CHUNK
    units = [(j, h) for j in range(nsub) for h in range(nheads)]
    us = range(len(units))
    rows = lambda j: slice(j * c, (j + 1) * c)
    cols = lambda g, h: slice(g * w + h * d, g * w + (h + 1) * d)

    q = [z_ref[0, rows(j), cols(0, h)] for j, h in units]
    k = [z_ref[0, rows(j), cols(1, h)] for j, h in units]
    v = [z_ref[0, rows(j), cols(2, h)] for j, h in units]
    s = [_dot1(q[u], k[u], _NT) * din_ref[units[u][1]] for u in us]
    sv = [_dot1(s[u], v[u]) for u in us]
    kv = [_dot1(k[u].astype(F32) * zeta_ref[units[u][1]], v[u], _TN) for u in us]

    ys = [None] * len(units)
    state = [r_scr[h] for h in range(nheads)]
    for j in (reversed(range(nsub)) if reverse else range(nsub)):
        for h in range(nheads):
            u = j * nheads + h
            ys[u] = sv[u] + _dot1(q[u], state[h]) * xi_ref[h]
            state[h] = state[h] * g_ref[h] + kv[u]
    for h in range(nheads):
        r_scr[h] = state[h]

    for u, (j, h) in enumerate(units):
        y = ys[u]
        if reverse:
            y = yf_ref[0, rows(j), cols(0, h)].astype(F32) + y
            dev = y - jnp.mean(y, axis=-1, keepdims=True)
            yn = dev * lax.rsqrt(jnp.mean(dev * dev, axis=-1, keepdims=True) + NORM_EPS)
            gt = z_ref[0, rows(j), cols(3, h)].astype(F32)
            y = gt * jax.nn.sigmoid(gt) * (yn * ng_ref[:, cols(0, h)])
        y_ref[0, rows(j), cols(0, h)] = y.astype(y_ref.dtype)

    @pl.when(i == pl.num_programs(1) - 1)
    def _():
        rT_ref[0] = r_scr[...]


def _ret_tables(log_gamma, reverse):
    c, d = RET_CHUNK, RET_HEAD_DIM
    idx = jnp.arange(c, dtype=F32)
    lg = log_gamma[:, None, None]
    if reverse:
        diff = idx[None, :] - idx[:, None]
        keep = diff > 0
        xi = jnp.exp(log_gamma[:, None] * (c - idx))
        zeta = jnp.exp(log_gamma[:, None] * idx)
    else:
        diff = idx[:, None] - idx[None, :]
        keep = diff >= 0
        xi = jnp.exp(log_gamma[:, None] * (idx + 1.0))
        zeta = jnp.exp(log_gamma[:, None] * (c - 1.0 - idx))
    din = jnp.where(keep, jnp.exp(lg * jnp.maximum(diff, 0.0)), 0.0)
    nh = log_gamma.shape[0]
    xi = jnp.broadcast_to(xi[:, :, None], (nh, c, d))
    zeta = jnp.broadcast_to(zeta[:, :, None], (nh, c, d))
    g = jnp.broadcast_to(jnp.exp(log_gamma * c)[:, None, None], (nh, d, d))
    return din, xi, zeta, g


def _ret_scan(z, r0, log_gamma, reverse, y_fwd=None, norm_g=None):
    bsz, t, zw = z.shape
    d = RET_HEAD_DIM
    nh = zw // (4 * d)
    w = nh * d
    c = RET_CHUNK
    assert t % c == 0
    per_step = RET_CHUNKS_PER_STEP[1 if reverse else 0]
    nsub = per_step if (t // c) % per_step == 0 else 1
    cb = nsub * c
    nc = t // cb
    din, xi, zeta, g = _ret_tables(log_gamma, reverse)
    pos = (lambda i: nc - 1 - i) if reverse else (lambda i: i)
    tok = lambda bi, i: (bi, pos(i), 0)
    y_spec = pl.BlockSpec((1, cb, w), tok)
    st_spec = pl.BlockSpec((1, nh, d, d), lambda bi, i: (bi, 0, 0, 0))
    tab_spec = lambda n: pl.BlockSpec((nh, n, d), lambda bi, i: (0, 0, 0))
    args = [z]
    in_specs = [pl.BlockSpec((1, cb, zw), tok)]
    args += [din, xi, zeta, g, r0]
    in_specs += [tab_spec(c), tab_spec(c), tab_spec(c), tab_spec(d), st_spec]
    if reverse:
        args += [y_fwd, norm_g.reshape(1, w)]
        in_specs += [y_spec, pl.BlockSpec((1, w), lambda bi, i: (0, 0))]
    return pl.pallas_call(
        functools.partial(_ret_kernel, nheads=nh, reverse=reverse, nsub=nsub),
        grid=(bsz, nc),
        in_specs=in_specs,
        out_specs=[y_spec, st_spec],
        out_shape=[jax.ShapeDtypeStruct((bsz, t, w), BF16),
                   jax.ShapeDtypeStruct((bsz, nh, d, d), F32)],
        scratch_shapes=[pltpu.VMEM((nh, d, d), F32)],
        compiler_params=pltpu.CompilerParams(
            dimension_semantics=("parallel", "arbitrary"),
            vmem_limit_bytes=VMEM_LIMIT_BYTES),
        name="ret_bwd" if reverse else "ret_fwd",
    )(*args)


def _rwkv_mix(z, p, s0_f, s0_b, want_out):
    y_f, feat, s_f = _wkv_scan(z, p, 0, s0_f)
    out, s_b = _wkv_scan(feat, p, 1, s0_b, y_f)
    return (out if want_out else None), s_f, s_b


def _ret_mix(z, r0_f, r0_b, lg_f, lg_b, ret_norm_g, want_out):
    y_f, r_f = _ret_scan(z, r0_f, lg_f, False)
    out, r_b = _ret_scan(z, r0_b, lg_b, True, y_f, ret_norm_g)
    return (out if want_out else None), r_f, r_b


def kernel(x, c, ctx, c_ctx, mod_w, mod_b, norm1_g, norm2_g, w_in, tshift_mu, w0, w_up, a0, a_up, g_up,
           k_k, k_a, r_k, lnx_g, lnx_b, conv_w, ret_norm_g, w_out, ffn_up, ffn_conv, ffn_down,
           final_norm_g):
    bsz, n, dm = x.shape
    depth = mod_w.shape[0]
    rwkv_w = k_k.shape[1]
    rwkv_cols = tshift_mu.shape[1]
    conv_cols = 3 * conv_w.shape[1]
    ret_heads = ret_norm_g.shape[1] // RET_HEAD_DIM
    npairs = rwkv_w // (2 * RWKV_HEAD_DIM)

    rows = n // GRID_W
    row = jnp.repeat(jnp.arange(rows, dtype=F32), GRID_W)
    col = jnp.tile(jnp.arange(GRID_W, dtype=F32), rows)
    pairs = RET_HEAD_DIM // 4
    freqs = 1.0 / (ROPE_BASE ** (jnp.arange(pairs, dtype=F32) / pairs))
    ang = jnp.concatenate([row[:, None] * freqs, col[:, None] * freqs], axis=-1)
    cos, sin = jnp.cos(ang), jnp.sin(ang)
    rope = (jnp.concatenate([cos, cos], axis=-1), jnp.concatenate([-sin, sin], axis=-1))
    hidx = jnp.arange(ret_heads, dtype=F32)
    lg_f = jnp.log1p(-jnp.exp2(-5.0 - hidx))
    lg_b = jnp.log1p(-jnp.exp2(-5.5 - hidx))
    s_zero = jnp.zeros((bsz, npairs, 2 * RWKV_HEAD_DIM, 2 * RWKV_HEAD_DIM), F32)
    r_zero = jnp.zeros((bsz, ret_heads, RET_HEAD_DIM, RET_HEAD_DIM), F32)

    xl = x.astype(F32)
    xc = ctx.astype(F32)
    silu_all = jnp.concatenate([jax.nn.silu(c.astype(F32)), jax.nn.silu(c_ctx.astype(F32))[None]], axis=0)

    w_in_b, w_out_b, up_b, down_b = (w.astype(BF16) for w in (w_in, w_out, ffn_up, ffn_down))
    widths = (rwkv_cols, conv_cols, w_in.shape[2] - rwkv_cols - conv_cols)
    z_dtypes = (F32, BF16, BF16)
    bc = lambda m: jnp.broadcast_to(m, (bsz, 1, dm))

    for l in range(depth):
        last = l == depth - 1
        mod = _mm_rows(silu_all, mod_w, l) + mod_b[l]
        ml = [m[:, None, :] for m in jnp.split(mod[:bsz], 6, axis=-1)]
        mc = jnp.split(mod[bsz], 6, axis=-1)
        rwkv_p = (tshift_mu[l], w0[l], w_up[l], a0[l], a_up[l], g_up[l], k_k[l], k_a[l],
                  r_k[l], lnx_g[l], lnx_b[l])

        gm_c, sh_c = bc(norm1_g[l] * (1.0 + mc[1])), bc(mc[0])
        gm_l, sh_l = norm1_g[l] * (1.0 + ml[1]), ml[0]
        ret_w = ret_heads * RET_HEAD_DIM
        zc_rw, zc_cv, zc_rt = _mm_norm(xc, gm_c, sh_c, w_in_b, l, widths, z_dtypes, ret_w)
        zl_rw, zl_cv, zl_rt = _mm_norm(xl, gm_l, sh_l, w_in_b, l, widths, z_dtypes, ret_w, rope)

        out_c_rw, s_f, s_b = _rwkv_mix(zc_rw, rwkv_p, s_zero, s_zero, not last)
        out_l_rw, _, _ = _rwkv_mix(zl_rw, rwkv_p, s_f, s_b, True)
        out_c_rt, r_f, r_b = _ret_mix(zc_rt, r_zero, r_zero, lg_f, lg_b, ret_norm_g[l], not last)
        out_l_rt, _, _ = _ret_mix(zl_rt, r_f, r_b, lg_f, lg_b, ret_norm_g[l], True)

        xl = _mix_out(out_l_rw, zl_cv, out_l_rt, xl, ml[2], w_out_b, l, conv_w[l])
        xl = _ffn(xl, norm2_g[l] * (1.0 + ml[4]), ml[3], ml[5], up_b, l, ffn_conv[l], down_b,
                  final_norm_g if last else None)
        if not last:
            xc = _mix_out(out_c_rw, zc_cv, out_c_rt, xc, bc(mc[2]), w_out_b, l, conv_w[l])
            xc = _ffn(xc, bc(norm2_g[l] * (1.0 + mc[4])), bc(mc[3]), bc(mc[5]), up_b, l, ffn_conv[l],
                      down_b)

    return xl.astype(x.dtype)
```

```python
import functools

import jax
import jax.numpy as jnp
from jax import lax
from jax.experimental import pallas as pl
from jax.experimental.pallas import tpu as pltpu

F32 = jnp.float32
BF16 = jnp.bfloat16

GRID_W = 64
RWKV_HEAD_DIM = 64
DECAY_LORA = 64
AAA_LORA = 64
GATE_LORA = 128
RET_HEAD_DIM = 128
RET_CHUNK = 128
RET_CHUNKS_PER_STEP = (8, 4)
WKV_CHUNK = 64
WKV_CHUNKS_PER_STEP = 8
ROPE_BASE = 10000.0
NORM_EPS = 1e-6
RWKV_GN_EPS = 64e-5

VMEM_LIMIT_BYTES = 48 * 1024 * 1024
VMEM_LIMIT_IN_PROJ_BYTES = 56 * 1024 * 1024

_NT = (((1,), (1,)), ((), ()))
_TN = (((0,), (0,)), ((), ()))
_NN = (((1,), (0,)), ((), ()))


def _mm_kernel(a_ref, b_ref, o_ref):
    o_ref[...] = jnp.dot(a_ref[...].astype(BF16), b_ref[...].astype(BF16), preferred_element_type=F32)


def _pick_tile(n, candidates):
    for c in candidates:
        if n % c == 0:
            return c
    return n


def _chunks_per_step(nchunks, preferred):
    while nchunks % preferred:
        preferred //= 2
    return preferred


def _mm_rows(a, b_all, layer):
    m, k = a.shape
    n = b_all.shape[-1]
    mp = -(-m // 8) * 8
    a = jnp.pad(a, ((0, mp - m), (0, 0)))
    tn = _pick_tile(n, (512, 384, 256, 128))
    out = pl.pallas_call(
        _mm_kernel,
        grid=(n // tn,),
        in_specs=[pl.BlockSpec((mp, k), lambda j: (0, 0)),
                  pl.BlockSpec((None, k, tn), lambda j: (layer, 0, j))],
        out_specs=pl.BlockSpec((mp, tn), lambda j: (0, j)),
        out_shape=jax.ShapeDtypeStruct((mp, n), F32),
        compiler_params=pltpu.CompilerParams(
            dimension_semantics=("parallel",),
            vmem_limit_bytes=VMEM_LIMIT_BYTES),
        name="mm",
    )(a, b_all)
    return out[:m]


def _resident(shape, layer=None):
    if layer is None:
        return pl.BlockSpec(shape, lambda *_: (0,) * len(shape), pipeline_mode=pl.Buffered(1))
    return pl.BlockSpec((None,) + tuple(shape), lambda *_: (layer,) + (0,) * len(shape),
                        pipeline_mode=pl.Buffered(1))


def _mm_norm_kernel(x_ref, gm_ref, sh_ref, w_ref, *refs, use_rope, ret_w):
    if use_rope:
        cos_ref, sin_ref, *o_refs = refs
    else:
        o_refs = refs
    xb = x_ref[0]
    ms = jnp.mean(xb * xb, axis=-1, keepdims=True)
    h = (xb * lax.rsqrt(ms + NORM_EPS) * gm_ref[0] + sh_ref[0]).astype(BF16)
    start = 0
    for o_ref in o_refs:
        n = o_ref.shape[-1]
        z = jnp.dot(h, w_ref[:, start:start + n], preferred_element_type=F32)
        start += n
        if o_ref is o_refs[-1]:
            d = RET_HEAD_DIM
            for c0 in range(0, 2 * ret_w, d):
                u = z[:, c0:c0 + d]
                if use_rope:
                    u = u * cos_ref[...] + pltpu.roll(u, d // 2, axis=1) * sin_ref[...]
                if c0 < ret_w:
                    u = u * (d ** -0.5)
                o_ref[0, :, c0:c0 + d] = u.astype(o_ref.dtype)
            o_ref[0, :, 2 * ret_w:] = z[:, 2 * ret_w:].astype(o_ref.dtype)
        else:
            o_ref[0] = z.astype(o_ref.dtype)


def _mm_norm(x, gmod, shift, w_all, layer, widths, dtypes, ret_w, rope=None):
    bsz, t, dm = x.shape
    assert sum(widths) == w_all.shape[2] and widths[-1] == 4 * ret_w
    tm = _pick_tile(t, (256, 128))
    vec_spec = pl.BlockSpec((1, 1, dm), lambda b, i: (b, 0, 0))
    in_specs = [pl.BlockSpec((1, tm, dm), lambda b, i: (b, i, 0)), vec_spec, vec_spec,
                _resident(w_all.shape[1:], layer)]
    args = [x, gmod, shift, w_all]
    if rope is not None:
        in_specs += [pl.BlockSpec((tm, RET_HEAD_DIM), lambda b, i: (i, 0))] * 2
        args += list(rope)
    return pl.pallas_call(
        functools.partial(_mm_norm_kernel, use_rope=rope is not None, ret_w=ret_w),
        grid=(bsz, t // tm),
        in_specs=in_specs,
        out_specs=[pl.BlockSpec((1, tm, n), lambda b, i: (b, i, 0)) for n in widths],
        out_shape=[jax.ShapeDtypeStruct((bsz, t, n), dt) for n, dt in zip(widths, dtypes)],
        compiler_params=pltpu.CompilerParams(
            dimension_semantics=("parallel", "parallel"),
            vmem_limit_bytes=VMEM_LIMIT_IN_PROJ_BYTES),
        name="mm_norm",
    )(*args)


def _mix_out_kernel(rw_ref, cv_ref, cvp_ref, cvn_ref, rt_ref, x_ref, gate_ref, cw_ref, w_ref, o_ref, *, tm):
    i = pl.program_id(1)
    cw = cv_ref.shape[-1] // 3
    hl = cvp_ref.shape[1]
    zc = cv_ref[0].astype(F32)
    gch = zc[:, cw:2 * cw] * zc[:, 2 * cw:]
    row_before = cvp_ref[0].astype(F32)[hl - 1:hl]
    row_after = cvn_ref[0].astype(F32)[0:1]
    before = jnp.where(i == 0, 0.0, row_before[:, cw:2 * cw] * row_before[:, 2 * cw:])
    after = jnp.where(i == pl.num_programs(1) - 1, 0.0, row_after[:, cw:2 * cw] * row_after[:, 2 * cw:])
    trow = lax.broadcasted_iota(jnp.int32, (tm, 1), 0)
    prev = jnp.where(trow == 0, before, pltpu.roll(gch, 1, axis=0))
    nxt = jnp.where(trow == tm - 1, after, pltpu.roll(gch, tm - 1, axis=0))
    conv = zc[:, :cw] * (prev * cw_ref[0:1] + gch * cw_ref[1:2] + nxt * cw_ref[2:3])
    mix = jnp.concatenate([rw_ref[0].astype(BF16), conv.astype(BF16), rt_ref[0].astype(BF16)], axis=1)
    o_ref[0] = x_ref[0] + gate_ref[0] * jnp.dot(mix, w_ref[...], preferred_element_type=F32)


def _mix_out(out_rw, z_cv, out_rt, x, gate, w_out_all, layer, conv_w):
    bsz, t, dm = x.shape
    rw, cvw, rtw = out_rw.shape[-1], z_cv.shape[-1], out_rt.shape[-1]
    tm = _pick_tile(t, (512, 256, 128))
    hl = 8 * 4 // z_cv.dtype.itemsize
    per = tm // hl
    nhb = t // hl
    return pl.pallas_call(
        functools.partial(_mix_out_kernel, tm=tm),
        grid=(bsz, t // tm),
        in_specs=[
            pl.BlockSpec((1, tm, rw), lambda b, i: (b, i, 0)),
            pl.BlockSpec((1, tm, cvw), lambda b, i: (b, i, 0)),
            pl.BlockSpec((1, hl, cvw), lambda b, i: (b, jnp.maximum(i * per - 1, 0), 0)),
            pl.BlockSpec((1, hl, cvw), lambda b, i: (b, jnp.minimum((i + 1) * per, nhb - 1), 0)),
            pl.BlockSpec((1, tm, rtw), lambda b, i: (b, i, 0)),
            pl.BlockSpec((1, tm, dm), lambda b, i: (b, i, 0)),
            pl.BlockSpec((1, 1, dm), lambda b, i: (b, 0, 0)),
            _resident((3, cvw // 3)),
            _resident((rw + cvw // 3 + rtw, dm), layer),
        ],
        out_specs=pl.BlockSpec((1, tm, dm), lambda b, i: (b, i, 0)),
        out_shape=jax.ShapeDtypeStruct((bsz, t, dm), F32),
        compiler_params=pltpu.CompilerParams(
            dimension_semantics=("parallel", "parallel"),
            vmem_limit_bytes=VMEM_LIMIT_BYTES),
        name="mix_out",
    )(out_rw, z_cv, z_cv, z_cv, out_rt, x, gate, conv_w.T, w_out_all)


FFN_HALO = 16

def _ffn_kernel(*refs, tm, final):
    if final:
        (x_ref, xp_ref, xn_ref, gm_ref, sh_ref, gate_ref, upg_ref, upv_ref, cwg_ref, cwv_ref, dn_ref,
         fg_ref, o_ref, h_scr, acc_scr) = refs
    else:
        (x_ref, xp_ref, xn_ref, gm_ref, sh_ref, gate_ref, upg_ref, upv_ref, cwg_ref, cwv_ref, dn_ref,
         o_ref, h_scr, acc_scr) = refs
    i = pl.program_id(1)
    j = pl.program_id(2)
    halo = FFN_HALO
    rows = tm + 2 * halo

    @pl.when(j == 0)
    def _():
        def modnorm(xb):
            ms = jnp.mean(xb * xb, axis=-1, keepdims=True)
            return xb * lax.rsqrt(ms + NORM_EPS) * gm_ref[0] + sh_ref[0]

        hp = jnp.where(i == 0, 0.0, modnorm(xp_ref[0]))
        hn = jnp.where(i == pl.num_programs(1) - 1, 0.0, modnorm(xn_ref[0]))
        h_scr[0:halo] = hp.astype(BF16)
        h_scr[halo:halo + tm] = modnorm(x_ref[0]).astype(BF16)
        h_scr[halo + tm:rows] = hn.astype(BF16)
        acc_scr[...] = jnp.zeros_like(acc_scr)

    h = h_scr[...]

    def conv_half(up_ref, cw_ref):
        pm = jnp.dot(h, up_ref[...], preferred_element_type=F32)
        cw = cw_ref[...]
        u = (pltpu.roll(pm, 1, axis=0) * cw[0:1] + pm * cw[1:2]
             + pltpu.roll(pm, rows - 1, axis=0) * cw[2:3])
        return u[halo:halo + tm]

    gt = conv_half(upg_ref, cwg_ref)
    vl = conv_half(upv_ref, cwv_ref)
    act = gt * jax.nn.sigmoid(gt) * vl
    acc_scr[...] += jnp.dot(act.astype(BF16), dn_ref[...], preferred_element_type=F32)

    @pl.when(j == pl.num_programs(2) - 1)
    def _():
        out = x_ref[0] + gate_ref[0] * acc_scr[...]
        if final:
            ms = jnp.mean(out * out, axis=-1, keepdims=True)
            out = out * lax.rsqrt(ms + NORM_EPS) * fg_ref[...]
        o_ref[0] = out


def _ffn(x, gmod, shift, gate, up, layer, cw, down, final_g=None):
    bsz, t, dm = x.shape
    f = down.shape[1]
    tm = _pick_tile(t, (512, 256, 128))
    tf = _pick_tile(f, (512, 256, 128))
    halo = FFN_HALO
    nf = f // tf
    per = tm // halo
    nhb = t // halo
    cwt = cw.T
    final = final_g is not None
    vec_spec = pl.BlockSpec((1, 1, dm), lambda b, i, j: (b, 0, 0))
    in_specs = [
        pl.BlockSpec((1, tm, dm), lambda b, i, j: (b, i, 0)),
        pl.BlockSpec((1, halo, dm), lambda b, i, j: (b, jnp.maximum(i * per - 1, 0), 0)),
        pl.BlockSpec((1, halo, dm), lambda b, i, j: (b, jnp.minimum((i + 1) * per, nhb - 1), 0)),
        vec_spec, vec_spec, vec_spec,
        pl.BlockSpec((None, dm, tf), lambda b, i, j: (layer, 0, j)),
        pl.BlockSpec((None, dm, tf), lambda b, i, j: (layer, 0, nf + j)),
        pl.BlockSpec((3, tf), lambda b, i, j: (0, j)),
        pl.BlockSpec((3, tf), lambda b, i, j: (0, nf + j)),
        pl.BlockSpec((None, tf, dm), lambda b, i, j: (layer, j, 0)),
    ]
    args = [x, x, x, gmod, shift, gate, up, up, cwt, cwt, down]
    if final:
        in_specs.append(pl.BlockSpec((1, dm), lambda b, i, j: (0, 0)))
        args.append(final_g.reshape(1, dm))
    return pl.pallas_call(
        functools.partial(_ffn_kernel, tm=tm, final=final),
        grid=(bsz, t // tm, nf),
        in_specs=in_specs,
        out_specs=pl.BlockSpec((1, tm, dm), lambda b, i, j: (b, i, 0)),
        out_shape=jax.ShapeDtypeStruct((bsz, t, dm), F32),
        scratch_shapes=[pltpu.VMEM((tm + 2 * halo, dm), BF16), pltpu.VMEM((tm, dm), F32)],
        compiler_params=pltpu.CompilerParams(
            dimension_semantics=("parallel", "parallel", "arbitrary"),
            vmem_limit_bytes=VMEM_LIMIT_BYTES),
        name="ffn",
    )(*args)


def _split_bf16(x):
    hi = x.astype(BF16)
    lo = (x - hi.astype(F32)).astype(BF16)
    return hi, lo


def _dot1(a, b, dn=_NN):
    return lax.dot_general(a.astype(BF16), b.astype(BF16), dn, preferred_element_type=F32)


def _wkv_kernel(*refs, reverse, npairs, chunk, nsub):
    if reverse:
        (feat_ref, ka_ref, w0_ref, a0_ref, lora_ref, s0_ref,
         yf_ref, a0f_ref, gup_ref, rk_ref, lng_ref, lnb_ref, y_ref, sT_ref, s_scr) = refs
    else:
        (z_ref, zp_ref, zn_ref, mu_ref, kk_ref, ka_ref, w0_ref, a0_ref, lora_ref, s0_ref,
         y_ref, feat_ref, sT_ref, s_scr) = refs
    c = chunk
    cb = nsub * c
    w2 = 2 * RWKV_HEAD_DIM
    rw = npairs * w2
    i = pl.program_id(1)
    nc = pl.num_programs(1)
    ci = nc - 1 - i if reverse else i

    @pl.when(i == 0)
    def _():
        s_scr[...] = s0_ref[0]

    row = lax.broadcasted_iota(jnp.int32, (cb, cb), 0)
    col = lax.broadcasted_iota(jnp.int32, (cb, cb), 1)
    same_chunk = row // c == col // c
    cum = (same_chunk & (row <= col if reverse else row >= col)).astype(BF16)
    last = 0 if reverse else c - 1
    first_head = lax.broadcasted_iota(jnp.int32, (c, w2), 1) < RWKV_HEAD_DIM
    first_head2 = jnp.concatenate([first_head, first_head], axis=1)
    brow = lax.broadcasted_iota(jnp.int32, (w2, w2), 0) < RWKV_HEAD_DIM
    bcol = lax.broadcasted_iota(jnp.int32, (w2, w2), 1) < RWKV_HEAD_DIM
    same_head = brow == bcol
    head_ones = same_head.astype(BF16)
    trow = lax.broadcasted_iota(jnp.int32, (c, 2 * c), 0)
    tcol = lax.broadcasted_iota(jnp.int32, (c, 2 * c), 1)
    tcol = jnp.where(tcol < c, tcol, tcol - c)
    eye_pair = jnp.where(trow == tcol, 1.0, 0.0)
    if reverse:
        strict, incl = trow < tcol, trow <= tcol
    else:
        strict, incl = trow > tcol, trow >= tcol
    pairs = range(npairs)
    sls = [slice(q * w2, (q + 1) * w2) for q in pairs]

    def head_sum(u):
        hi, lo = _split_bf16(u)
        ones2 = jnp.concatenate([head_ones, head_ones], axis=0)
        return jnp.concatenate(
            [jnp.dot(jnp.concatenate([hi[:, sl], lo[:, sl]], axis=1), ones2, preferred_element_type=F32)
             for sl in sls], axis=1)

    zw = 3 * rw + 2 * w2
    if reverse:
        z = feat_ref[0, :, :zw]
        kk = feat_ref[0, :, zw:]
    else:
        z = z_ref[0]
        trow = lax.broadcasted_iota(jnp.int32, (cb, 1), 0)
        z_before = jnp.where(ci == 0, 0.0, zp_ref[0, 7:8, :])
        z_after = jnp.where(ci == nc - 1, 0.0, zn_ref[0, 0:1, :])
        zprev = jnp.where(trow == 0, z_before, pltpu.roll(z, 1, axis=0))
        znext = jnp.where(trow == cb - 1, z_after, pltpu.roll(z, cb - 1, axis=0))
        z = z + (0.5 * (zprev + znext) - z) * mu_ref[...]
    r = z[:, :rw]
    k = z[:, rw:2 * rw]
    v_all = z[:, 2 * rw:3 * rw]
    wa = z[:, 3 * rw:3 * rw + w2]
    gd = z[:, 3 * rw + w2:]

    if not reverse:
        kkr = k * kk_ref[...]
        kk = kkr / jnp.maximum(jnp.sqrt(head_sum(kkr * kkr)), 1e-12)
        feat_ref[0, :, :zw] = z
        feat_ref[0, :, zw:] = kk
    decay_lane = lax.broadcasted_iota(jnp.int32, (cb, w2), 1) < DECAY_LORA
    lora = jnp.dot(jnp.where(decay_lane, jnp.tanh(wa), wa).astype(BF16), lora_ref[...],
                   preferred_element_type=F32)
    xw = w0_ref[...] + lora[:, :rw]
    w_log = -(jnp.maximum(-xw, 0.0) + jnp.log1p(jnp.exp(-jnp.abs(xw)))) - 0.5
    lw = -jnp.exp(w_log)
    rate = jax.nn.sigmoid(a0_ref[...] + lora[:, rw:2 * rw])
    key = k * (1.0 + (rate - 1.0) * ka_ref[...])

    lw_hi, lw_lo = _split_bf16(lw)
    cl = jnp.dot(jnp.concatenate([cum, cum], axis=1), jnp.concatenate([lw_hi, lw_lo], axis=0),
                 preferred_element_type=F32)
    p = jnp.exp(cl)
    pinv = jnp.exp(-cl)
    at_all = -kk * jnp.exp(cl - lw)
    bt_all = kk * rate * pinv
    kt_all = key * pinv
    rt_all = r * p

    def stack(u):
        m = first_head if u.shape[1] == w2 else first_head2
        zero = jnp.zeros_like(u)
        return jnp.concatenate([jnp.where(m, u, zero), jnp.where(m, zero, u)], axis=0)

    def dotb(a, b, dn=_NN):
        return lax.dot_general(a, b, dn, preferred_element_type=F32)

    def pdot(mat, u):
        return dotb(mat, stack(u))

    units = [(j, q) for j in range(nsub) for q in pairs]
    cut = lambda u: [u[j * c:(j + 1) * c, sls[q]] for j, q in units]
    rt = cut(rt_all)
    at, bt, kt, rtb, v = [cut(u.astype(BF16)) for u in (at_all, bt_all, kt_all, rt_all, v_all)]
    us = range(len(units))

    a_ab, a_ak, m_rb, m_rk = [], [], [], []
    for u in us:
        lhs = jnp.concatenate([at[u], rtb[u]], axis=0)
        mbk = dotb(lhs, jnp.concatenate([stack(bt[u]), stack(kt[u])], axis=0), _NT)
        a_ab.append(jnp.where(strict, mbk[:c, :2 * c], 0.0).astype(BF16))
        m_rb.append(jnp.where(incl, mbk[c:, :2 * c], 0.0).astype(BF16))
        a_ak.append(jnp.where(strict, mbk[:c, 2 * c:], 0.0).astype(BF16))
        m_rk.append(jnp.where(incl, mbk[c:, 2 * c:], 0.0).astype(BF16))

    tinv = [eye_pair + a_ab[u].astype(F32) for u in us]
    ak = [pdot(a_ab[u], a_ab[u]).astype(BF16) for u in us]
    nsteps = max(1, (c - 1).bit_length()) - 1
    for s in range(nsteps):
        if s + 1 < nsteps:
            both = [pdot(ak[u], jnp.concatenate([ak[u], tinv[u].astype(BF16)], axis=1)) for u in us]
            ak = [both[u][:, :2 * c].astype(BF16) for u in us]
            tinv = [tinv[u] + both[u][:, 2 * c:] for u in us]
        else:
            tinv = [tinv[u] + pdot(ak[u], tinv[u].astype(BF16)) for u in us]
    xu0 = [pdot(a_ak[u], v[u]).astype(BF16) for u in us]
    x = [pdot(tinv[u].astype(BF16), jnp.concatenate([at[u], xu0[u]], axis=1)).astype(BF16)
         for u in us]

    zero_v = [jnp.zeros_like(v[u]) for u in us]
    xv = [jnp.concatenate([zero_v[u], v[u]], axis=1) for u in us]
    ry = [dotb(jnp.concatenate([m_rb[u], m_rk[u]], axis=1),
               jnp.concatenate([stack(x[u]), stack(xv[u])], axis=0)) for u in us]
    rbar = [rt[u] + ry[u][:, :w2] for u in us]
    y0 = [ry[u][:, w2:] for u in us]
    gh = [dotb(jnp.concatenate([x[u], xv[u]], axis=0), jnp.concatenate([bt[u], kt[u]], axis=0), _TN)
          for u in us]
    g = [jnp.where(same_head, gh[u][:w2], 0.0) for u in us]
    hh = [jnp.where(same_head, gh[u][w2:], 0.0) for u in us]

    ys = [None] * len(units)
    state = [s_scr[q] for q in pairs]
    for j in (reversed(range(nsub)) if reverse else range(nsub)):
        for q in pairs:
            u = j * npairs + q
            s0 = state[q]
            ys[u] = _dot1(rbar[u], s0, _NT) + y0[u]
            state[q] = (s0 + _dot1(s0, g[u]) + hh[u]) * p[j * c + last:j * c + last + 1, sls[q]]
    for q in pairs:
        s_scr[q] = state[q]
    y = jnp.concatenate([jnp.concatenate(ys[j * npairs:(j + 1) * npairs], axis=1)
                         for j in range(nsub)], axis=0)

    if reverse:
        y = yf_ref[0] + y
        inv_n = 1.0 / RWKV_HEAD_DIM
        rate_f = jax.nn.sigmoid(a0f_ref[...] + lora[:, 2 * rw:])
        k_mid = 0.5 * (k * (1.0 + (rate_f - 1.0) * ka_ref[...]) + key)
        sums = head_sum(jnp.concatenate([y, r * k_mid * rk_ref[...]], axis=0))
        dev = y - sums[:cb] * inv_n
        yn = dev * lax.rsqrt(head_sum(dev * dev) * inv_n + RWKV_GN_EPS)
        bonus = sums[cb:] * v_all
        gate = jnp.dot(jax.nn.sigmoid(gd).astype(BF16), gup_ref[...], preferred_element_type=F32)
        y = (yn * lng_ref[...] + lnb_ref[...] + bonus) * gate
    y_ref[0] = y.astype(y_ref.dtype)

    @pl.when(i == pl.num_programs(1) - 1)
    def _():
        sT_ref[0] = s_scr[...]


def _wkv_scan(z, p, d, s0, y_fwd=None):
    mu, w0, w_up, a0, a_up, g_up, k_k, k_a, r_k, lnx_g, lnx_b = p
    bsz, t, zw = z.shape
    w = k_k.shape[0]
    w2 = 2 * RWKV_HEAD_DIM
    npairs = w // w2
    c = WKV_CHUNK
    assert c == RWKV_HEAD_DIM and t % c == 0 and w % w2 == 0
    nsub = _chunks_per_step(t // c, WKV_CHUNKS_PER_STEP)
    cb = nsub * c
    nc = t // cb
    reverse = d == 1
    per = cb // 8
    nhb = t // 8
    pos = (lambda i: nc - 1 - i) if reverse else (lambda i: i)
    tok = lambda bi, i: (bi, pos(i), 0)
    row1 = lambda u: u.reshape(1, -1)
    zero = jnp.zeros((DECAY_LORA, w), F32)
    blocks = [jnp.concatenate([w_up[d], zero], axis=0), jnp.concatenate([zero, a_up[d]], axis=0)]
    if reverse:
        blocks.append(jnp.concatenate([zero, a_up[0]], axis=0))
    lora_w = jnp.concatenate(blocks, axis=1).astype(BF16)
    full = lambda arr: pl.BlockSpec(arr.shape, lambda bi, i: (0, 0))
    st_spec = pl.BlockSpec((1, npairs, w2, w2), lambda bi, i: (bi, 0, 0, 0))
    y_spec = pl.BlockSpec((1, cb, w), tok)
    common = [row1(k_a), row1(w0[d]), row1(a0[d]), lora_w]
    out_specs = [y_spec, st_spec]
    out_shape = [jax.ShapeDtypeStruct((bsz, t, w), BF16 if reverse else F32),
                 jax.ShapeDtypeStruct((bsz, npairs, w2, w2), F32)]
    if reverse:
        extra = [row1(a0[0]), g_up.astype(BF16), row1(r_k), row1(lnx_g), row1(lnx_b)]
        args = [z] + common + [s0, y_fwd] + extra
        in_specs = ([pl.BlockSpec((1, cb, zw), tok)] + [full(a) for a in common] + [st_spec, y_spec]
                    + [full(a) for a in extra])
    else:
        head = [row1(mu), row1(k_k)]
        args = [z, z, z] + head + common + [s0]
        in_specs = [
            pl.BlockSpec((1, cb, zw), tok),
            pl.BlockSpec((1, 8, zw), lambda bi, i: (bi, jnp.maximum(pos(i) * per - 1, 0), 0)),
            pl.BlockSpec((1, 8, zw), lambda bi, i: (bi, jnp.minimum((pos(i) + 1) * per, nhb - 1), 0)),
        ] + [full(a) for a in head + common] + [st_spec]
        out_specs.insert(1, pl.BlockSpec((1, cb, zw + w), tok))
        out_shape.insert(1, jax.ShapeDtypeStruct((bsz, t, zw + w), F32))
    return pl.pallas_call(
        functools.partial(_wkv_kernel, reverse=reverse, npairs=npairs, chunk=c, nsub=nsub),
        grid=(bsz, nc),
        in_specs=in_specs,
        out_specs=out_specs,
        out_shape=out_shape,
        scratch_shapes=[pltpu.VMEM((npairs, w2, w2), F32)],
        compiler_params=pltpu.CompilerParams(
            dimension_semantics=("parallel", "arbitrary"),
            vmem_limit_bytes=VMEM_LIMIT_BYTES),
        name="wkv_bwd" if reverse else "wkv_fwd",
    )(*args)


def _ret_kernel(*refs, nheads, reverse, nsub):
    refs = list(refs)
    z_ref = refs.pop(0)
    din_ref, xi_ref, zeta_ref, g_ref, r0_ref = refs[:5]
    refs = refs[5:]
    yf_ref, ng_ref = (refs.pop(0), refs.pop(0)) if reverse else (None, None)
    y_ref, rT_ref, r_scr = refs
    d = RET_HEAD_DIM
    w = nheads * d
    i = pl.program_id(1)

    @pl.when(i == 0)
    def _():
        r_scr[...] = r0_ref[0]

    c = RET_CHUNK
    units = [(j, h) for j in range(nsub) for h in range(nheads)]
    us = range(len(units))
    rows = lambda j: slice(j * c, (j + 1) * c)
    cols = lambda g, h: slice(g * w + h * d, g * w + (h + 1) * d)

    q = [z_ref[0, rows(j), cols(0, h)] for j, h in units]
    k = [z_ref[0, rows(j), cols(1, h)] for j, h in units]
    v = [z_ref[0, rows(j), cols(2, h)] for j, h in units]
    s = [_dot1(q[u], k[u], _NT) * din_ref[units[u][1]] for u in us]
    sv = [_dot1(s[u], v[u]) for u in us]
    kv = [_dot1(k[u].astype(F32) * zeta_ref[units[u][1]], v[u], _TN) for u in us]

    ys = [None] * len(units)
    state = [r_scr[h] for h in range(nheads)]
    for j in (reversed(range(nsub)) if reverse else range(nsub)):
        for h in range(nheads):
            u = j * nheads + h
            ys[u] = sv[u] + _dot1(q[u], state[h]) * xi_ref[h]
            state[h] = state[h] * g_ref[h] + kv[u]
    for h in range(nheads):
        r_scr[h] = state[h]

    for u, (j, h) in enumerate(units):
        y = ys[u]
        if reverse:
            y = yf_ref[0, rows(j), cols(0, h)].astype(F32) + y
            dev = y - jnp.mean(y, axis=-1, keepdims=True)
            yn = dev * lax.rsqrt(jnp.mean(dev * dev, axis=-1, keepdims=True) + NORM_EPS)
            gt = z_ref[0, rows(j), cols(3, h)].astype(F32)
            y = gt * jax.nn.sigmoid(gt) * (yn * ng_ref[:, cols(0, h)])
        y_ref[0, rows(j), cols(0, h)] = y.astype(y_ref.dtype)

    @pl.when(i == pl.num_programs(1) - 1)
    def _():
        rT_ref[0] = r_scr[...]


def _ret_tables(log_gamma, reverse):
    c, d = RET_CHUNK, RET_HEAD_DIM
    idx = jnp.arange(c, dtype=F32)
    lg = log_gamma[:, None, None]
    if reverse:
        diff = idx[None, :] - idx[:, None]
        keep = diff > 0
        xi = jnp.exp(log_gamma[:, None] * (c - idx))
        zeta = jnp.exp(log_gamma[:, None] * idx)
    else:
        diff = idx[:, None] - idx[None, :]
        keep = diff >= 0
        xi = jnp.exp(log_gamma[:, None] * (idx + 1.0))
        zeta = jnp.exp(log_gamma[:, None] * (c - 1.0 - idx))
    din = jnp.where(keep, jnp.exp(lg * jnp.maximum(diff, 0.0)), 0.0)
    nh = log_gamma.shape[0]
    xi = jnp.broadcast_to(xi[:, :, None], (nh, c, d))
    zeta = jnp.broadcast_to(zeta[:, :, None], (nh, c, d))
    g = jnp.broadcast_to(jnp.exp(log_gamma * c)[:, None, None], (nh, d, d))
    return din, xi, zeta, g


def _ret_scan(z, r0, log_gamma, reverse, y_fwd=None, norm_g=None):
    bsz, t, zw = z.shape
    d = RET_HEAD_DIM
    nh = zw // (4 * d)
    w = nh * d
    c = RET_CHUNK
    assert t % c == 0
    nsub = _chunks_per_step(t // c, RET_CHUNKS_PER_STEP[1 if reverse else 0])
    cb = nsub * c
    nc = t // cb
    din, xi, zeta, g = _ret_tables(log_gamma, reverse)
    pos = (lambda i: nc - 1 - i) if reverse else (lambda i: i)
    tok = lambda bi, i: (bi, pos(i), 0)
    y_spec = pl.BlockSpec((1, cb, w), tok)
    st_spec = pl.BlockSpec((1, nh, d, d), lambda bi, i: (bi, 0, 0, 0))
    tab_spec = lambda n: pl.BlockSpec((nh, n, d), lambda bi, i: (0, 0, 0))
    args = [z]
    in_specs = [pl.BlockSpec((1, cb, zw), tok)]
    args += [din, xi, zeta, g, r0]
    in_specs += [tab_spec(c), tab_spec(c), tab_spec(c), tab_spec(d), st_spec]
    if reverse:
        args += [y_fwd, norm_g.reshape(1, w)]
        in_specs += [y_spec, pl.BlockSpec((1, w), lambda bi, i: (0, 0))]
    return pl.pallas_call(
        functools.partial(_ret_kernel, nheads=nh, reverse=reverse, nsub=nsub),
        grid=(bsz, nc),
        in_specs=in_specs,
        out_specs=[y_spec, st_spec],
        out_shape=[jax.ShapeDtypeStruct((bsz, t, w), BF16),
                   jax.ShapeDtypeStruct((bsz, nh, d, d), F32)],
        scratch_shapes=[pltpu.VMEM((nh, d, d), F32)],
        compiler_params=pltpu.CompilerParams(
            dimension_semantics=("parallel", "arbitrary"),
            vmem_limit_bytes=VMEM_LIMIT_BYTES),
        name="ret_bwd" if reverse else "ret_fwd",
    )(*args)


def _rwkv_mix(z, p, s0_f, s0_b, want_out):
    y_f, feat, s_f = _wkv_scan(z, p, 0, s0_f)
    out, s_b = _wkv_scan(feat, p, 1, s0_b, y_f)
    return (out if want_out else None), s_f, s_b


def _ret_mix(z, r0_f, r0_b, lg_f, lg_b, ret_norm_g, want_out):
    y_f, r_f = _ret_scan(z, r0_f, lg_f, False)
    out, r_b = _ret_scan(z, r0_b, lg_b, True, y_f, ret_norm_g)
    return (out if want_out else None), r_f, r_b


def kernel(x, c, ctx, c_ctx, mod_w, mod_b, norm1_g, norm2_g, w_in, tshift_mu, w0, w_up, a0, a_up, g_up,
           k_k, k_a, r_k, lnx_g, lnx_b, conv_w, ret_norm_g, w_out, ffn_up, ffn_conv, ffn_down,
           final_norm_g):
    bsz, n, dm = x.shape
    depth = mod_w.shape[0]
    rwkv_w = k_k.shape[1]
    rwkv_cols = tshift_mu.shape[1]
    conv_cols = 3 * conv_w.shape[1]
    ret_heads = ret_norm_g.shape[1] // RET_HEAD_DIM
    npairs = rwkv_w // (2 * RWKV_HEAD_DIM)

    rows = n // GRID_W
    row = jnp.repeat(jnp.arange(rows, dtype=F32), GRID_W)
    col = jnp.tile(jnp.arange(GRID_W, dtype=F32), rows)
    pairs = RET_HEAD_DIM // 4
    freqs = 1.0 / (ROPE_BASE ** (jnp.arange(pairs, dtype=F32) / pairs))
    ang = jnp.concatenate([row[:, None] * freqs, col[:, None] * freqs], axis=-1)
    cos, sin = jnp.cos(ang), jnp.sin(ang)
    rope = (jnp.concatenate([cos, cos], axis=-1), jnp.concatenate([-sin, sin], axis=-1))
    hidx = jnp.arange(ret_heads, dtype=F32)
    lg_f = jnp.log1p(-jnp.exp2(-5.0 - hidx))
    lg_b = jnp.log1p(-jnp.exp2(-5.5 - hidx))
    s_zero = jnp.zeros((bsz, npairs, 2 * RWKV_HEAD_DIM, 2 * RWKV_HEAD_DIM), F32)
    r_zero = jnp.zeros((bsz, ret_heads, RET_HEAD_DIM, RET_HEAD_DIM), F32)

    xl = x.astype(F32)
    xc = ctx.astype(F32)
    silu_all = jnp.concatenate([jax.nn.silu(c.astype(F32)), jax.nn.silu(c_ctx.astype(F32))[None]], axis=0)

    w_in_b, w_out_b, up_b, down_b = (w.astype(BF16) for w in (w_in, w_out, ffn_up, ffn_down))
    widths = (rwkv_cols, conv_cols, w_in.shape[2] - rwkv_cols - conv_cols)
    z_dtypes = (F32, BF16, BF16)
    bc = lambda m: jnp.broadcast_to(m, (bsz, 1, dm))

    for l in range(depth):
        last = l == depth - 1
        mod = _mm_rows(silu_all, mod_w, l) + mod_b[l]
        ml = [m[:, None, :] for m in jnp.split(mod[:bsz], 6, axis=-1)]
        mc = jnp.split(mod[bsz], 6, axis=-1)
        rwkv_p = (tshift_mu[l], w0[l], w_up[l], a0[l], a_up[l], g_up[l], k_k[l], k_a[l],
                  r_k[l], lnx_g[l], lnx_b[l])

        gm_c, sh_c = bc(norm1_g[l] * (1.0 + mc[1])), bc(mc[0])
        gm_l, sh_l = norm1_g[l] * (1.0 + ml[1]), ml[0]
        ret_w = ret_heads * RET_HEAD_DIM
        zc_rw, zc_cv, zc_rt = _mm_norm(xc, gm_c, sh_c, w_in_b, l, widths, z_dtypes, ret_w)
        zl_rw, zl_cv, zl_rt = _mm_norm(xl, gm_l, sh_l, w_in_b, l, widths, z_dtypes, ret_w, rope)

        out_c_rw, s_f, s_b = _rwkv_mix(zc_rw, rwkv_p, s_zero, s_zero, not last)
        out_l_rw, _, _ = _rwkv_mix(zl_rw, rwkv_p, s_f, s_b, True)
        out_c_rt, r_f, r_b = _ret_mix(zc_rt, r_zero, r_zero, lg_f, lg_b, ret_norm_g[l], not last)
        out_l_rt, _, _ = _ret_mix(zl_rt, r_f, r_b, lg_f, lg_b, ret_norm_g[l], True)

        xl = _mix_out(out_l_rw, zl_cv, out_l_rt, xl, ml[2], w_out_b, l, conv_w[l])
        xl = _ffn(xl, norm2_g[l] * (1.0 + ml[4]), ml[3], ml[5], up_b, l, ffn_conv[l], down_b,
                  final_norm_g if last else None)
        if not last:
            xc = _mix_out(out_c_rw, zc_cv, out_c_rt, xc, bc(mc[2]), w_out_b, l, conv_w[l])
            xc = _ffn(xc, bc(norm2_g[l] * (1.0 + mc[4])), bc(mc[3]), bc(mc[5]), up_b, l, ffn_conv[l],
                      down_b)

    return xl.astype(x.dtype)
```

```python
import functools

import jax
import jax.numpy as jnp
from jax import lax
from jax.experimental import pallas as pl
from jax.experimental.pallas import tpu as pltpu

F32 = jnp.float32
BF16 = jnp.bfloat16

GRID_W = 64
RWKV_HEAD_DIM = 64
DECAY_LORA = 64
AAA_LORA = 64
GATE_LORA = 128
RET_HEAD_DIM = 128
RET_CHUNK = 128
RET_CHUNKS_PER_STEP = (8, 4)
WKV_CHUNK = 64
WKV_CHUNKS_PER_STEP = 8
ROPE_BASE = 10000.0
NORM_EPS = 1e-6
RWKV_GN_EPS = 64e-5
KEY_NORM_FLOOR = 1e-12

F32_SUBLANES = 8
VMEM_LIMIT_BYTES = 48 * 1024 * 1024
VMEM_LIMIT_IN_PROJ_BYTES = 56 * 1024 * 1024

_NT = (((1,), (1,)), ((), ()))
_TN = (((0,), (0,)), ((), ()))
_NN = (((1,), (0,)), ((), ()))


def _mm_kernel(a_ref, b_ref, o_ref):
    o_ref[...] = jnp.dot(a_ref[...].astype(BF16), b_ref[...].astype(BF16), preferred_element_type=F32)


def _pick_tile(n, candidates):
    for c in candidates:
        if n % c == 0:
            return c
    return n


def _chunks_per_step(nchunks, preferred):
    while nchunks % preferred:
        preferred //= 2
    return preferred


def _mm_rows(a, b_all, layer):
    m, k = a.shape
    n = b_all.shape[-1]
    mp = -(-m // F32_SUBLANES) * F32_SUBLANES
    a = jnp.pad(a, ((0, mp - m), (0, 0)))
    tn = _pick_tile(n, (512, 384, 256, 128))
    out = pl.pallas_call(
        _mm_kernel,
        grid=(n // tn,),
        in_specs=[pl.BlockSpec((mp, k), lambda j: (0, 0)),
                  pl.BlockSpec((None, k, tn), lambda j: (layer, 0, j))],
        out_specs=pl.BlockSpec((mp, tn), lambda j: (0, j)),
        out_shape=jax.ShapeDtypeStruct((mp, n), F32),
        compiler_params=pltpu.CompilerParams(
            dimension_semantics=("parallel",),
            vmem_limit_bytes=VMEM_LIMIT_BYTES),
        name="mm",
    )(a, b_all)
    return out[:m]


def _resident(shape, layer=None):
    if layer is None:
        return pl.BlockSpec(shape, lambda *_: (0,) * len(shape), pipeline_mode=pl.Buffered(1))
    return pl.BlockSpec((None,) + tuple(shape), lambda *_: (layer,) + (0,) * len(shape),
                        pipeline_mode=pl.Buffered(1))


def _mm_norm_kernel(x_ref, gm_ref, sh_ref, w_ref, *refs, use_rope, ret_w):
    if use_rope:
        cos_ref, sin_ref, *o_refs = refs
    else:
        o_refs = refs
    xb = x_ref[0]
    ms = jnp.mean(xb * xb, axis=-1, keepdims=True)
    h = (xb * lax.rsqrt(ms + NORM_EPS) * gm_ref[0] + sh_ref[0]).astype(BF16)
    start = 0
    for o_ref in o_refs:
        n = o_ref.shape[-1]
        z = jnp.dot(h, w_ref[:, start:start + n], preferred_element_type=F32)
        start += n
        if o_ref is o_refs[-1]:
            d = RET_HEAD_DIM
            for c0 in range(0, 2 * ret_w, d):
                u = z[:, c0:c0 + d]
                if use_rope:
                    u = u * cos_ref[...] + pltpu.roll(u, d // 2, axis=1) * sin_ref[...]
                if c0 < ret_w:
                    u = u * (d ** -0.5)
                o_ref[0, :, c0:c0 + d] = u.astype(o_ref.dtype)
            o_ref[0, :, 2 * ret_w:] = z[:, 2 * ret_w:].astype(o_ref.dtype)
        else:
            o_ref[0] = z.astype(o_ref.dtype)


def _mm_norm(x, gmod, shift, w_all, layer, widths, dtypes, ret_w, rope=None):
    bsz, t, dm = x.shape
    assert sum(widths) == w_all.shape[2] and widths[-1] == 4 * ret_w
    tm = _pick_tile(t, (256, 128))
    vec_spec = pl.BlockSpec((1, 1, dm), lambda b, i: (b, 0, 0))
    in_specs = [pl.BlockSpec((1, tm, dm), lambda b, i: (b, i, 0)), vec_spec, vec_spec,
                _resident(w_all.shape[1:], layer)]
    args = [x, gmod, shift, w_all]
    if rope is not None:
        in_specs += [pl.BlockSpec((tm, RET_HEAD_DIM), lambda b, i: (i, 0))] * 2
        args += list(rope)
    return pl.pallas_call(
        functools.partial(_mm_norm_kernel, use_rope=rope is not None, ret_w=ret_w),
        grid=(bsz, t // tm),
        in_specs=in_specs,
        out_specs=[pl.BlockSpec((1, tm, n), lambda b, i: (b, i, 0)) for n in widths],
        out_shape=[jax.ShapeDtypeStruct((bsz, t, n), dt) for n, dt in zip(widths, dtypes)],
        compiler_params=pltpu.CompilerParams(
            dimension_semantics=("parallel", "parallel"),
            vmem_limit_bytes=VMEM_LIMIT_IN_PROJ_BYTES),
        name="mm_norm",
    )(*args)


def _mix_out_kernel(rw_ref, cv_ref, cvp_ref, cvn_ref, rt_ref, x_ref, gate_ref, cw_ref, w_ref, o_ref, *, tm):
    i = pl.program_id(1)
    cw = cv_ref.shape[-1] // 3
    hl = cvp_ref.shape[1]
    zc = cv_ref[0].astype(F32)
    gch = zc[:, cw:2 * cw] * zc[:, 2 * cw:]
    row_before = cvp_ref[0].astype(F32)[hl - 1:hl]
    row_after = cvn_ref[0].astype(F32)[0:1]
    before = jnp.where(i == 0, 0.0, row_before[:, cw:2 * cw] * row_before[:, 2 * cw:])
    after = jnp.where(i == pl.num_programs(1) - 1, 0.0, row_after[:, cw:2 * cw] * row_after[:, 2 * cw:])
    trow = lax.broadcasted_iota(jnp.int32, (tm, 1), 0)
    prev = jnp.where(trow == 0, before, pltpu.roll(gch, 1, axis=0))
    nxt = jnp.where(trow == tm - 1, after, pltpu.roll(gch, tm - 1, axis=0))
    conv = zc[:, :cw] * (prev * cw_ref[0:1] + gch * cw_ref[1:2] + nxt * cw_ref[2:3])
    mix = jnp.concatenate([rw_ref[0].astype(BF16), conv.astype(BF16), rt_ref[0].astype(BF16)], axis=1)
    o_ref[0] = x_ref[0] + gate_ref[0] * jnp.dot(mix, w_ref[...], preferred_element_type=F32)


def _mix_out(out_rw, z_cv, out_rt, x, gate, w_out_all, layer, conv_w):
    bsz, t, dm = x.shape
    rw, cvw, rtw = out_rw.shape[-1], z_cv.shape[-1], out_rt.shape[-1]
    tm = _pick_tile(t, (512, 256, 128))
    hl = F32_SUBLANES * 4 // z_cv.dtype.itemsize
    per = tm // hl
    nhb = t // hl
    return pl.pallas_call(
        functools.partial(_mix_out_kernel, tm=tm),
        grid=(bsz, t // tm),
        in_specs=[
            pl.BlockSpec((1, tm, rw), lambda b, i: (b, i, 0)),
            pl.BlockSpec((1, tm, cvw), lambda b, i: (b, i, 0)),
            pl.BlockSpec((1, hl, cvw), lambda b, i: (b, jnp.maximum(i * per - 1, 0), 0)),
            pl.BlockSpec((1, hl, cvw), lambda b, i: (b, jnp.minimum((i + 1) * per, nhb - 1), 0)),
            pl.BlockSpec((1, tm, rtw), lambda b, i: (b, i, 0)),
            pl.BlockSpec((1, tm, dm), lambda b, i: (b, i, 0)),
            pl.BlockSpec((1, 1, dm), lambda b, i: (b, 0, 0)),
            _resident((3, cvw // 3)),
            _resident((rw + cvw // 3 + rtw, dm), layer),
        ],
        out_specs=pl.BlockSpec((1, tm, dm), lambda b, i: (b, i, 0)),
        out_shape=jax.ShapeDtypeStruct((bsz, t, dm), F32),
        compiler_params=pltpu.CompilerParams(
            dimension_semantics=("parallel", "parallel"),
            vmem_limit_bytes=VMEM_LIMIT_BYTES),
        name="mix_out",
    )(out_rw, z_cv, z_cv, z_cv, out_rt, x, gate, conv_w.T, w_out_all)


FFN_HALO = 2 * F32_SUBLANES


def _ffn_kernel(*refs, tm, final):
    if final:
        (x_ref, xp_ref, xn_ref, gm_ref, sh_ref, gate_ref, upg_ref, upv_ref, cwg_ref, cwv_ref, dn_ref,
         fg_ref, o_ref, h_scr, acc_scr) = refs
    else:
        (x_ref, xp_ref, xn_ref, gm_ref, sh_ref, gate_ref, upg_ref, upv_ref, cwg_ref, cwv_ref, dn_ref,
         o_ref, h_scr, acc_scr) = refs
    i = pl.program_id(1)
    j = pl.program_id(2)
    halo = FFN_HALO
    rows = tm + 2 * halo

    @pl.when(j == 0)
    def _():
        def modnorm(xb):
            ms = jnp.mean(xb * xb, axis=-1, keepdims=True)
            return xb * lax.rsqrt(ms + NORM_EPS) * gm_ref[0] + sh_ref[0]

        hp = jnp.where(i == 0, 0.0, modnorm(xp_ref[0]))
        hn = jnp.where(i == pl.num_programs(1) - 1, 0.0, modnorm(xn_ref[0]))
        h_scr[0:halo] = hp.astype(BF16)
        h_scr[halo:halo + tm] = modnorm(x_ref[0]).astype(BF16)
        h_scr[halo + tm:rows] = hn.astype(BF16)
        acc_scr[...] = jnp.zeros_like(acc_scr)

    h = h_scr[...]

    def conv_half(up_ref, cw_ref):
        pm = jnp.dot(h, up_ref[...], preferred_element_type=F32)
        cw = cw_ref[...]
        u = (pltpu.roll(pm, 1, axis=0) * cw[0:1] + pm * cw[1:2]
             + pltpu.roll(pm, rows - 1, axis=0) * cw[2:3])
        return u[halo:halo + tm]

    gt = conv_half(upg_ref, cwg_ref)
    vl = conv_half(upv_ref, cwv_ref)
    act = gt * jax.nn.sigmoid(gt) * vl
    acc_scr[...] += jnp.dot(act.astype(BF16), dn_ref[...], preferred_element_type=F32)

    @pl.when(j == pl.num_programs(2) - 1)
    def _():
        out = x_ref[0] + gate_ref[0] * acc_scr[...]
        if final:
            ms = jnp.mean(out * out, axis=-1, keepdims=True)
            out = out * lax.rsqrt(ms + NORM_EPS) * fg_ref[...]
        o_ref[0] = out


def _ffn(x, gmod, shift, gate, up, layer, cw, down, final_g=None):
    bsz, t, dm = x.shape
    f = down.shape[1]
    tm = _pick_tile(t, (512, 256, 128))
    tf = _pick_tile(f, (512, 256, 128))
    halo = FFN_HALO
    nf = f // tf
    per = tm // halo
    nhb = t // halo
    cwt = cw.T
    final = final_g is not None
    vec_spec = pl.BlockSpec((1, 1, dm), lambda b, i, j: (b, 0, 0))
    in_specs = [
        pl.BlockSpec((1, tm, dm), lambda b, i, j: (b, i, 0)),
        pl.BlockSpec((1, halo, dm), lambda b, i, j: (b, jnp.maximum(i * per - 1, 0), 0)),
        pl.BlockSpec((1, halo, dm), lambda b, i, j: (b, jnp.minimum((i + 1) * per, nhb - 1), 0)),
        vec_spec, vec_spec, vec_spec,
        pl.BlockSpec((None, dm, tf), lambda b, i, j: (layer, 0, j)),
        pl.BlockSpec((None, dm, tf), lambda b, i, j: (layer, 0, nf + j)),
        pl.BlockSpec((3, tf), lambda b, i, j: (0, j)),
        pl.BlockSpec((3, tf), lambda b, i, j: (0, nf + j)),
        pl.BlockSpec((None, tf, dm), lambda b, i, j: (layer, j, 0)),
    ]
    args = [x, x, x, gmod, shift, gate, up, up, cwt, cwt, down]
    if final:
        in_specs.append(pl.BlockSpec((1, dm), lambda b, i, j: (0, 0)))
        args.append(final_g.reshape(1, dm))
    return pl.pallas_call(
        functools.partial(_ffn_kernel, tm=tm, final=final),
        grid=(bsz, t // tm, nf),
        in_specs=in_specs,
        out_specs=pl.BlockSpec((1, tm, dm), lambda b, i, j: (b, i, 0)),
        out_shape=jax.ShapeDtypeStruct((bsz, t, dm), F32),
        scratch_shapes=[pltpu.VMEM((tm + 2 * halo, dm), BF16), pltpu.VMEM((tm, dm), F32)],
        compiler_params=pltpu.CompilerParams(
            dimension_semantics=("parallel", "parallel", "arbitrary"),
            vmem_limit_bytes=VMEM_LIMIT_BYTES),
        name="ffn",
    )(*args)


def _split_bf16(x):
    hi = x.astype(BF16)
    lo = (x - hi.astype(F32)).astype(BF16)
    return hi, lo


def _dot1(a, b, dn=_NN):
    return lax.dot_general(a.astype(BF16), b.astype(BF16), dn, preferred_element_type=F32)


def _wkv_kernel(*refs, reverse, npairs, chunk, nsub):
    if reverse:
        (feat_ref, ka_ref, w0_ref, a0_ref, lora_ref, s0_ref,
         yf_ref, a0f_ref, gup_ref, rk_ref, lng_ref, lnb_ref, y_ref, sT_ref, s_scr) = refs
    else:
        (z_ref, zp_ref, zn_ref, mu_ref, kk_ref, ka_ref, w0_ref, a0_ref, lora_ref, s0_ref,
         y_ref, feat_ref, sT_ref, s_scr) = refs
    c = chunk
    cb = nsub * c
    w2 = 2 * RWKV_HEAD_DIM
    rw = npairs * w2
    i = pl.program_id(1)
    nc = pl.num_programs(1)
    ci = nc - 1 - i if reverse else i

    @pl.when(i == 0)
    def _():
        s_scr[...] = s0_ref[0]

    row = lax.broadcasted_iota(jnp.int32, (cb, cb), 0)
    col = lax.broadcasted_iota(jnp.int32, (cb, cb), 1)
    same_chunk = row // c == col // c
    cum = (same_chunk & (row <= col if reverse else row >= col)).astype(BF16)
    last = 0 if reverse else c - 1
    first_head = lax.broadcasted_iota(jnp.int32, (c, w2), 1) < RWKV_HEAD_DIM
    first_head2 = jnp.concatenate([first_head, first_head], axis=1)
    brow = lax.broadcasted_iota(jnp.int32, (w2, w2), 0) < RWKV_HEAD_DIM
    bcol = lax.broadcasted_iota(jnp.int32, (w2, w2), 1) < RWKV_HEAD_DIM
    same_head = brow == bcol
    head_ones = same_head.astype(BF16)
    trow = lax.broadcasted_iota(jnp.int32, (c, 2 * c), 0)
    tcol = lax.broadcasted_iota(jnp.int32, (c, 2 * c), 1)
    tcol = jnp.where(tcol < c, tcol, tcol - c)
    eye_pair = jnp.where(trow == tcol, 1.0, 0.0)
    if reverse:
        strict, incl = trow < tcol, trow <= tcol
    else:
        strict, incl = trow > tcol, trow >= tcol
    pairs = range(npairs)
    sls = [slice(q * w2, (q + 1) * w2) for q in pairs]

    def head_sum(u):
        hi, lo = _split_bf16(u)
        ones2 = jnp.concatenate([head_ones, head_ones], axis=0)
        return jnp.concatenate(
            [jnp.dot(jnp.concatenate([hi[:, sl], lo[:, sl]], axis=1), ones2, preferred_element_type=F32)
             for sl in sls], axis=1)

    zw = 3 * rw + 2 * w2
    if reverse:
        z = feat_ref[0, :, :zw]
        kk = feat_ref[0, :, zw:]
    else:
        z = z_ref[0]
        trow = lax.broadcasted_iota(jnp.int32, (cb, 1), 0)
        z_before = jnp.where(ci == 0, 0.0, zp_ref[0, F32_SUBLANES - 1:F32_SUBLANES, :])
        z_after = jnp.where(ci == nc - 1, 0.0, zn_ref[0, 0:1, :])
        zprev = jnp.where(trow == 0, z_before, pltpu.roll(z, 1, axis=0))
        znext = jnp.where(trow == cb - 1, z_after, pltpu.roll(z, cb - 1, axis=0))
        z = z + (0.5 * (zprev + znext) - z) * mu_ref[...]
    r = z[:, :rw]
    k = z[:, rw:2 * rw]
    v_all = z[:, 2 * rw:3 * rw]
    wa = z[:, 3 * rw:3 * rw + w2]
    gd = z[:, 3 * rw + w2:]

    if not reverse:
        kkr = k * kk_ref[...]
        kk = kkr / jnp.maximum(jnp.sqrt(head_sum(kkr * kkr)), KEY_NORM_FLOOR)
        feat_ref[0, :, :zw] = z
        feat_ref[0, :, zw:] = kk
    decay_lane = lax.broadcasted_iota(jnp.int32, (cb, w2), 1) < DECAY_LORA
    lora = jnp.dot(jnp.where(decay_lane, jnp.tanh(wa), wa).astype(BF16), lora_ref[...],
                   preferred_element_type=F32)
    xw = w0_ref[...] + lora[:, :rw]
    w_log = -(jnp.maximum(-xw, 0.0) + jnp.log1p(jnp.exp(-jnp.abs(xw)))) - 0.5
    lw = -jnp.exp(w_log)
    rate = jax.nn.sigmoid(a0_ref[...] + lora[:, rw:2 * rw])
    key = k * (1.0 + (rate - 1.0) * ka_ref[...])

    lw_hi, lw_lo = _split_bf16(lw)
    cl = jnp.dot(jnp.concatenate([cum, cum], axis=1), jnp.concatenate([lw_hi, lw_lo], axis=0),
                 preferred_element_type=F32)
    p = jnp.exp(cl)
    pinv = jnp.exp(-cl)
    at_all = -kk * jnp.exp(cl - lw)
    bt_all = kk * rate * pinv
    kt_all = key * pinv
    rt_all = r * p

    def stack(u):
        m = first_head if u.shape[1] == w2 else first_head2
        zero = jnp.zeros_like(u)
        return jnp.concatenate([jnp.where(m, u, zero), jnp.where(m, zero, u)], axis=0)

    def dotb(a, b, dn=_NN):
        return lax.dot_general(a, b, dn, preferred_element_type=F32)

    def pdot(mat, u):
        return dotb(mat, stack(u))

    units = [(j, q) for j in range(nsub) for q in pairs]
    cut = lambda u: [u[j * c:(j + 1) * c, sls[q]] for j, q in units]
    rt = cut(rt_all)
    at, bt, kt, rtb, v = [cut(u.astype(BF16)) for u in (at_all, bt_all, kt_all, rt_all, v_all)]
    us = range(len(units))

    a_ab, a_ak, m_rb, m_rk = [], [], [], []
    for u in us:
        lhs = jnp.concatenate([at[u], rtb[u]], axis=0)
        mbk = dotb(lhs, jnp.concatenate([stack(bt[u]), stack(kt[u])], axis=0), _NT)
        a_ab.append(jnp.where(strict, mbk[:c, :2 * c], 0.0).astype(BF16))
        m_rb.append(jnp.where(incl, mbk[c:, :2 * c], 0.0).astype(BF16))
        a_ak.append(jnp.where(strict, mbk[:c, 2 * c:], 0.0).astype(BF16))
        m_rk.append(jnp.where(incl, mbk[c:, 2 * c:], 0.0).astype(BF16))

    tinv = [eye_pair + a_ab[u].astype(F32) for u in us]
    ak = [pdot(a_ab[u], a_ab[u]).astype(BF16) for u in us]
    nsteps = max(1, (c - 1).bit_length()) - 1
    for s in range(nsteps):
        if s + 1 < nsteps:
            both = [pdot(ak[u], jnp.concatenate([ak[u], tinv[u].astype(BF16)], axis=1)) for u in us]
            ak = [both[u][:, :2 * c].astype(BF16) for u in us]
            tinv = [tinv[u] + both[u][:, 2 * c:] for u in us]
        else:
            tinv = [tinv[u] + pdot(ak[u], tinv[u].astype(BF16)) for u in us]
    xu0 = [pdot(a_ak[u], v[u]).astype(BF16) for u in us]
    x = [pdot(tinv[u].astype(BF16), jnp.concatenate([at[u], xu0[u]], axis=1)).astype(BF16)
         for u in us]

    zero_v = [jnp.zeros_like(v[u]) for u in us]
    xv = [jnp.concatenate([zero_v[u], v[u]], axis=1) for u in us]
    ry = [dotb(jnp.concatenate([m_rb[u], m_rk[u]], axis=1),
               jnp.concatenate([stack(x[u]), stack(xv[u])], axis=0)) for u in us]
    rbar = [rt[u] + ry[u][:, :w2] for u in us]
    y0 = [ry[u][:, w2:] for u in us]
    gh = [dotb(jnp.concatenate([x[u], xv[u]], axis=0), jnp.concatenate([bt[u], kt[u]], axis=0), _TN)
          for u in us]
    g = [jnp.where(same_head, gh[u][:w2], 0.0) for u in us]
    hh = [jnp.where(same_head, gh[u][w2:], 0.0) for u in us]

    ys = [None] * len(units)
    state = [s_scr[q] for q in pairs]
    for j in (reversed(range(nsub)) if reverse else range(nsub)):
        for q in pairs:
            u = j * npairs + q
            s0 = state[q]
            ys[u] = _dot1(rbar[u], s0, _NT) + y0[u]
            state[q] = (s0 + _dot1(s0, g[u]) + hh[u]) * p[j * c + last:j * c + last + 1, sls[q]]
    for q in pairs:
        s_scr[q] = state[q]
    y = jnp.concatenate([jnp.concatenate(ys[j * npairs:(j + 1) * npairs], axis=1)
                         for j in range(nsub)], axis=0)

    if reverse:
        y = yf_ref[0] + y
        inv_n = 1.0 / RWKV_HEAD_DIM
        rate_f = jax.nn.sigmoid(a0f_ref[...] + lora[:, 2 * rw:])
        k_mid = 0.5 * (k * (1.0 + (rate_f - 1.0) * ka_ref[...]) + key)
        sums = head_sum(jnp.concatenate([y, r * k_mid * rk_ref[...]], axis=0))
        dev = y - sums[:cb] * inv_n
        yn = dev * lax.rsqrt(head_sum(dev * dev) * inv_n + RWKV_GN_EPS)
        bonus = sums[cb:] * v_all
        gate = jnp.dot(jax.nn.sigmoid(gd).astype(BF16), gup_ref[...], preferred_element_type=F32)
        y = (yn * lng_ref[...] + lnb_ref[...] + bonus) * gate
    y_ref[0] = y.astype(y_ref.dtype)

    @pl.when(i == pl.num_programs(1) - 1)
    def _():
        sT_ref[0] = s_scr[...]


def _wkv_scan(z, p, d, s0, y_fwd=None):
    mu, w0, w_up, a0, a_up, g_up, k_k, k_a, r_k, lnx_g, lnx_b = p
    bsz, t, zw = z.shape
    w = k_k.shape[0]
    w2 = 2 * RWKV_HEAD_DIM
    npairs = w // w2
    c = WKV_CHUNK
    assert c == RWKV_HEAD_DIM and t % c == 0 and w % w2 == 0
    nsub = _chunks_per_step(t // c, WKV_CHUNKS_PER_STEP)
    cb = nsub * c
    nc = t // cb
    reverse = d == 1
    assert DECAY_LORA + AAA_LORA == w2 and (zw if not reverse else zw - w) == 3 * w + w2 + GATE_LORA
    halo = F32_SUBLANES
    per = cb // halo
    nhb = t // halo
    pos = (lambda i: nc - 1 - i) if reverse else (lambda i: i)
    tok = lambda bi, i: (bi, pos(i), 0)
    row1 = lambda u: u.reshape(1, -1)
    no_decay, no_rate = jnp.zeros((DECAY_LORA, w), F32), jnp.zeros((AAA_LORA, w), F32)
    blocks = [jnp.concatenate([w_up[d], no_rate], axis=0), jnp.concatenate([no_decay, a_up[d]], axis=0)]
    if reverse:
        blocks.append(jnp.concatenate([no_decay, a_up[0]], axis=0))
    lora_w = jnp.concatenate(blocks, axis=1).astype(BF16)
    full = lambda arr: pl.BlockSpec(arr.shape, lambda bi, i: (0, 0))
    st_spec = pl.BlockSpec((1, npairs, w2, w2), lambda bi, i: (bi, 0, 0, 0))
    y_spec = pl.BlockSpec((1, cb, w), tok)
    common = [row1(k_a), row1(w0[d]), row1(a0[d]), lora_w]
    out_specs = [y_spec, st_spec]
    out_shape = [jax.ShapeDtypeStruct((bsz, t, w), BF16 if reverse else F32),
                 jax.ShapeDtypeStruct((bsz, npairs, w2, w2), F32)]
    if reverse:
        extra = [row1(a0[0]), g_up.astype(BF16), row1(r_k), row1(lnx_g), row1(lnx_b)]
        args = [z] + common + [s0, y_fwd] + extra
        in_specs = ([pl.BlockSpec((1, cb, zw), tok)] + [full(a) for a in common] + [st_spec, y_spec]
                    + [full(a) for a in extra])
    else:
        head = [row1(mu), row1(k_k)]
        args = [z, z, z] + head + common + [s0]
        in_specs = [
            pl.BlockSpec((1, cb, zw), tok),
            pl.BlockSpec((1, halo, zw), lambda bi, i: (bi, jnp.maximum(pos(i) * per - 1, 0), 0)),
            pl.BlockSpec((1, halo, zw), lambda bi, i: (bi, jnp.minimum((pos(i) + 1) * per, nhb - 1), 0)),
        ] + [full(a) for a in head + common] + [st_spec]
        out_specs.insert(1, pl.BlockSpec((1, cb, zw + w), tok))
        out_shape.insert(1, jax.ShapeDtypeStruct((bsz, t, zw + w), F32))
    return pl.pallas_call(
        functools.partial(_wkv_kernel, reverse=reverse, npairs=npairs, chunk=c, nsub=nsub),
        grid=(bsz, nc),
        in_specs=in_specs,
        out_specs=out_specs,
        out_shape=out_shape,
        scratch_shapes=[pltpu.VMEM((npairs, w2, w2), F32)],
        compiler_params=pltpu.CompilerParams(
            dimension_semantics=("parallel", "arbitrary"),
            vmem_limit_bytes=VMEM_LIMIT_BYTES),
        name="wkv_bwd" if reverse else "wkv_fwd",
    )(*args)


def _ret_kernel(*refs, nheads, reverse, nsub):
    refs = list(refs)
    z_ref = refs.pop(0)
    din_ref, xi_ref, zeta_ref, g_ref, r0_ref = refs[:5]
    refs = refs[5:]
    yf_ref, ng_ref = (refs.pop(0), refs.pop(0)) if reverse else (None, None)
    y_ref, rT_ref, r_scr = refs
    d = RET_HEAD_DIM
    w = nheads * d
    i = pl.program_id(1)

    @pl.when(i == 0)
    def _():
        r_scr[...] = r0_ref[0]

    c = RET_CHUNK
    units = [(j, h) for j in range(nsub) for h in range(nheads)]
    us = range(len(units))
    rows = lambda j: slice(j * c, (j + 1) * c)
    cols = lambda g, h: slice(g * w + h * d, g * w + (h + 1) * d)

    q = [z_ref[0, rows(j), cols(0, h)] for j, h in units]
    k = [z_ref[0, rows(j), cols(1, h)] for j, h in units]
    v = [z_ref[0, rows(j), cols(2, h)] for j, h in units]
    s = [_dot1(q[u], k[u], _NT) * din_ref[units[u][1]] for u in us]
    sv = [_dot1(s[u], v[u]) for u in us]
    kv = [_dot1(k[u].astype(F32) * zeta_ref[units[u][1]], v[u], _TN) for u in us]

    ys = [None] * len(units)
    state = [r_scr[h] for h in range(nheads)]
    for j in (reversed(range(nsub)) if reverse else range(nsub)):
        for h in range(nheads):
            u = j * nheads + h
            ys[u] = sv[u] + _dot1(q[u], state[h]) * xi_ref[h]
            state[h] = state[h] * g_ref[h] + kv[u]
    for h in range(nheads):
        r_scr[h] = state[h]

    for u, (j, h) in enumerate(units):
        y = ys[u]
        if reverse:
            y = yf_ref[0, rows(j), cols(0, h)].astype(F32) + y
            dev = y - jnp.mean(y, axis=-1, keepdims=True)
            yn = dev * lax.rsqrt(jnp.mean(dev * dev, axis=-1, keepdims=True) + NORM_EPS)
            gt = z_ref[0, rows(j), cols(3, h)].astype(F32)
            y = gt * jax.nn.sigmoid(gt) * (yn * ng_ref[:, cols(0, h)])
        y_ref[0, rows(j), cols(0, h)] = y.astype(y_ref.dtype)

    @pl.when(i == pl.num_programs(1) - 1)
    def _():
        rT_ref[0] = r_scr[...]


def _ret_tables(log_gamma, reverse):
    c, d = RET_CHUNK, RET_HEAD_DIM
    idx = jnp.arange(c, dtype=F32)
    lg = log_gamma[:, None, None]
    if reverse:
        diff = idx[None, :] - idx[:, None]
        keep = diff > 0
        xi = jnp.exp(log_gamma[:, None] * (c - idx))
        zeta = jnp.exp(log_gamma[:, None] * idx)
    else:
        diff = idx[:, None] - idx[None, :]
        keep = diff >= 0
        xi = jnp.exp(log_gamma[:, None] * (idx + 1.0))
        zeta = jnp.exp(log_gamma[:, None] * (c - 1.0 - idx))
    din = jnp.where(keep, jnp.exp(lg * jnp.maximum(diff, 0.0)), 0.0)
    nh = log_gamma.shape[0]
    xi = jnp.broadcast_to(xi[:, :, None], (nh, c, d))
    zeta = jnp.broadcast_to(zeta[:, :, None], (nh, c, d))
    g = jnp.broadcast_to(jnp.exp(log_gamma * c)[:, None, None], (nh, d, d))
    return din, xi, zeta, g


def _ret_scan(z, r0, log_gamma, reverse, y_fwd=None, norm_g=None):
    bsz, t, zw = z.shape
    d = RET_HEAD_DIM
    nh = zw // (4 * d)
    w = nh * d
    c = RET_CHUNK
    assert t % c == 0
    nsub = _chunks_per_step(t // c, RET_CHUNKS_PER_STEP[1 if reverse else 0])
    cb = nsub * c
    nc = t // cb
    din, xi, zeta, g = _ret_tables(log_gamma, reverse)
    pos = (lambda i: nc - 1 - i) if reverse else (lambda i: i)
    tok = lambda bi, i: (bi, pos(i), 0)
    y_spec = pl.BlockSpec((1, cb, w), tok)
    st_spec = pl.BlockSpec((1, nh, d, d), lambda bi, i: (bi, 0, 0, 0))
    tab_spec = lambda n: pl.BlockSpec((nh, n, d), lambda bi, i: (0, 0, 0))
    args = [z]
    in_specs = [pl.BlockSpec((1, cb, zw), tok)]
    args += [din, xi, zeta, g, r0]
    in_specs += [tab_spec(c), tab_spec(c), tab_spec(c), tab_spec(d), st_spec]
    if reverse:
        args += [y_fwd, norm_g.reshape(1, w)]
        in_specs += [y_spec, pl.BlockSpec((1, w), lambda bi, i: (0, 0))]
    return pl.pallas_call(
        functools.partial(_ret_kernel, nheads=nh, reverse=reverse, nsub=nsub),
        grid=(bsz, nc),
        in_specs=in_specs,
        out_specs=[y_spec, st_spec],
        out_shape=[jax.ShapeDtypeStruct((bsz, t, w), BF16),
                   jax.ShapeDtypeStruct((bsz, nh, d, d), F32)],
        scratch_shapes=[pltpu.VMEM((nh, d, d), F32)],
        compiler_params=pltpu.CompilerParams(
            dimension_semantics=("parallel", "arbitrary"),
            vmem_limit_bytes=VMEM_LIMIT_BYTES),
        name="ret_bwd" if reverse else "ret_fwd",
    )(*args)


def _rwkv_mix(z, p, s0_f, s0_b, want_out):
    y_f, feat, s_f = _wkv_scan(z, p, 0, s0_f)
    out, s_b = _wkv_scan(feat, p, 1, s0_b, y_f)
    return (out if want_out else None), s_f, s_b


def _ret_mix(z, r0_f, r0_b, lg_f, lg_b, ret_norm_g, want_out):
    y_f, r_f = _ret_scan(z, r0_f, lg_f, False)
    out, r_b = _ret_scan(z, r0_b, lg_b, True, y_f, ret_norm_g)
    return (out if want_out else None), r_f, r_b


def kernel(x, c, ctx, c_ctx, mod_w, mod_b, norm1_g, norm2_g, w_in, tshift_mu, w0, w_up, a0, a_up, g_up,
           k_k, k_a, r_k, lnx_g, lnx_b, conv_w, ret_norm_g, w_out, ffn_up, ffn_conv, ffn_down,
           final_norm_g):
    bsz, n, dm = x.shape
    depth = mod_w.shape[0]
    rwkv_w = k_k.shape[1]
    rwkv_cols = tshift_mu.shape[1]
    conv_cols = 3 * conv_w.shape[1]
    ret_heads = ret_norm_g.shape[1] // RET_HEAD_DIM
    npairs = rwkv_w // (2 * RWKV_HEAD_DIM)

    rows = n // GRID_W
    row = jnp.repeat(jnp.arange(rows, dtype=F32), GRID_W)
    col = jnp.tile(jnp.arange(GRID_W, dtype=F32), rows)
    pairs = RET_HEAD_DIM // 4
    freqs = 1.0 / (ROPE_BASE ** (jnp.arange(pairs, dtype=F32) / pairs))
    ang = jnp.concatenate([row[:, None] * freqs, col[:, None] * freqs], axis=-1)
    cos, sin = jnp.cos(ang), jnp.sin(ang)
    rope = (jnp.concatenate([cos, cos], axis=-1), jnp.concatenate([-sin, sin], axis=-1))
    hidx = jnp.arange(ret_heads, dtype=F32)
    lg_f = jnp.log1p(-jnp.exp2(-5.0 - hidx))
    lg_b = jnp.log1p(-jnp.exp2(-5.5 - hidx))
    s_zero = jnp.zeros((bsz, npairs, 2 * RWKV_HEAD_DIM, 2 * RWKV_HEAD_DIM), F32)
    r_zero = jnp.zeros((bsz, ret_heads, RET_HEAD_DIM, RET_HEAD_DIM), F32)

    xl = x.astype(F32)
    xc = ctx.astype(F32)
    silu_all = jnp.concatenate([jax.nn.silu(c.astype(F32)), jax.nn.silu(c_ctx.astype(F32))[None]], axis=0)

    w_in_b, w_out_b, up_b, down_b = (w.astype(BF16) for w in (w_in, w_out, ffn_up, ffn_down))
    widths = (rwkv_cols, conv_cols, w_in.shape[2] - rwkv_cols - conv_cols)
    z_dtypes = (F32, BF16, BF16)
    bc = lambda m: jnp.broadcast_to(m, (bsz, 1, dm))

    for l in range(depth):
        last = l == depth - 1
        mod = _mm_rows(silu_all, mod_w, l) + mod_b[l]
        ml = [m[:, None, :] for m in jnp.split(mod[:bsz], 6, axis=-1)]
        mc = jnp.split(mod[bsz], 6, axis=-1)
        rwkv_p = (tshift_mu[l], w0[l], w_up[l], a0[l], a_up[l], g_up[l], k_k[l], k_a[l],
                  r_k[l], lnx_g[l], lnx_b[l])

        gm_c, sh_c = bc(norm1_g[l] * (1.0 + mc[1])), bc(mc[0])
        gm_l, sh_l = norm1_g[l] * (1.0 + ml[1]), ml[0]
        ret_w = ret_heads * RET_HEAD_DIM
        zc_rw, zc_cv, zc_rt = _mm_norm(xc, gm_c, sh_c, w_in_b, l, widths, z_dtypes, ret_w)
        zl_rw, zl_cv, zl_rt = _mm_norm(xl, gm_l, sh_l, w_in_b, l, widths, z_dtypes, ret_w, rope)

        out_c_rw, s_f, s_b = _rwkv_mix(zc_rw, rwkv_p, s_zero, s_zero, not last)
        out_l_rw, _, _ = _rwkv_mix(zl_rw, rwkv_p, s_f, s_b, True)
        out_c_rt, r_f, r_b = _ret_mix(zc_rt, r_zero, r_zero, lg_f, lg_b, ret_norm_g[l], not last)
        out_l_rt, _, _ = _ret_mix(zl_rt, r_f, r_b, lg_f, lg_b, ret_norm_g[l], True)

        xl = _mix_out(out_l_rw, zl_cv, out_l_rt, xl, ml[2], w_out_b, l, conv_w[l])
        xl = _ffn(xl, norm2_g[l] * (1.0 + ml[4]), ml[3], ml[5], up_b, l, ffn_conv[l], down_b,
                  final_norm_g if last else None)
        if not last:
            xc = _mix_out(out_c_rw, zc_cv, out_c_rt, xc, bc(mc[2]), w_out_b, l, conv_w[l])
            xc = _ffn(xc, bc(norm2_g[l] * (1.0 + mc[4])), bc(mc[3]), bc(mc[5]), up_b, l, ffn_conv[l],
                      down_b)

    return xl.astype(x.dtype)
```

```python
import functools

import jax
import jax.numpy as jnp
from jax import lax
from jax.experimental import pallas as pl
from jax.experimental.pallas import tpu as pltpu

F32 = jnp.float32
BF16 = jnp.bfloat16

GRID_W = 64
RWKV_HEAD_DIM = 64
DECAY_LORA = 64
AAA_LORA = 64
GATE_LORA = 128
RET_HEAD_DIM = 128
RET_CHUNK = 128
RET_CHUNKS_PER_STEP = (8, 4)
WKV_CHUNK = 64
WKV_CHUNKS_PER_STEP = 8
ROPE_BASE = 10000.0
NORM_EPS = 1e-6
RWKV_GN_EPS = 64e-5
KEY_NORM_FLOOR = 1e-12

F32_SUBLANES = 8
VMEM_LIMIT_BYTES = 48 * 1024 * 1024
VMEM_LIMIT_IN_PROJ_BYTES = 56 * 1024 * 1024

_NT = (((1,), (1,)), ((), ()))
_TN = (((0,), (0,)), ((), ()))
_NN = (((1,), (0,)), ((), ()))


def _mm_kernel(a_ref, b_ref, o_ref):
    o_ref[...] = jnp.dot(a_ref[...].astype(BF16), b_ref[...].astype(BF16), preferred_element_type=F32)


def _pick_tile(n, candidates):
    for c in candidates:
        if n % c == 0:
            return c
    return n


def _chunks_per_step(nchunks, preferred):
    while nchunks % preferred:
        preferred //= 2
    return preferred


def _mm_rows(a, b_all, layer):
    m, k = a.shape
    n = b_all.shape[-1]
    mp = -(-m // F32_SUBLANES) * F32_SUBLANES
    a = jnp.pad(a, ((0, mp - m), (0, 0)))
    tn = _pick_tile(n, (512, 384, 256, 128))
    out = pl.pallas_call(
        _mm_kernel,
        grid=(n // tn,),
        in_specs=[pl.BlockSpec((mp, k), lambda j: (0, 0)),
                  pl.BlockSpec((None, k, tn), lambda j: (layer, 0, j))],
        out_specs=pl.BlockSpec((mp, tn), lambda j: (0, j)),
        out_shape=jax.ShapeDtypeStruct((mp, n), F32),
        compiler_params=pltpu.CompilerParams(
            dimension_semantics=("parallel",),
            vmem_limit_bytes=VMEM_LIMIT_BYTES),
        name="mm",
    )(a, b_all)
    return out[:m]


def _resident(shape, layer=None):
    if layer is None:
        return pl.BlockSpec(shape, lambda *_: (0,) * len(shape), pipeline_mode=pl.Buffered(1))
    return pl.BlockSpec((None,) + tuple(shape), lambda *_: (layer,) + (0,) * len(shape),
                        pipeline_mode=pl.Buffered(1))


def _mm_norm_kernel(x_ref, gm_ref, sh_ref, w_ref, *refs, use_rope, ret_w):
    if use_rope:
        cos_ref, sin_ref, *o_refs = refs
    else:
        o_refs = refs
    xb = x_ref[0]
    ms = jnp.mean(xb * xb, axis=-1, keepdims=True)
    h = (xb * lax.rsqrt(ms + NORM_EPS) * gm_ref[0] + sh_ref[0]).astype(BF16)
    start = 0
    for o_ref in o_refs:
        n = o_ref.shape[-1]
        z = jnp.dot(h, w_ref[:, start:start + n], preferred_element_type=F32)
        start += n
        if o_ref is o_refs[-1]:
            d = RET_HEAD_DIM
            for c0 in range(0, 2 * ret_w, d):
                u = z[:, c0:c0 + d]
                if use_rope:
                    u = u * cos_ref[...] + pltpu.roll(u, d // 2, axis=1) * sin_ref[...]
                if c0 < ret_w:
                    u = u * (d ** -0.5)
                o_ref[0, :, c0:c0 + d] = u.astype(o_ref.dtype)
            o_ref[0, :, 2 * ret_w:] = z[:, 2 * ret_w:].astype(o_ref.dtype)
        else:
            o_ref[0] = z.astype(o_ref.dtype)


def _mm_norm(x, gmod, shift, w_all, layer, widths, dtypes, ret_w, rope=None):
    bsz, t, dm = x.shape
    assert sum(widths) == w_all.shape[2] and widths[-1] == 4 * ret_w
    tm = _pick_tile(t, (256, 128))
    vec_spec = pl.BlockSpec((1, 1, dm), lambda b, i: (b, 0, 0))
    in_specs = [pl.BlockSpec((1, tm, dm), lambda b, i: (b, i, 0)), vec_spec, vec_spec,
                _resident(w_all.shape[1:], layer)]
    args = [x, gmod, shift, w_all]
    if rope is not None:
        in_specs += [pl.BlockSpec((tm, RET_HEAD_DIM), lambda b, i: (i, 0))] * 2
        args += list(rope)
    return pl.pallas_call(
        functools.partial(_mm_norm_kernel, use_rope=rope is not None, ret_w=ret_w),
        grid=(bsz, t // tm),
        in_specs=in_specs,
        out_specs=[pl.BlockSpec((1, tm, n), lambda b, i: (b, i, 0)) for n in widths],
        out_shape=[jax.ShapeDtypeStruct((bsz, t, n), dt) for n, dt in zip(widths, dtypes)],
        compiler_params=pltpu.CompilerParams(
            dimension_semantics=("parallel", "parallel"),
            vmem_limit_bytes=VMEM_LIMIT_IN_PROJ_BYTES),
        name="mm_norm",
    )(*args)


def _mix_out_kernel(rw_ref, cv_ref, cvp_ref, cvn_ref, rt_ref, x_ref, gate_ref, cw_ref, w_ref, o_ref, *, tm):
    i = pl.program_id(1)
    cw = cv_ref.shape[-1] // 3
    hl = cvp_ref.shape[1]
    zc = cv_ref[0].astype(F32)
    gch = zc[:, cw:2 * cw] * zc[:, 2 * cw:]
    row_before = cvp_ref[0].astype(F32)[hl - 1:hl]
    row_after = cvn_ref[0].astype(F32)[0:1]
    before = jnp.where(i == 0, 0.0, row_before[:, cw:2 * cw] * row_before[:, 2 * cw:])
    after = jnp.where(i == pl.num_programs(1) - 1, 0.0, row_after[:, cw:2 * cw] * row_after[:, 2 * cw:])
    trow = lax.broadcasted_iota(jnp.int32, (tm, 1), 0)
    prev = jnp.where(trow == 0, before, pltpu.roll(gch, 1, axis=0))
    nxt = jnp.where(trow == tm - 1, after, pltpu.roll(gch, tm - 1, axis=0))
    conv = zc[:, :cw] * (prev * cw_ref[0:1] + gch * cw_ref[1:2] + nxt * cw_ref[2:3])
    mix = jnp.concatenate([rw_ref[0].astype(BF16), conv.astype(BF16), rt_ref[0].astype(BF16)], axis=1)
    o_ref[0] = x_ref[0] + gate_ref[0] * jnp.dot(mix, w_ref[...], preferred_element_type=F32)


def _mix_out(out_rw, z_cv, out_rt, x, gate, w_out_all, layer, conv_w):
    bsz, t, dm = x.shape
    rw, cvw, rtw = out_rw.shape[-1], z_cv.shape[-1], out_rt.shape[-1]
    tm = _pick_tile(t, (512, 256, 128))
    hl = F32_SUBLANES * 4 // z_cv.dtype.itemsize
    per = tm // hl
    nhb = t // hl
    return pl.pallas_call(
        functools.partial(_mix_out_kernel, tm=tm),
        grid=(bsz, t // tm),
        in_specs=[
            pl.BlockSpec((1, tm, rw), lambda b, i: (b, i, 0)),
            pl.BlockSpec((1, tm, cvw), lambda b, i: (b, i, 0)),
            pl.BlockSpec((1, hl, cvw), lambda b, i: (b, jnp.maximum(i * per - 1, 0), 0)),
            pl.BlockSpec((1, hl, cvw), lambda b, i: (b, jnp.minimum((i + 1) * per, nhb - 1), 0)),
            pl.BlockSpec((1, tm, rtw), lambda b, i: (b, i, 0)),
            pl.BlockSpec((1, tm, dm), lambda b, i: (b, i, 0)),
            pl.BlockSpec((1, 1, dm), lambda b, i: (b, 0, 0)),
            _resident((3, cvw // 3)),
            _resident((rw + cvw // 3 + rtw, dm), layer),
        ],
        out_specs=pl.BlockSpec((1, tm, dm), lambda b, i: (b, i, 0)),
        out_shape=jax.ShapeDtypeStruct((bsz, t, dm), F32),
        compiler_params=pltpu.CompilerParams(
            dimension_semantics=("parallel", "parallel"),
            vmem_limit_bytes=VMEM_LIMIT_BYTES),
        name="mix_out",
    )(out_rw, z_cv, z_cv, z_cv, out_rt, x, gate, conv_w.T, w_out_all)


FFN_HALO = 2 * F32_SUBLANES


def _ffn_kernel(*refs, tm, final):
    if final:
        (x_ref, xp_ref, xn_ref, gm_ref, sh_ref, gate_ref, upg_ref, upv_ref, cwg_ref, cwv_ref, dn_ref,
         fg_ref, o_ref, h_scr, acc_scr) = refs
    else:
        (x_ref, xp_ref, xn_ref, gm_ref, sh_ref, gate_ref, upg_ref, upv_ref, cwg_ref, cwv_ref, dn_ref,
         o_ref, h_scr, acc_scr) = refs
    i = pl.program_id(1)
    j = pl.program_id(2)
    halo = FFN_HALO
    rows = tm + 2 * halo

    @pl.when(j == 0)
    def _():
        def modnorm(xb):
            ms = jnp.mean(xb * xb, axis=-1, keepdims=True)
            return xb * lax.rsqrt(ms + NORM_EPS) * gm_ref[0] + sh_ref[0]

        hp = jnp.where(i == 0, 0.0, modnorm(xp_ref[0]))
        hn = jnp.where(i == pl.num_programs(1) - 1, 0.0, modnorm(xn_ref[0]))
        h_scr[0:halo] = hp.astype(BF16)
        h_scr[halo:halo + tm] = modnorm(x_ref[0]).astype(BF16)
        h_scr[halo + tm:rows] = hn.astype(BF16)

    def hidden_tile_product():
        h = h_scr[...]

        def conv_half(up_ref, cw_ref):
            pm = jnp.dot(h, up_ref[...], preferred_element_type=F32)
            cw = cw_ref[...]
            u = (pltpu.roll(pm, 1, axis=0) * cw[0:1] + pm * cw[1:2]
                 + pltpu.roll(pm, rows - 1, axis=0) * cw[2:3])
            return u[halo:halo + tm]

        gt = conv_half(upg_ref, cwg_ref)
        vl = conv_half(upv_ref, cwv_ref)
        act = gt * (0.5 * jnp.tanh(0.5 * gt) + 0.5) * vl
        return jnp.dot(act.astype(BF16), dn_ref[...], preferred_element_type=F32)

    @pl.when(j == 0)
    def _():
        acc_scr[...] = hidden_tile_product()

    @pl.when(j > 0)
    def _():
        acc_scr[...] += hidden_tile_product()

    @pl.when(j == pl.num_programs(2) - 1)
    def _():
        out = x_ref[0] + gate_ref[0] * acc_scr[...]
        if final:
            ms = jnp.mean(out * out, axis=-1, keepdims=True)
            out = out * lax.rsqrt(ms + NORM_EPS) * fg_ref[...]
        o_ref[0] = out


def _ffn(x, gmod, shift, gate, up, layer, cw, down, final_g=None):
    bsz, t, dm = x.shape
    f = down.shape[1]
    tm = _pick_tile(t, (512, 256, 128))
    tf = _pick_tile(f, (512, 256, 128))
    halo = FFN_HALO
    nf = f // tf
    per = tm // halo
    nhb = t // halo
    cwt = cw.T
    final = final_g is not None
    vec_spec = pl.BlockSpec((1, 1, dm), lambda b, i, j: (b, 0, 0))
    in_specs = [
        pl.BlockSpec((1, tm, dm), lambda b, i, j: (b, i, 0)),
        pl.BlockSpec((1, halo, dm), lambda b, i, j: (b, jnp.maximum(i * per - 1, 0), 0)),
        pl.BlockSpec((1, halo, dm), lambda b, i, j: (b, jnp.minimum((i + 1) * per, nhb - 1), 0)),
        vec_spec, vec_spec, vec_spec,
        pl.BlockSpec((None, dm, tf), lambda b, i, j: (layer, 0, j)),
        pl.BlockSpec((None, dm, tf), lambda b, i, j: (layer, 0, nf + j)),
        pl.BlockSpec((3, tf), lambda b, i, j: (0, j)),
        pl.BlockSpec((3, tf), lambda b, i, j: (0, nf + j)),
        pl.BlockSpec((None, tf, dm), lambda b, i, j: (layer, j, 0)),
    ]
    args = [x, x, x, gmod, shift, gate, up, up, cwt, cwt, down]
    if final:
        in_specs.append(pl.BlockSpec((1, dm), lambda b, i, j: (0, 0)))
        args.append(final_g.reshape(1, dm))
    return pl.pallas_call(
        functools.partial(_ffn_kernel, tm=tm, final=final),
        grid=(bsz, t // tm, nf),
        in_specs=in_specs,
        out_specs=pl.BlockSpec((1, tm, dm), lambda b, i, j: (b, i, 0)),
        out_shape=jax.ShapeDtypeStruct((bsz, t, dm), F32),
        scratch_shapes=[pltpu.VMEM((tm + 2 * halo, dm), BF16), pltpu.VMEM((tm, dm), F32)],
        compiler_params=pltpu.CompilerParams(
            dimension_semantics=("parallel", "parallel", "arbitrary"),
            vmem_limit_bytes=VMEM_LIMIT_BYTES),
        name="ffn",
    )(*args)


def _split_bf16(x):
    hi = x.astype(BF16)
    lo = (x - hi.astype(F32)).astype(BF16)
    return hi, lo


def _dot1(a, b, dn=_NN):
    return lax.dot_general(a.astype(BF16), b.astype(BF16), dn, preferred_element_type=F32)


def _wkv_kernel(*refs, reverse, npairs, chunk, nsub):
    if reverse:
        (feat_ref, ka_ref, w0_ref, a0_ref, lora_ref, s0_ref,
         yf_ref, a0f_ref, gup_ref, rk_ref, lng_ref, lnb_ref, y_ref, sT_ref, s_scr) = refs
    else:
        (z_ref, zp_ref, zn_ref, mu_ref, kk_ref, ka_ref, w0_ref, a0_ref, lora_ref, s0_ref,
         y_ref, feat_ref, sT_ref, s_scr) = refs
    c = chunk
    cb = nsub * c
    w2 = 2 * RWKV_HEAD_DIM
    rw = npairs * w2
    i = pl.program_id(1)
    nc = pl.num_programs(1)
    ci = nc - 1 - i if reverse else i

    @pl.when(i == 0)
    def _():
        s_scr[...] = s0_ref[0]

    row = lax.broadcasted_iota(jnp.int32, (cb, cb), 0)
    col = lax.broadcasted_iota(jnp.int32, (cb, cb), 1)
    same_chunk = row // c == col // c
    cum = (same_chunk & (row <= col if reverse else row >= col)).astype(BF16)
    last = 0 if reverse else c - 1
    first_head = lax.broadcasted_iota(jnp.int32, (c, w2), 1) < RWKV_HEAD_DIM
    first_head2 = jnp.concatenate([first_head, first_head], axis=1)
    brow = lax.broadcasted_iota(jnp.int32, (w2, w2), 0) < RWKV_HEAD_DIM
    bcol = lax.broadcasted_iota(jnp.int32, (w2, w2), 1) < RWKV_HEAD_DIM
    same_head = brow == bcol
    head_ones = same_head.astype(BF16)
    trow = lax.broadcasted_iota(jnp.int32, (c, 2 * c), 0)
    tcol = lax.broadcasted_iota(jnp.int32, (c, 2 * c), 1)
    tcol = jnp.where(tcol < c, tcol, tcol - c)
    eye_pair = jnp.where(trow == tcol, 1.0, 0.0)
    if reverse:
        strict, incl = trow < tcol, trow <= tcol
    else:
        strict, incl = trow > tcol, trow >= tcol
    pairs = range(npairs)
    sls = [slice(q * w2, (q + 1) * w2) for q in pairs]

    def head_sum(u):
        hi, lo = _split_bf16(u)
        ones2 = jnp.concatenate([head_ones, head_ones], axis=0)
        return jnp.concatenate(
            [jnp.dot(jnp.concatenate([hi[:, sl], lo[:, sl]], axis=1), ones2, preferred_element_type=F32)
             for sl in sls], axis=1)

    zw = 3 * rw + 2 * w2
    if reverse:
        z = feat_ref[0, :, :zw]
        kk = feat_ref[0, :, zw:]
    else:
        z = z_ref[0]
        trow = lax.broadcasted_iota(jnp.int32, (cb, 1), 0)
        z_before = jnp.where(ci == 0, 0.0, zp_ref[0, F32_SUBLANES - 1:F32_SUBLANES, :])
        z_after = jnp.where(ci == nc - 1, 0.0, zn_ref[0, 0:1, :])
        zprev = jnp.where(trow == 0, z_before, pltpu.roll(z, 1, axis=0))
        znext = jnp.where(trow == cb - 1, z_after, pltpu.roll(z, cb - 1, axis=0))
        z = z + (0.5 * (zprev + znext) - z) * mu_ref[...]
    r = z[:, :rw]
    k = z[:, rw:2 * rw]
    v_all = z[:, 2 * rw:3 * rw]
    wa = z[:, 3 * rw:3 * rw + w2]
    gd = z[:, 3 * rw + w2:]

    if not reverse:
        kkr = k * kk_ref[...]
        kk = kkr / jnp.maximum(jnp.sqrt(head_sum(kkr * kkr)), KEY_NORM_FLOOR)
        feat_ref[0, :, :zw] = z
        feat_ref[0, :, zw:] = kk
    decay_lane = lax.broadcasted_iota(jnp.int32, (cb, w2), 1) < DECAY_LORA
    lora = jnp.dot(jnp.where(decay_lane, jnp.tanh(wa), wa).astype(BF16), lora_ref[...],
                   preferred_element_type=F32)
    xw = w0_ref[...] + lora[:, :rw]
    w_log = -(jnp.maximum(-xw, 0.0) + jnp.log1p(jnp.exp(-jnp.abs(xw)))) - 0.5
    lw = -jnp.exp(w_log)
    rate = jax.nn.sigmoid(a0_ref[...] + lora[:, rw:2 * rw])
    key = k * (1.0 + (rate - 1.0) * ka_ref[...])

    lw_hi, lw_lo = _split_bf16(lw)
    cl = jnp.dot(jnp.concatenate([cum, cum], axis=1), jnp.concatenate([lw_hi, lw_lo], axis=0),
                 preferred_element_type=F32)
    p = jnp.exp(cl)
    pinv = jnp.exp(-cl)
    at_all = -kk * jnp.exp(cl - lw)
    bt_all = kk * rate * pinv
    kt_all = key * pinv
    rt_all = r * p

    def stack(u):
        m = first_head if u.shape[1] == w2 else first_head2
        zero = jnp.zeros_like(u)
        return jnp.concatenate([jnp.where(m, u, zero), jnp.where(m, zero, u)], axis=0)

    def dotb(a, b, dn=_NN):
        return lax.dot_general(a, b, dn, preferred_element_type=F32)

    def pdot(mat, u):
        return dotb(mat, stack(u))

    units = [(j, q) for j in range(nsub) for q in pairs]
    cut = lambda u: [u[j * c:(j + 1) * c, sls[q]] for j, q in units]
    rt = cut(rt_all)
    at, bt, kt, rtb, v = [cut(u.astype(BF16)) for u in (at_all, bt_all, kt_all, rt_all, v_all)]
    us = range(len(units))

    a_ab, a_ak, m_rb, m_rk = [], [], [], []
    for u in us:
        lhs = jnp.concatenate([at[u], rtb[u]], axis=0)
        mbk = dotb(lhs, jnp.concatenate([stack(bt[u]), stack(kt[u])], axis=0), _NT)
        a_ab.append(jnp.where(strict, mbk[:c, :2 * c], 0.0).astype(BF16))
        m_rb.append(jnp.where(incl, mbk[c:, :2 * c], 0.0).astype(BF16))
        a_ak.append(jnp.where(strict, mbk[:c, 2 * c:], 0.0).astype(BF16))
        m_rk.append(jnp.where(incl, mbk[c:, 2 * c:], 0.0).astype(BF16))

    tinv = [eye_pair + a_ab[u].astype(F32) for u in us]
    ak = [pdot(a_ab[u], a_ab[u]).astype(BF16) for u in us]
    nsteps = max(1, (c - 1).bit_length()) - 1
    for s in range(nsteps):
        if s + 1 < nsteps:
            both = [pdot(ak[u], jnp.concatenate([ak[u], tinv[u].astype(BF16)], axis=1)) for u in us]
            ak = [both[u][:, :2 * c].astype(BF16) for u in us]
            tinv = [tinv[u] + both[u][:, 2 * c:] for u in us]
        else:
            tinv = [tinv[u] + pdot(ak[u], tinv[u].astype(BF16)) for u in us]
    xu0 = [pdot(a_ak[u], v[u]).astype(BF16) for u in us]
    x = [pdot(tinv[u].astype(BF16), jnp.concatenate([at[u], xu0[u]], axis=1)).astype(BF16)
         for u in us]

    zero_v = [jnp.zeros_like(v[u]) for u in us]
    xv = [jnp.concatenate([zero_v[u], v[u]], axis=1) for u in us]
    ry = [dotb(jnp.concatenate([m_rb[u], m_rk[u]], axis=1),
               jnp.concatenate([stack(x[u]), stack(xv[u])], axis=0)) for u in us]
    rbar = [rt[u] + ry[u][:, :w2] for u in us]
    y0 = [ry[u][:, w2:] for u in us]
    gh = [dotb(jnp.concatenate([x[u], xv[u]], axis=0), jnp.concatenate([bt[u], kt[u]], axis=0), _TN)
          for u in us]
    g = [jnp.where(same_head, gh[u][:w2], 0.0) for u in us]
    hh = [jnp.where(same_head, gh[u][w2:], 0.0) for u in us]

    ys = [None] * len(units)
    state = [s_scr[q] for q in pairs]
    for j in (reversed(range(nsub)) if reverse else range(nsub)):
        for q in pairs:
            u = j * npairs + q
            s0 = state[q]
            ys[u] = _dot1(rbar[u], s0, _NT) + y0[u]
            state[q] = (s0 + _dot1(s0, g[u]) + hh[u]) * p[j * c + last:j * c + last + 1, sls[q]]
    for q in pairs:
        s_scr[q] = state[q]
    y = jnp.concatenate([jnp.concatenate(ys[j * npairs:(j + 1) * npairs], axis=1)
                         for j in range(nsub)], axis=0)

    if reverse:
        y = yf_ref[0] + y
        inv_n = 1.0 / RWKV_HEAD_DIM
        rate_f = jax.nn.sigmoid(a0f_ref[...] + lora[:, 2 * rw:])
        k_mid = 0.5 * (k * (1.0 + (rate_f - 1.0) * ka_ref[...]) + key)
        sums = head_sum(jnp.concatenate([y, r * k_mid * rk_ref[...]], axis=0))
        dev = y - sums[:cb] * inv_n
        yn = dev * lax.rsqrt(head_sum(dev * dev) * inv_n + RWKV_GN_EPS)
        bonus = sums[cb:] * v_all
        gate = jnp.dot(jax.nn.sigmoid(gd).astype(BF16), gup_ref[...], preferred_element_type=F32)
        y = (yn * lng_ref[...] + lnb_ref[...] + bonus) * gate
    y_ref[0] = y.astype(y_ref.dtype)

    @pl.when(i == pl.num_programs(1) - 1)
    def _():
        sT_ref[0] = s_scr[...]


def _wkv_scan(z, p, d, s0, y_fwd=None):
    mu, w0, w_up, a0, a_up, g_up, k_k, k_a, r_k, lnx_g, lnx_b = p
    bsz, t, zw = z.shape
    w = k_k.shape[0]
    w2 = 2 * RWKV_HEAD_DIM
    npairs = w // w2
    c = WKV_CHUNK
    assert c == RWKV_HEAD_DIM and t % c == 0 and w % w2 == 0
    nsub = _chunks_per_step(t // c, WKV_CHUNKS_PER_STEP)
    cb = nsub * c
    nc = t // cb
    reverse = d == 1
    assert DECAY_LORA + AAA_LORA == w2 and (zw if not reverse else zw - w) == 3 * w + w2 + GATE_LORA
    halo = F32_SUBLANES
    per = cb // halo
    nhb = t // halo
    pos = (lambda i: nc - 1 - i) if reverse else (lambda i: i)
    tok = lambda bi, i: (bi, pos(i), 0)
    row1 = lambda u: u.reshape(1, -1)
    no_decay, no_rate = jnp.zeros((DECAY_LORA, w), F32), jnp.zeros((AAA_LORA, w), F32)
    blocks = [jnp.concatenate([w_up[d], no_rate], axis=0), jnp.concatenate([no_decay, a_up[d]], axis=0)]
    if reverse:
        blocks.append(jnp.concatenate([no_decay, a_up[0]], axis=0))
    lora_w = jnp.concatenate(blocks, axis=1).astype(BF16)
    full = lambda arr: pl.BlockSpec(arr.shape, lambda bi, i: (0, 0))
    st_spec = pl.BlockSpec((1, npairs, w2, w2), lambda bi, i: (bi, 0, 0, 0))
    y_spec = pl.BlockSpec((1, cb, w), tok)
    common = [row1(k_a), row1(w0[d]), row1(a0[d]), lora_w]
    out_specs = [y_spec, st_spec]
    out_shape = [jax.ShapeDtypeStruct((bsz, t, w), BF16 if reverse else F32),
                 jax.ShapeDtypeStruct((bsz, npairs, w2, w2), F32)]
    if reverse:
        extra = [row1(a0[0]), g_up.astype(BF16), row1(r_k), row1(lnx_g), row1(lnx_b)]
        args = [z] + common + [s0, y_fwd] + extra
        in_specs = ([pl.BlockSpec((1, cb, zw), tok)] + [full(a) for a in common] + [st_spec, y_spec]
                    + [full(a) for a in extra])
    else:
        head = [row1(mu), row1(k_k)]
        args = [z, z, z] + head + common + [s0]
        in_specs = [
            pl.BlockSpec((1, cb, zw), tok),
            pl.BlockSpec((1, halo, zw), lambda bi, i: (bi, jnp.maximum(pos(i) * per - 1, 0), 0)),
            pl.BlockSpec((1, halo, zw), lambda bi, i: (bi, jnp.minimum((pos(i) + 1) * per, nhb - 1), 0)),
        ] + [full(a) for a in head + common] + [st_spec]
        out_specs.insert(1, pl.BlockSpec((1, cb, zw + w), tok))
        out_shape.insert(1, jax.ShapeDtypeStruct((bsz, t, zw + w), F32))
    return pl.pallas_call(
        functools.partial(_wkv_kernel, reverse=reverse, npairs=npairs, chunk=c, nsub=nsub),
        grid=(bsz, nc),
        in_specs=in_specs,
        out_specs=out_specs,
        out_shape=out_shape,
        scratch_shapes=[pltpu.VMEM((npairs, w2, w2), F32)],
        compiler_params=pltpu.CompilerParams(
            dimension_semantics=("parallel", "arbitrary"),
            vmem_limit_bytes=VMEM_LIMIT_BYTES),
        name="wkv_bwd" if reverse else "wkv_fwd",
    )(*args)


def _ret_kernel(*refs, nheads, reverse, nsub):
    refs = list(refs)
    z_ref = refs.pop(0)
    din_ref, xi_ref, zeta_ref, g_ref, r0_ref = refs[:5]
    refs = refs[5:]
    yf_ref, ng_ref = (refs.pop(0), refs.pop(0)) if reverse else (None, None)
    y_ref, rT_ref, r_scr = refs
    d = RET_HEAD_DIM
    w = nheads * d
    i = pl.program_id(1)

    @pl.when(i == 0)
    def _():
        r_scr[...] = r0_ref[0]

    c = RET_CHUNK
    units = [(j, h) for j in range(nsub) for h in range(nheads)]
    us = range(len(units))
    rows = lambda j: slice(j * c, (j + 1) * c)
    cols = lambda g, h: slice(g * w + h * d, g * w + (h + 1) * d)

    q = [z_ref[0, rows(j), cols(0, h)] for j, h in units]
    k = [z_ref[0, rows(j), cols(1, h)] for j, h in units]
    v = [z_ref[0, rows(j), cols(2, h)] for j, h in units]
    s = [_dot1(q[u], k[u], _NT) * din_ref[units[u][1]] for u in us]
    sv = [_dot1(s[u], v[u]) for u in us]
    kv = [_dot1(k[u].astype(F32) * zeta_ref[units[u][1]], v[u], _TN) for u in us]

    ys = [None] * len(units)
    state = [r_scr[h] for h in range(nheads)]
    for j in (reversed(range(nsub)) if reverse else range(nsub)):
        for h in range(nheads):
            u = j * nheads + h
            ys[u] = sv[u] + _dot1(q[u], state[h]) * xi_ref[h]
            state[h] = state[h] * g_ref[h] + kv[u]
    for h in range(nheads):
        r_scr[h] = state[h]

    for u, (j, h) in enumerate(units):
        y = ys[u]
        if reverse:
            y = yf_ref[0, rows(j), cols(0, h)].astype(F32) + y
            dev = y - jnp.mean(y, axis=-1, keepdims=True)
            yn = dev * lax.rsqrt(jnp.mean(dev * dev, axis=-1, keepdims=True) + NORM_EPS)
            gt = z_ref[0, rows(j), cols(3, h)].astype(F32)
            y = gt * jax.nn.sigmoid(gt) * (yn * ng_ref[:, cols(0, h)])
        y_ref[0, rows(j), cols(0, h)] = y.astype(y_ref.dtype)

    @pl.when(i == pl.num_programs(1) - 1)
    def _():
        rT_ref[0] = r_scr[...]


def _ret_tables(log_gamma, reverse):
    c, d = RET_CHUNK, RET_HEAD_DIM
    idx = jnp.arange(c, dtype=F32)
    lg = log_gamma[:, None, None]
    if reverse:
        diff = idx[None, :] - idx[:, None]
        keep = diff > 0
        xi = jnp.exp(log_gamma[:, None] * (c - idx))
        zeta = jnp.exp(log_gamma[:, None] * idx)
    else:
        diff = idx[:, None] - idx[None, :]
        keep = diff >= 0
        xi = jnp.exp(log_gamma[:, None] * (idx + 1.0))
        zeta = jnp.exp(log_gamma[:, None] * (c - 1.0 - idx))
    din = jnp.where(keep, jnp.exp(lg * jnp.maximum(diff, 0.0)), 0.0)
    nh = log_gamma.shape[0]
    xi = jnp.broadcast_to(xi[:, :, None], (nh, c, d))
    zeta = jnp.broadcast_to(zeta[:, :, None], (nh, c, d))
    g = jnp.broadcast_to(jnp.exp(log_gamma * c)[:, None, None], (nh, d, d))
    return din, xi, zeta, g


def _ret_scan(z, r0, log_gamma, reverse, y_fwd=None, norm_g=None):
    bsz, t, zw = z.shape
    d = RET_HEAD_DIM
    nh = zw // (4 * d)
    w = nh * d
    c = RET_CHUNK
    assert t % c == 0
    nsub = _chunks_per_step(t // c, RET_CHUNKS_PER_STEP[1 if reverse else 0])
    cb = nsub * c
    nc = t // cb
    din, xi, zeta, g = _ret_tables(log_gamma, reverse)
    pos = (lambda i: nc - 1 - i) if reverse else (lambda i: i)
    tok = lambda bi, i: (bi, pos(i), 0)
    y_spec = pl.BlockSpec((1, cb, w), tok)
    st_spec = pl.BlockSpec((1, nh, d, d), lambda bi, i: (bi, 0, 0, 0))
    tab_spec = lambda n: pl.BlockSpec((nh, n, d), lambda bi, i: (0, 0, 0))
    args = [z]
    in_specs = [pl.BlockSpec((1, cb, zw), tok)]
    args += [din, xi, zeta, g, r0]
    in_specs += [tab_spec(c), tab_spec(c), tab_spec(c), tab_spec(d), st_spec]
    if reverse:
        args += [y_fwd, norm_g.reshape(1, w)]
        in_specs += [y_spec, pl.BlockSpec((1, w), lambda bi, i: (0, 0))]
    return pl.pallas_call(
        functools.partial(_ret_kernel, nheads=nh, reverse=reverse, nsub=nsub),
        grid=(bsz, nc),
        in_specs=in_specs,
        out_specs=[y_spec, st_spec],
        out_shape=[jax.ShapeDtypeStruct((bsz, t, w), BF16),
                   jax.ShapeDtypeStruct((bsz, nh, d, d), F32)],
        scratch_shapes=[pltpu.VMEM((nh, d, d), F32)],
        compiler_params=pltpu.CompilerParams(
            dimension_semantics=("parallel", "arbitrary"),
            vmem_limit_bytes=VMEM_LIMIT_BYTES),
        name="ret_bwd" if reverse else "ret_fwd",
    )(*args)


def _rwkv_mix(z, p, s0_f, s0_b, want_out):
    y_f, feat, s_f = _wkv_scan(z, p, 0, s0_f)
    out, s_b = _wkv_scan(feat, p, 1, s0_b, y_f)
    return (out if want_out else None), s_f, s_b


def _ret_mix(z, r0_f, r0_b, lg_f, lg_b, ret_norm_g, want_out):
    y_f, r_f = _ret_scan(z, r0_f, lg_f, False)
    out, r_b = _ret_scan(z, r0_b, lg_b, True, y_f, ret_norm_g)
    return (out if want_out else None), r_f, r_b


def kernel(x, c, ctx, c_ctx, mod_w, mod_b, norm1_g, norm2_g, w_in, tshift_mu, w0, w_up, a0, a_up, g_up,
           k_k, k_a, r_k, lnx_g, lnx_b, conv_w, ret_norm_g, w_out, ffn_up, ffn_conv, ffn_down,
           final_norm_g):
    bsz, n, dm = x.shape
    depth = mod_w.shape[0]
    rwkv_w = k_k.shape[1]
    rwkv_cols = tshift_mu.shape[1]
    conv_cols = 3 * conv_w.shape[1]
    ret_heads = ret_norm_g.shape[1] // RET_HEAD_DIM
    npairs = rwkv_w // (2 * RWKV_HEAD_DIM)

    rows = n // GRID_W
    row = jnp.repeat(jnp.arange(rows, dtype=F32), GRID_W)
    col = jnp.tile(jnp.arange(GRID_W, dtype=F32), rows)
    pairs = RET_HEAD_DIM // 4
    freqs = 1.0 / (ROPE_BASE ** (jnp.arange(pairs, dtype=F32) / pairs))
    ang = jnp.concatenate([row[:, None] * freqs, col[:, None] * freqs], axis=-1)
    cos, sin = jnp.cos(ang), jnp.sin(ang)
    rope = (jnp.concatenate([cos, cos], axis=-1), jnp.concatenate([-sin, sin], axis=-1))
    hidx = jnp.arange(ret_heads, dtype=F32)
    lg_f = jnp.log1p(-jnp.exp2(-5.0 - hidx))
    lg_b = jnp.log1p(-jnp.exp2(-5.5 - hidx))
    s_zero = jnp.zeros((bsz, npairs, 2 * RWKV_HEAD_DIM, 2 * RWKV_HEAD_DIM), F32)
    r_zero = jnp.zeros((bsz, ret_heads, RET_HEAD_DIM, RET_HEAD_DIM), F32)

    xl = x.astype(F32)
    xc = ctx.astype(F32)
    silu_all = jnp.concatenate([jax.nn.silu(c.astype(F32)), jax.nn.silu(c_ctx.astype(F32))[None]], axis=0)

    w_in_b, w_out_b, up_b, down_b = (w.astype(BF16) for w in (w_in, w_out, ffn_up, ffn_down))
    widths = (rwkv_cols, conv_cols, w_in.shape[2] - rwkv_cols - conv_cols)
    z_dtypes = (F32, BF16, BF16)
    bc = lambda m: jnp.broadcast_to(m, (bsz, 1, dm))

    for l in range(depth):
        last = l == depth - 1
        mod = _mm_rows(silu_all, mod_w, l) + mod_b[l]
        ml = [m[:, None, :] for m in jnp.split(mod[:bsz], 6, axis=-1)]
        mc = jnp.split(mod[bsz], 6, axis=-1)
        rwkv_p = (tshift_mu[l], w0[l], w_up[l], a0[l], a_up[l], g_up[l], k_k[l], k_a[l],
                  r_k[l], lnx_g[l], lnx_b[l])

        gm_c, sh_c = bc(norm1_g[l] * (1.0 + mc[1])), bc(mc[0])
        gm_l, sh_l = norm1_g[l] * (1.0 + ml[1]), ml[0]
        ret_w = ret_heads * RET_HEAD_DIM
        zc_rw, zc_cv, zc_rt = _mm_norm(xc, gm_c, sh_c, w_in_b, l, widths, z_dtypes, ret_w)
        zl_rw, zl_cv, zl_rt = _mm_norm(xl, gm_l, sh_l, w_in_b, l, widths, z_dtypes, ret_w, rope)

        out_c_rw, s_f, s_b = _rwkv_mix(zc_rw, rwkv_p, s_zero, s_zero, not last)
        out_l_rw, _, _ = _rwkv_mix(zl_rw, rwkv_p, s_f, s_b, True)
        out_c_rt, r_f, r_b = _ret_mix(zc_rt, r_zero, r_zero, lg_f, lg_b, ret_norm_g[l], not last)
        out_l_rt, _, _ = _ret_mix(zl_rt, r_f, r_b, lg_f, lg_b, ret_norm_g[l], True)

        xl = _mix_out(out_l_rw, zl_cv, out_l_rt, xl, ml[2], w_out_b, l, conv_w[l])
        xl = _ffn(xl, norm2_g[l] * (1.0 + ml[4]), ml[3], ml[5], up_b, l, ffn_conv[l], down_b,
                  final_norm_g if last else None)
        if not last:
            xc = _mix_out(out_c_rw, zc_cv, out_c_rt, xc, bc(mc[2]), w_out_b, l, conv_w[l])
            xc = _ffn(xc, bc(norm2_g[l] * (1.0 + mc[4])), bc(mc[3]), bc(mc[5]), up_b, l, ffn_conv[l],
                      down_b)

    return xl.astype(x.dtype)
```
